```python
import math
import jax, jax.numpy as jnp
from jax import lax
import numpy as np

D_MODEL = 2048
BATCH = 4
SEQ = 2048
DEPTH = 1
DEC_BATCH = 128
DEC_SEQ = 1
PAST_LEN = 2048
PAGE_SIZE = 128

N_HEADS = 16
HEAD_DIM = D_MODEL // N_HEADS
ATTN_WIDTH = N_HEADS * HEAD_DIM
ROT_DIM = HEAD_DIM // 4
ROPE_THETA = 500000.0
MOBA_BLOCK = 256
MOBA_TOPK = 3
QUERY_ROWS = 128
SSM_WIDTH = D_MODEL
SSM_GROUP = 16
SSM_GROUPS = SSM_WIDTH // SSM_GROUP
SSM_STATE = 64
D_FF = 4 * D_MODEL
N_BRANCH = 2
IN_COLS = 3 * ATTN_WIDTH + SSM_WIDTH + N_BRANCH * D_MODEL
RMS_EPS = 1e-6
NEG = -1e30

kernel_name = "hybrid_s5_moba_decode_step"


def rmsnorm(x, g):
    xf = x.astype(jnp.float32)
    y = xf * lax.rsqrt(jnp.mean(xf * xf, axis=-1, keepdims=True) + RMS_EPS) * g.astype(jnp.float32)
    return y.astype(x.dtype)


def partial_rope(x, pos):
    half = ROT_DIM // 2
    inv = ROPE_THETA ** (-jnp.arange(half, dtype=jnp.float32) / half)
    ang = pos.astype(jnp.float32)[:, None] * inv[None, :]
    cos = jnp.cos(ang)[None, :, None, :]
    sin = jnp.sin(ang)[None, :, None, :]
    xr = x[..., :ROT_DIM].astype(jnp.float32)
    x1, x2 = xr[..., :half], xr[..., half:]
    rot = jnp.concatenate([x1 * cos - x2 * sin, x2 * cos + x1 * sin], axis=-1).astype(x.dtype)
    return jnp.concatenate([rot, x[..., ROT_DIM:]], axis=-1)


def _query_chunk(b, s):
    cap = max(1, QUERY_ROWS // b)
    return max(d for d in range(1, min(cap, s) + 1) if s % d == 0)


def moba_attention(q, k, v, q_pos):
    B, Sq = q.shape[0], q.shape[1]
    T = k.shape[1]
    pad = (-T) % MOBA_BLOCK
    if pad:
        k = jnp.pad(k, ((0, 0), (0, pad), (0, 0), (0, 0)))
        v = jnp.pad(v, ((0, 0), (0, pad), (0, 0), (0, 0)))
    nb = (T + pad) // MOBA_BLOCK
    kb = k.reshape(B, nb, MOBA_BLOCK, N_HEADS, HEAD_DIM)
    vb = v.reshape(B, nb, MOBA_BLOCK, N_HEADS, HEAD_DIM)
    k_mean = jnp.mean(kb.astype(jnp.float32), axis=2)
    n_sel = min(MOBA_TOPK, nb)
    qc = _query_chunk(B, Sq)
    n_chunks = Sq // qc
    scale = HEAD_DIM ** -0.5
    b_idx = jnp.arange(B)[:, None, None, None]
    h_idx = jnp.arange(N_HEADS)[None, None, :, None]
    blk_ids = jnp.arange(nb)
    offs = jnp.arange(MOBA_BLOCK)

    def chunk_fn(args):
        qch, pch = args
        qf = qch.astype(jnp.float32)
        n_past = pch // MOBA_BLOCK
        gate = jnp.einsum('bqhd,bnhd->bqhn', qf, k_mean)
        past_ok = blk_ids[None, :] < n_past[:, None]
        gate = jnp.where(past_ok[None, :, None, :], gate, NEG)
        _, top_idx = lax.top_k(gate, n_sel)
        own = jnp.broadcast_to(n_past[None, :, None, None], (B, qc, N_HEADS, 1))
        idx = jnp.concatenate([top_idx, own.astype(top_idx.dtype)], axis=-1)
        valid = jnp.concatenate([jnp.arange(n_sel)[None, :] < n_past[:, None],
                                 jnp.ones((qc, 1), dtype=bool)], axis=-1)
        kpos = idx[..., None] * MOBA_BLOCK + offs
        mask = valid[None, :, None, :, None] & (kpos <= pch[None, :, None, None, None])
        kg = kb[b_idx, idx, :, h_idx].astype(jnp.float32)
        vg = vb[b_idx, idx, :, h_idx].astype(jnp.float32)
        s = jnp.einsum('bqhd,bqhnkd->bqhnk', qf, kg) * scale
        s = jnp.where(mask, s, NEG)
        p = jax.nn.softmax(s.reshape(B, qc, N_HEADS, -1), axis=-1).reshape(s.shape)
        out = jnp.einsum('bqhnk,bqhnkd->bqhd', p, vg)
        return out.astype(q.dtype)

    q_chunks = q.reshape(B, n_chunks, qc, N_HEADS, HEAD_DIM).transpose(1, 0, 2, 3, 4)
    p_chunks = q_pos.reshape(n_chunks, qc)
    out = lax.map(chunk_fn, (q_chunks, p_chunks))
    return out.transpose(1, 0, 2, 3, 4).reshape(B, Sq, ATTN_WIDTH)


def s5_scan(u, h0_re, h0_im, lambda_re, lambda_im, log_step, b_re, b_im, c_re, c_im, d_skip):
    Bsz, S = u.shape[0], u.shape[1]
    uf = u.astype(jnp.float32).reshape(Bsz, S, SSM_GROUPS, SSM_GROUP)
    step = jnp.exp(log_step.astype(jnp.float32))[:, None]
    lr = lambda_re.astype(jnp.float32)
    li = lambda_im.astype(jnp.float32)
    mag = jnp.exp(lr * step)
    ang = li * step
    ab_re, ab_im = mag * jnp.cos(ang), mag * jnp.sin(ang)
    den = lr * lr + li * li
    nr, ni = ab_re - 1.0, ab_im
    f_re = (nr * lr + ni * li) / den
    f_im = (ni * lr - nr * li) / den
    br, bi = b_re.astype(jnp.float32), b_im.astype(jnp.float32)
    bb_re = f_re[..., None] * br - f_im[..., None] * bi
    bb_im = f_re[..., None] * bi + f_im[..., None] * br
    x_re = jnp.einsum('bsgc,gnc->bsgn', uf, bb_re)
    x_im = jnp.einsum('bsgc,gnc->bsgn', uf, bb_im)
    h0r = h0_re.astype(jnp.float32)
    h0i = h0_im.astype(jnp.float32)
    x_re = x_re.at[:, 0].add(ab_re * h0r - ab_im * h0i)
    x_im = x_im.at[:, 0].add(ab_re * h0i + ab_im * h0r)
    a_re = jnp.broadcast_to(ab_re, x_re.shape)
    a_im = jnp.broadcast_to(ab_im, x_im.shape)

    def combine(e1, e2):
        a1r, a1i, b1r, b1i = e1
        a2r, a2i, b2r, b2i = e2
        return (a2r * a1r - a2i * a1i, a2r * a1i + a2i * a1r,
                a2r * b1r - a2i * b1i + b2r, a2r * b1i + a2i * b1r + b2i)

    _, _, h_re, h_im = lax.associative_scan(combine, (a_re, a_im, x_re, x_im), axis=1)
    y = (jnp.einsum('bsgn,gcn->bsgc', h_re, c_re.astype(jnp.float32))
         - jnp.einsum('bsgn,gcn->bsgc', h_im, c_im.astype(jnp.float32))
         + d_skip.astype(jnp.float32).reshape(SSM_GROUPS, SSM_GROUP) * uf)
    return y.reshape(Bsz, S, SSM_WIDTH).astype(u.dtype), h_re[:, -1], h_im[:, -1]


def hybrid_layer(x, pos, k_past, v_past, h0_re, h0_im, norm_mix, w_in, lambda_re, lambda_im,
                 log_step, b_re, b_im, c_re, c_im, d_skip, w_glu_v, w_glu_g, w_out,
                 norm_ffn, w_up, w_down):
    B, S = x.shape[0], x.shape[1]
    h = rmsnorm(x, norm_mix)
    proj = h @ w_in
    q, k, v, u, g = jnp.split(proj, [ATTN_WIDTH, 2 * ATTN_WIDTH, 3 * ATTN_WIDTH,
                                     3 * ATTN_WIDTH + SSM_WIDTH], axis=-1)
    q = partial_rope(q.reshape(B, S, N_HEADS, HEAD_DIM), pos)
    k = partial_rope(k.reshape(B, S, N_HEADS, HEAD_DIM), pos)
    v = v.reshape(B, S, N_HEADS, HEAD_DIM)
    if k_past is None:
        k_all, v_all = k, v
    else:
        k_all = jnp.concatenate([k_past, k], axis=1)
        v_all = jnp.concatenate([v_past, v], axis=1)
    attn_out = moba_attention(q, k_all, v_all, pos)
    y_ssm, hT_re, hT_im = s5_scan(u, h0_re, h0_im, lambda_re, lambda_im, log_step,
                                  b_re, b_im, c_re, c_im, d_skip)
    z = jax.nn.gelu(y_ssm)
    ssm_out = (z @ w_glu_v) * jax.nn.sigmoid(z @ w_glu_g)
    gates = jax.nn.sigmoid(g.astype(jnp.float32)).astype(x.dtype)
    g_attn, g_ssm = gates[..., :D_MODEL], gates[..., D_MODEL:]
    x = x + (g_attn * attn_out + g_ssm * ssm_out) @ w_out
    h2 = rmsnorm(x, norm_ffn)
    x = x + jnp.square(jax.nn.relu(h2 @ w_up)) @ w_down
    return x, k, v, hT_re, hT_im


def setup_inputs(seed: int = 0) -> dict:
    key = jax.random.key(seed)
    ks = jax.random.split(key, 24)
    f32 = jnp.float32
    n_pages = PAST_LEN // PAGE_SIZE
    n_used = DEC_BATCH * n_pages
    n_pool = n_used + max(1, n_used // 4)
    perm = jax.random.permutation(ks[0], n_pool)
    page_table = perm[:n_used].reshape(DEC_BATCH, n_pages).astype(jnp.int32)
    nrm = lambda k, shp, s=1.0: jax.random.normal(k, shp, f32) * s
    lam_im = (math.pi * jnp.arange(SSM_STATE, dtype=f32))[None, None, :] + nrm(ks[9], (DEPTH, SSM_GROUPS, SSM_STATE), 0.01)
    return {
        "x_prompt": nrm(ks[1], (BATCH, SEQ, D_MODEL)),
        "x_sample": nrm(ks[2], (DEC_BATCH, DEC_SEQ, D_MODEL)),
        "cache_k": nrm(ks[3], (DEPTH, n_pool, PAGE_SIZE, N_HEADS, HEAD_DIM)),
        "cache_v": nrm(ks[4], (DEPTH, n_pool, PAGE_SIZE, N_HEADS, HEAD_DIM)),
        "state_ssm_re": nrm(ks[5], (DEPTH, DEC_BATCH, SSM_GROUPS, SSM_STATE), 0.5),
        "state_ssm_im": nrm(ks[6], (DEPTH, DEC_BATCH, SSM_GROUPS, SSM_STATE), 0.5),
        "page_table": page_table,
        "norm_mix": 1.0 + nrm(ks[7], (DEPTH, D_MODEL), 0.02),
        "w_in": nrm(ks[8], (DEPTH, D_MODEL, IN_COLS), D_MODEL ** -0.5),
        "lambda_re": -0.5 + nrm(ks[10], (DEPTH, SSM_GROUPS, SSM_STATE), 0.01),
        "lambda_im": lam_im,
        "log_step": jax.random.uniform(ks[11], (DEPTH, SSM_GROUPS), f32, math.log(1e-3), math.log(1e-1)),
        "b_re": nrm(ks[12], (DEPTH, SSM_GROUPS, SSM_STATE, SSM_GROUP), (2 * SSM_GROUP) ** -0.5),
        "b_im": nrm(ks[13], (DEPTH, SSM_GROUPS, SSM_STATE, SSM_GROUP), (2 * SSM_GROUP) ** -0.5),
        "c_re": nrm(ks[14], (DEPTH, SSM_GROUPS, SSM_GROUP, SSM_STATE), (2 * SSM_STATE) ** -0.5),
        "c_im": nrm(ks[15], (DEPTH, SSM_GROUPS, SSM_GROUP, SSM_STATE), (2 * SSM_STATE) ** -0.5),
        "d_skip": nrm(ks[16], (DEPTH, SSM_WIDTH)),
        "w_glu_v": nrm(ks[17], (DEPTH, SSM_WIDTH, D_MODEL), SSM_WIDTH ** -0.5),
        "w_glu_g": nrm(ks[18], (DEPTH, SSM_WIDTH, D_MODEL), SSM_WIDTH ** -0.5),
        "w_out": nrm(ks[19], (DEPTH, D_MODEL, D_MODEL), D_MODEL ** -0.5),
        "norm_ffn": 1.0 + nrm(ks[20], (DEPTH, D_MODEL), 0.02),
        "w_up": nrm(ks[21], (DEPTH, D_MODEL, D_FF), D_MODEL ** -0.5),
        "w_down": nrm(ks[22], (DEPTH, D_FF, D_MODEL), D_FF ** -0.5),
        "norm_final": 1.0 + nrm(ks[23], (D_MODEL,), 0.02),
    }


def reference(x_prompt, x_sample, cache_k, cache_v, state_ssm_re, state_ssm_im, page_table,
              norm_mix, w_in, lambda_re, lambda_im, log_step, b_re, b_im, c_re, c_im, d_skip,
              w_glu_v, w_glu_g, w_out, norm_ffn, w_up, w_down, norm_final):
    n_pages = page_table.shape[1]
    past_len = n_pages * PAGE_SIZE
    n_dec, s_dec = x_sample.shape[0], x_sample.shape[1]
    pos_prompt = jnp.arange(x_prompt.shape[1], dtype=jnp.int32)
    pos_sample = past_len + jnp.arange(s_dec, dtype=jnp.int32)
    xp, xs = x_prompt, x_sample
    kp_l, vp_l, hpr_l, hpi_l, ks_l, vs_l, hsr_l, hsi_l = [], [], [], [], [], [], [], []
    for l in range(DEPTH):
        w = (norm_mix[l], w_in[l], lambda_re[l], lambda_im[l], log_step[l], b_re[l], b_im[l],
             c_re[l], c_im[l], d_skip[l], w_glu_v[l], w_glu_g[l], w_out[l], norm_ffn[l],
             w_up[l], w_down[l])
        h0 = jnp.zeros((xp.shape[0], SSM_GROUPS, SSM_STATE), jnp.float32)
        xp, kp, vp, hpr, hpi = hybrid_layer(xp, pos_prompt, None, None, h0, h0, *w)
        k_past = cache_k[l][page_table].reshape(n_dec, past_len, N_HEADS, HEAD_DIM)
        v_past = cache_v[l][page_table].reshape(n_dec, past_len, N_HEADS, HEAD_DIM)
        xs, ksn, vsn, hsr, hsi = hybrid_layer(xs, pos_sample, k_past, v_past,
                                             state_ssm_re[l], state_ssm_im[l], *w)
        kp_l.append(kp); vp_l.append(vp); hpr_l.append(hpr); hpi_l.append(hpi)
        ks_l.append(ksn); vs_l.append(vsn); hsr_l.append(hsr); hsi_l.append(hsi)
    y_prompt = rmsnorm(xp, norm_final)
    y_sample = rmsnorm(xs, norm_final)
    k_prompt = jnp.stack(kp_l)
    v_prompt = jnp.stack(vp_l)
    ssm_re_prompt = jnp.stack(hpr_l).astype(x_prompt.dtype)
    ssm_im_prompt = jnp.stack(hpi_l).astype(x_prompt.dtype)
    k_sample = jnp.stack(ks_l)
    v_sample = jnp.stack(vs_l)
    ssm_re_sample = jnp.stack(hsr_l).astype(x_sample.dtype)
    ssm_im_sample = jnp.stack(hsi_l).astype(x_sample.dtype)
    return (y_prompt, y_sample, k_prompt, v_prompt, ssm_re_prompt, ssm_im_prompt,
            k_sample, v_sample, ssm_re_sample, ssm_im_sample)
```

```python
import functools
import math

import jax
import jax.numpy as jnp
from jax import lax
from jax.experimental import pallas as pl
from jax.experimental.pallas import tpu as pltpu

N_HEADS = 16
HEAD_DIM = 128
ROT_DIM = HEAD_DIM // 4
ROT_HALF = ROT_DIM // 2
ROPE_THETA = 500000.0
MOBA_BLOCK = 256
MOBA_TOPK = 3
SSM_GROUP = 16
SSM_STATE = 64
RMS_EPS = 1e-6
NEG = -1e30

LANES = 128
SUBLANES = 8
MXU_DIM = 256
VMEM_LIMIT_BYTES = 56 * 1024 * 1024

SSM_CH_TILE = MXU_DIM
SSM_GROUPS_PER_TILE = SSM_CH_TILE // SSM_GROUP
SSM_STATE_TILE = SSM_GROUPS_PER_TILE * SSM_STATE
SSM_STATE_ROWS = SSM_STATE_TILE // LANES

F32 = jnp.float32
BF16 = jnp.bfloat16


def _cparams(*sem):
    return pltpu.CompilerParams(dimension_semantics=sem, vmem_limit_bytes=VMEM_LIMIT_BYTES)


def _sigmoid(x):
    return 1.0 / (1.0 + jnp.exp(-x))


def _gelu_tanh(x):
    c = math.sqrt(2.0 / math.pi)
    return 0.5 * x * (1.0 + jnp.tanh(c * (x + 0.044715 * (x * x * x))))


def _rms_scale(x):
    return x * lax.rsqrt(jnp.mean(x * x, axis=-1, keepdims=True) + RMS_EPS)


def _rope_table_kernel(inv_ref, cos_ref, sin_a_ref, sin_b_ref):
    rows = cos_ref.shape[0]
    pos = lax.broadcasted_iota(jnp.int32, (rows, LANES), 0).astype(F32)
    lane = lax.broadcasted_iota(jnp.int32, (rows, LANES), 1)
    ang = pos * inv_ref[...]
    c = jnp.cos(ang)
    s = jnp.sin(ang)
    cos_ref[...] = c
    sin_a_ref[...] = jnp.where((lane >= ROT_HALF) & (lane < ROT_DIM), s, 0.0)
    sin_b_ref[...] = jnp.where(lane < ROT_HALF, -s, 0.0)


def _rope_tables(n_pos):
    rows = -(-n_pos // SUBLANES) * SUBLANES
    inv = ROPE_THETA ** (-jnp.arange(ROT_HALF, dtype=F32) / ROT_HALF)
    inv_row = jnp.concatenate([inv, inv, jnp.zeros((LANES - ROT_DIM,), F32)])[None, :]
    out = jax.ShapeDtypeStruct((rows, LANES), F32)
    return pl.pallas_call(_rope_table_kernel, out_shape=(out, out, out), name="rope_tables")(inv_row)


def _ssm_prep_kernel(ls_ref, lr_ref, li_ref, lrr_ref, lir_ref, br_ref, bi_ref,
                     abr_ref, abi_ref, bbr_ref, bbi_ref):
    step = jnp.exp(ls_ref[...])

    def disc(lr, li):
        mag = jnp.exp(lr * step)
        ang = li * step
        ab_re, ab_im = mag * jnp.cos(ang), mag * jnp.sin(ang)
        den = lr * lr + li * li
        nr, ni = ab_re - 1.0, ab_im
        f_re = (nr * lr + ni * li) / den
        f_im = (ni * lr - nr * li) / den
        return ab_re, ab_im, f_re, f_im

    ab_re, ab_im, _, _ = disc(lr_ref[...], li_ref[...])
    abr_ref[...] = ab_re
    abi_ref[...] = ab_im
    _, _, f_re, f_im = disc(lrr_ref[...], lir_ref[...])
    br, bi = br_ref[...], bi_ref[...]
    bbr_ref[...] = f_re * br - f_im * bi
    bbi_ref[...] = f_re * bi + f_im * br


def _ssm_prep(lambda_re, lambda_im, log_step, b_re, b_im):
    g, n = lambda_re.shape
    flat = g, n * SSM_GROUP
    outs = (jax.ShapeDtypeStruct((g, n), F32),) * 2 + (jax.ShapeDtypeStruct(flat, F32),) * 2
    return pl.pallas_call(_ssm_prep_kernel, out_shape=outs, name="ssm_prep")(
        log_step.reshape(g, 1), lambda_re, lambda_im,
        jnp.repeat(lambda_re, SSM_GROUP, axis=1), jnp.repeat(lambda_im, SSM_GROUP, axis=1),
        b_re.reshape(flat), b_im.reshape(flat))


def _block_diag_tiles(w):
    g, r, c = w.shape
    t = g // SSM_GROUPS_PER_TILE
    w = w.reshape(t, SSM_GROUPS_PER_TILE, r, 1, c)
    eye = jnp.eye(SSM_GROUPS_PER_TILE, dtype=w.dtype)[None, :, None, :, None]
    return (w * eye).reshape(t, SSM_GROUPS_PER_TILE * r, SSM_GROUPS_PER_TILE * c).astype(BF16)


def _norm_proj_kernel(*refs, mode):
    if mode == "rope":
        x_ref, g_ref, w_ref, cos_ref, sa_ref, sb_ref, o_ref, hn_ref = refs
    else:
        x_ref, g_ref, w_ref, o_ref, hn_ref = refs

    @pl.when(pl.program_id(1) == 0)
    def _():
        hn_ref[...] = (_rms_scale(x_ref[...]) * g_ref[...]).astype(BF16)

    acc = jnp.dot(hn_ref[...], w_ref[...].astype(BF16), preferred_element_type=F32)
    if mode == "rope":
        cos, sa, sb = cos_ref[...], sa_ref[...], sb_ref[...]
        for h in range(acc.shape[1] // HEAD_DIM):
            xh = acc[:, h * HEAD_DIM:(h + 1) * HEAD_DIM]
            o_ref[:, h * HEAD_DIM:(h + 1) * HEAD_DIM] = (
                xh * cos + pltpu.roll(xh, ROT_HALF, 1) * sa
                + pltpu.roll(xh, HEAD_DIM - ROT_HALF, 1) * sb)
    elif mode == "sigmoid":
        o_ref[...] = _sigmoid(acc)
    else:
        o_ref[...] = acc


def _norm_proj(x, gain, w, col0, n_cols, mode, tm, tn, rope=None, rope_blocks=1):
    m, d = x.shape
    cb0 = col0 // tn
    in_specs = [pl.BlockSpec((tm, d), lambda i, j: (i, 0)),
                pl.BlockSpec((1, d), lambda i, j: (0, 0)),
                pl.BlockSpec((d, tn), lambda i, j: (0, cb0 + j))]
    args = [x, gain, w]
    if mode == "rope":
        in_specs += [pl.BlockSpec((tm, LANES), lambda i, j: (i % rope_blocks, 0))] * 3
        args += list(rope)
    return pl.pallas_call(
        functools.partial(_norm_proj_kernel, mode=mode),
        out_shape=jax.ShapeDtypeStruct((m, n_cols), F32),
        grid=(m // tm, n_cols // tn),
        in_specs=in_specs,
        out_specs=pl.BlockSpec((tm, tn), lambda i, j: (i, j)),
        scratch_shapes=[pltpu.VMEM((tm, d), BF16)],
        compiler_params=_cparams("parallel", "arbitrary"),
        name="norm_proj_" + mode,
    )(*args)


def _topk_selected(gates, i):
    return _topk_rank(gates, i) < float(MOBA_TOPK)


def _topk_rank(gates, i):
    gi = gates[i]
    rank = jnp.zeros(gi.shape, F32)
    for i2, g2 in enumerate(gates):
        if i2 == i:
            continue
        ahead = (g2 >= gi) if i2 < i else (g2 > gi)
        rank = rank + jnp.where(ahead, 1.0, 0.0)
    return rank


def _moba_prompt_kernel(q_ref, k_ref, v_ref, o_ref):
    seq = q_ref.shape[1]
    nb = seq // MOBA_BLOCK
    scale = HEAD_DIM ** -0.5
    q, k, v = q_ref[0], k_ref[0], v_ref[0]
    k_mean = jnp.concatenate(
        [jnp.mean(k[i * MOBA_BLOCK:(i + 1) * MOBA_BLOCK], axis=0, keepdims=True) for i in range(nb)],
        axis=0)
    gate = lax.dot_general(q, k_mean, (((1,), (1,)), ((), ())),
                           precision=lax.Precision.HIGHEST, preferred_element_type=F32)
    qb, kb, vb = q.astype(BF16), k.astype(BF16), v.astype(BF16)
    row = lax.broadcasted_iota(jnp.int32, (MOBA_BLOCK, MOBA_BLOCK), 0)
    col = lax.broadcasted_iota(jnp.int32, (MOBA_BLOCK, MOBA_BLOCK), 1)
    causal = col <= row
    for j in range(nb):
        rows = slice(j * MOBA_BLOCK, (j + 1) * MOBA_BLOCK)
        n_keys = (j + 1) * MOBA_BLOCK
        s = lax.dot_general(qb[rows], kb[:n_keys], (((1,), (1,)), ((), ())),
                            preferred_element_type=F32) * scale
        gates = [gate[rows, i:i + 1] for i in range(j)]
        pieces = []
        for i in range(j):
            s_i = s[:, i * MOBA_BLOCK:(i + 1) * MOBA_BLOCK]
            if j > MOBA_TOPK:
                rank = jnp.broadcast_to(_topk_rank(gates, i), s_i.shape)
                s_i = jnp.where(rank < float(MOBA_TOPK), s_i, NEG)
            pieces.append(s_i)
        pieces.append(jnp.where(causal, s[:, j * MOBA_BLOCK:], NEG))
        s = jnp.concatenate(pieces, axis=1)
        p = jnp.exp(s - jnp.max(s, axis=-1, keepdims=True))
        l = jnp.sum(p, axis=-1, keepdims=True)
        o = jnp.dot(p.astype(BF16), vb[:n_keys], preferred_element_type=F32)
        o_ref[0, rows, :] = o / l


def _moba_prompt(q, k, v):
    b, s, _ = q.shape
    spec = pl.BlockSpec((1, s, HEAD_DIM), lambda bi, h: (bi, 0, h))
    return pl.pallas_call(
        _moba_prompt_kernel,
        out_shape=jax.ShapeDtypeStruct(q.shape, F32),
        grid=(b, N_HEADS),
        in_specs=[spec, spec, spec],
        out_specs=spec,
        compiler_params=_cparams("parallel", "parallel"),
        name="moba_prompt",
    )(q, k, v)


def _head_fold(x):
    rows = lax.broadcasted_iota(jnp.int32, (N_HEADS, HEAD_DIM), 0)
    out = jnp.zeros((N_HEADS, HEAD_DIM), F32)
    for h in range(N_HEADS):
        out = out + jnp.where(rows == h, x[:, h * HEAD_DIM:(h + 1) * HEAD_DIM], 0.0)
    return out


def _moba_decode_kernel(pt_ref, q_ref, kn_ref, vn_ref, k0_ref, k1_ref, v0_ref, v1_ref, o_ref,
                        g_s, m_s, l_s, o_s):
    del pt_ref
    blk = pl.program_id(1)
    n_blk = pl.num_programs(1)
    scale = HEAD_DIM ** -0.5
    width = N_HEADS * HEAD_DIM
    q = q_ref[0]
    head_of_col = lax.broadcasted_iota(jnp.int32, (N_HEADS, width), 1) // HEAD_DIM
    head_of_row = lax.broadcasted_iota(jnp.int32, (N_HEADS, width), 0)
    own = head_of_col == head_of_row
    q_bd = jnp.where(own, jnp.broadcast_to(q, (N_HEADS, width)), 0.0)
    q_bd16 = q_bd.astype(BF16)

    k_blk = jnp.concatenate([k0_ref[0], k1_ref[0]], axis=0)
    v_blk = jnp.concatenate([v0_ref[0], v1_ref[0]], axis=0)
    k_mean = jnp.mean(k_blk, axis=0, keepdims=True)
    gate = jnp.sum(q_bd * k_mean, axis=-1, keepdims=True)
    s = lax.dot_general(q_bd16, k_blk.astype(BF16), (((1,), (1,)), ((), ())),
                        preferred_element_type=F32) * scale
    m = jnp.max(s, axis=-1, keepdims=True)
    p = jnp.exp(s - m)
    l = jnp.sum(p, axis=-1, keepdims=True)
    o_full = jnp.dot(p.astype(BF16), v_blk.astype(BF16), preferred_element_type=F32)
    lanes = (N_HEADS, HEAD_DIM)
    g_s[blk] = jnp.broadcast_to(gate, lanes)
    m_s[blk] = jnp.broadcast_to(m, lanes)
    l_s[blk] = jnp.broadcast_to(l, lanes)
    o_s[blk] = _head_fold(o_full)

    @pl.when(blk == n_blk - 1)
    def _():
        nb = g_s.shape[0]
        kn = kn_ref[0].astype(BF16).astype(F32)
        vn = vn_ref[0].astype(BF16).astype(F32)
        s_own = jnp.sum(q_bd16.astype(F32) * kn, axis=-1, keepdims=True) * scale
        s_own = jnp.broadcast_to(s_own, lanes)
        v_own = _head_fold(jnp.broadcast_to(vn, (N_HEADS, width)))
        gates = [g_s[i] for i in range(nb)]
        sel = [_topk_selected(gates, i) for i in range(nb)]
        m_all = s_own
        for i in range(nb):
            m_all = jnp.maximum(m_all, jnp.where(sel[i], m_s[i], NEG))
        w_own = jnp.exp(s_own - m_all)
        den = w_own
        num = w_own.astype(BF16).astype(F32) * v_own
        for i in range(nb):
            w = jnp.where(sel[i], jnp.exp(m_s[i] - m_all), 0.0)
            den = den + w * l_s[i]
            num = num + w * o_s[i]
        o_ref[0] = num / den


def _moba_decode(q, k_new, v_new, cache_k, cache_v, page_table):
    n_dec, width = q.shape
    n_pool, page = cache_k.shape[0], cache_k.shape[1]
    n_pages = page_table.shape[1]
    pages_per_blk = MOBA_BLOCK // page
    assert pages_per_blk == 2 and (n_pages * page) % MOBA_BLOCK == 0
    nb = n_pages // pages_per_blk
    ck = cache_k.reshape(n_pool, page, width)
    cv = cache_v.reshape(n_pool, page, width)
    row = pl.BlockSpec((1, 1, width), lambda b, i, pt: (b, 0, 0))
    pg0 = pl.BlockSpec((1, page, width), lambda b, i, pt: (pt[b, 2 * i], 0, 0))
    pg1 = pl.BlockSpec((1, page, width), lambda b, i, pt: (pt[b, 2 * i + 1], 0, 0))
    stat = pltpu.VMEM((nb, N_HEADS, HEAD_DIM), F32)
    out = pl.pallas_call(
        _moba_decode_kernel,
        out_shape=jax.ShapeDtypeStruct((n_dec, N_HEADS, HEAD_DIM), F32),
        grid_spec=pltpu.PrefetchScalarGridSpec(
            num_scalar_prefetch=1,
            grid=(n_dec, nb),
            in_specs=[row, row, row, pg0, pg1, pg0, pg1],
            out_specs=pl.BlockSpec((1, N_HEADS, HEAD_DIM), lambda b, i, pt: (b, 0, 0)),
            scratch_shapes=[stat, stat, stat, stat]),
        compiler_params=_cparams("parallel", "arbitrary"),
        name="moba_decode",
    )(page_table, q.reshape(n_dec, 1, width), k_new.reshape(n_dec, 1, width),
      v_new.reshape(n_dec, 1, width), ck, ck, cv, cv)
    return out.reshape(n_dec, width)


def _s5_prompt_kernel(u_ref, wbr_ref, wbi_ref, wcr_ref, wci_ref, ar_ref, ai_ref, d_ref,
                      z_ref, hr_out, hi_out, xr_s, xi_s, hr_s, hi_s):
    c = pl.program_id(1)
    n_b, t_c = u_ref.shape[0], u_ref.shape[1]

    @pl.when(c == 0)
    def _():
        hr_s[...] = jnp.zeros_like(hr_s)
        hi_s[...] = jnp.zeros_like(hi_s)

    wbr, wbi = wbr_ref[0], wbi_ref[0]
    for b in range(n_b):
        ub = u_ref[b].astype(BF16)
        xr = jnp.dot(ub, wbr, preferred_element_type=F32)
        xi = jnp.dot(ub, wbi, preferred_element_type=F32)
        for j in range(SSM_STATE_ROWS):
            cols = slice(j * LANES, (j + 1) * LANES)
            xr_s[b, pl.ds(j, t_c, stride=SSM_STATE_ROWS), :] = xr[:, cols]
            xi_s[b, pl.ds(j, t_c, stride=SSM_STATE_ROWS), :] = xi[:, cols]

    ar, ai = ar_ref[0], ai_ref[0]

    def step(t, carry):
        rows = pl.ds(pl.multiple_of(t * SSM_STATE_ROWS, SSM_STATE_ROWS), SSM_STATE_ROWS)
        new = []
        for b in range(n_b):
            hr, hi = carry[2 * b], carry[2 * b + 1]
            nhr = ar * hr - ai * hi + xr_s[b, rows, :]
            nhi = ar * hi + ai * hr + xi_s[b, rows, :]
            xr_s[b, rows, :] = nhr
            xi_s[b, rows, :] = nhi
            new += [nhr, nhi]
        return tuple(new)

    carry0 = []
    for b in range(n_b):
        carry0 += [hr_s[b], hi_s[b]]
    carry = lax.fori_loop(0, t_c, step, tuple(carry0))
    for b in range(n_b):
        hr_s[b] = carry[2 * b]
        hi_s[b] = carry[2 * b + 1]

    @pl.when(c == pl.num_programs(1) - 1)
    def _():
        for b in range(n_b):
            hr_out[b, 0] = carry[2 * b]
            hi_out[b, 0] = carry[2 * b + 1]

    wcr, wci = wcr_ref[0], wci_ref[0]
    d = d_ref[...]
    for b in range(n_b):
        h_re = jnp.concatenate(
            [xr_s[b, pl.ds(j, t_c, stride=SSM_STATE_ROWS), :] for j in range(SSM_STATE_ROWS)], axis=1)
        h_im = jnp.concatenate(
            [xi_s[b, pl.ds(j, t_c, stride=SSM_STATE_ROWS), :] for j in range(SSM_STATE_ROWS)], axis=1)
        y = (jnp.dot(h_re.astype(BF16), wcr, preferred_element_type=F32)
             - jnp.dot(h_im.astype(BF16), wci, preferred_element_type=F32)
             + d * u_ref[b])
        z_ref[b] = _gelu_tanh(y)


def _s5_prompt(u, wb_re, wb_im, wc_re, wc_im, ab_re, ab_im, d_skip, t_c):
    n_b, seq, width = u.shape
    n_t = width // SSM_CH_TILE
    state = jax.ShapeDtypeStruct((n_b, n_t, SSM_STATE_ROWS, LANES), F32)
    wb_spec = pl.BlockSpec((1, SSM_CH_TILE, SSM_STATE_TILE), lambda kt, c: (kt, 0, 0))
    wc_spec = pl.BlockSpec((1, SSM_STATE_TILE, SSM_CH_TILE), lambda kt, c: (kt, 0, 0))
    a_spec = pl.BlockSpec((1, SSM_STATE_ROWS, LANES), lambda kt, c: (kt, 0, 0))
    u_spec = pl.BlockSpec((n_b, t_c, SSM_CH_TILE), lambda kt, c: (0, c, kt))
    h_spec = pl.BlockSpec((n_b, 1, SSM_STATE_ROWS, LANES), lambda kt, c: (0, kt, 0, 0))
    x_scr = pltpu.VMEM((n_b, t_c * SSM_STATE_ROWS, LANES), F32)
    h_scr = pltpu.VMEM((n_b, SSM_STATE_ROWS, LANES), F32)
    return pl.pallas_call(
        _s5_prompt_kernel,
        out_shape=(jax.ShapeDtypeStruct(u.shape, F32), state, state),
        grid=(n_t, seq // t_c),
        in_specs=[u_spec, wb_spec, wb_spec, wc_spec, wc_spec, a_spec, a_spec,
                  pl.BlockSpec((1, SSM_CH_TILE), lambda kt, c: (0, kt))],
        out_specs=(u_spec, h_spec, h_spec),
        scratch_shapes=[x_scr, x_scr, h_scr, h_scr],
        compiler_params=_cparams("parallel", "arbitrary"),
        name="s5_prompt",
    )(u, wb_re, wb_im, wc_re, wc_im, ab_re, ab_im, d_skip)


def _s5_step_kernel(u_ref, h0r_ref, h0i_ref, wbr_ref, wbi_ref, wcr_ref, wci_ref, ar_ref, ai_ref,
                    d_ref, z_ref, hr_out, hi_out):
    u = u_ref[...]
    ub = u.astype(BF16)
    ar, ai = ar_ref[...], ai_ref[...]
    h0r, h0i = h0r_ref[...], h0i_ref[...]
    hr = jnp.dot(ub, wbr_ref[0], preferred_element_type=F32) + (ar * h0r - ai * h0i)
    hi = jnp.dot(ub, wbi_ref[0], preferred_element_type=F32) + (ar * h0i + ai * h0r)
    hr_out[...] = hr
    hi_out[...] = hi
    y = (jnp.dot(hr.astype(BF16), wcr_ref[0], preferred_element_type=F32)
         - jnp.dot(hi.astype(BF16), wci_ref[0], preferred_element_type=F32)
         + d_ref[...] * u)
    z_ref[...] = _gelu_tanh(y)


def _s5_step(u, h0_re, h0_im, wb_re, wb_im, wc_re, wc_im, ab_re, ab_im, d_skip):
    n_seq, width = u.shape
    n_t = width // SSM_CH_TILE
    n_state = h0_re.shape[1]
    state = jax.ShapeDtypeStruct((n_seq, n_state), F32)
    u_spec = pl.BlockSpec((n_seq, SSM_CH_TILE), lambda kt: (0, kt))
    h_spec = pl.BlockSpec((n_seq, SSM_STATE_TILE), lambda kt: (0, kt))
    wb_spec = pl.BlockSpec((1, SSM_CH_TILE, SSM_STATE_TILE), lambda kt: (kt, 0, 0))
    wc_spec = pl.BlockSpec((1, SSM_STATE_TILE, SSM_CH_TILE), lambda kt: (kt, 0, 0))
    a_spec = pl.BlockSpec((1, SSM_STATE_TILE), lambda kt: (0, kt))
    return pl.pallas_call(
        _s5_step_kernel,
        out_shape=(jax.ShapeDtypeStruct(u.shape, F32), state, state),
        grid=(n_t,),
        in_specs=[u_spec, h_spec, h_spec, wb_spec, wb_spec, wc_spec, wc_spec, a_spec, a_spec,
                  pl.BlockSpec((1, SSM_CH_TILE), lambda kt: (0, kt))],
        out_specs=(u_spec, h_spec, h_spec),
        compiler_params=_cparams("parallel"),
        name="s5_step",
    )(u, h0_re, h0_im, wb_re, wb_im, wc_re, wc_im, ab_re, ab_im, d_skip)


def _merge_out_kernel(z_ref, attn_ref, ga_ref, gs_ref, x_ref, wv_ref, wg_ref, wo_ref, o_ref, zb_ref):
    j = pl.program_id(1)

    @pl.when(j == 0)
    def _():
        zb_ref[...] = z_ref[...].astype(BF16)
        o_ref[...] = x_ref[...]

    zb = zb_ref[...]
    val = jnp.dot(zb, wv_ref[...].astype(BF16), preferred_element_type=F32)
    gat = jnp.dot(zb, wg_ref[...].astype(BF16), preferred_element_type=F32)
    mix = ga_ref[...] * attn_ref[...] + gs_ref[...] * (val * _sigmoid(gat))
    o_ref[...] += jnp.dot(mix.astype(BF16), wo_ref[...].astype(BF16), preferred_element_type=F32)


def _merge_out(z, attn, gates, x, w_glu_v, w_glu_g, w_out, tm, tn):
    m, d = x.shape
    gs_off = d // tn
    row = pl.BlockSpec((tm, d), lambda i, j: (i, 0))
    col = pl.BlockSpec((tm, tn), lambda i, j: (i, j))
    return pl.pallas_call(
        _merge_out_kernel,
        out_shape=jax.ShapeDtypeStruct((m, d), F32),
        grid=(m // tm, d // tn),
        in_specs=[row, col, col, pl.BlockSpec((tm, tn), lambda i, j: (i, gs_off + j)), row,
                  pl.BlockSpec((d, tn), lambda i, j: (0, j)),
                  pl.BlockSpec((d, tn), lambda i, j: (0, j)),
                  pl.BlockSpec((tn, d), lambda i, j: (j, 0))],
        out_specs=row,
        scratch_shapes=[pltpu.VMEM((tm, d), BF16)],
        compiler_params=_cparams("parallel", "arbitrary"),
        name="merge_out",
    )(z, attn, gates, gates, x, w_glu_v, w_glu_g, w_out)


def _ffn_kernel(x_ref, g_ref, gf_ref, wu_ref, wd_ref, o_ref, hn_ref):
    f = pl.program_id(1)

    @pl.when(f == 0)
    def _():
        x = x_ref[...]
        hn_ref[...] = (_rms_scale(x) * g_ref[...]).astype(BF16)
        o_ref[...] = x

    up = jnp.dot(hn_ref[...], wu_ref[...].astype(BF16), preferred_element_type=F32)
    act = jnp.square(jnp.maximum(up, 0.0))
    o_ref[...] += jnp.dot(act.astype(BF16), wd_ref[...].astype(BF16), preferred_element_type=F32)

    @pl.when(f == pl.num_programs(1) - 1)
    def _():
        o_ref[...] = _rms_scale(o_ref[...]) * gf_ref[...]


def _ffn(x, norm_ffn, norm_final, w_up, w_down, tm, tf):
    m, d = x.shape
    d_ff = w_up.shape[1]
    row = pl.BlockSpec((tm, d), lambda i, f: (i, 0))
    vec = pl.BlockSpec((1, d), lambda i, f: (0, 0))
    return pl.pallas_call(
        _ffn_kernel,
        out_shape=jax.ShapeDtypeStruct((m, d), F32),
        grid=(m // tm, d_ff // tf),
        in_specs=[row, vec, vec,
                  pl.BlockSpec((d, tf), lambda i, f: (0, f)),
                  pl.BlockSpec((tf, d), lambda i, f: (f, 0))],
        out_specs=row,
        scratch_shapes=[pltpu.VMEM((tm, d), BF16)],
        compiler_params=_cparams("parallel", "arbitrary"),
        name="ffn",
    )(x, norm_ffn, norm_final, w_up, w_down)


def kernel(x_prompt, x_sample, cache_k, cache_v, state_ssm_re, state_ssm_im, page_table, norm_mix, w_in, lambda_re, lambda_im, log_step, b_re, b_im, c_re, c_im, d_skip, w_glu_v, w_glu_g, w_out, norm_ffn, w_up, w_down, norm_final):
    depth = w_in.shape[0]
    assert depth == 1, "single trunk layer"
    n_b, seq, d = x_prompt.shape
    n_dec, s_dec, _ = x_sample.shape
    assert s_dec == 1
    width = N_HEADS * HEAD_DIM
    past_len = page_table.shape[1] * cache_k.shape[2]
    n_groups = lambda_re.shape[1]

    cos_t, sin_a, sin_b = _rope_tables(max(seq, past_len + s_dec))
    ab_re, ab_im, bb_re, bb_im = _ssm_prep(lambda_re[0], lambda_im[0], log_step[0], b_re[0], b_im[0])
    to_in = lambda bb: _block_diag_tiles(
        bb.reshape(n_groups, SSM_STATE, SSM_GROUP).transpose(0, 2, 1))
    wb_re, wb_im = to_in(bb_re), to_in(bb_im)
    to_out = lambda cc: _block_diag_tiles(cc.transpose(0, 2, 1))
    wc_re, wc_im = to_out(c_re[0]), to_out(c_im[0])
    n_t = d // SSM_CH_TILE
    ab_re_t = ab_re.reshape(n_t, SSM_STATE_ROWS, LANES)
    ab_im_t = ab_im.reshape(n_t, SSM_STATE_ROWS, LANES)

    g_mix, g_ffn, g_fin = norm_mix[0][None, :], norm_ffn[0][None, :], norm_final[None, :]
    w_in0, d_row = w_in[0], d_skip[0][None, :]

    def in_proj(x, tm, rope, rope_blocks):
        proj = functools.partial(_norm_proj, x, g_mix, w_in0, tm=tm, tn=512)
        q = proj(0, width, "rope", rope=rope, rope_blocks=rope_blocks)
        k = proj(width, width, "rope", rope=rope, rope_blocks=rope_blocks)
        v = proj(2 * width, width, "none")
        u = proj(3 * width, d, "none")
        gates = proj(3 * width + d, 2 * d, "sigmoid")
        return q, k, v, u, gates

    def tail(x, z, attn, gates, tm):
        x1 = _merge_out(z, attn, gates, x, w_glu_v[0], w_glu_g[0], w_out[0], tm=tm, tn=256)
        return _ffn(x1, g_ffn, g_fin, w_up[0], w_down[0], tm=tm, tf=256)

    tm_p = 512
    xp = x_prompt.reshape(n_b * seq, d)
    qp, kp, vp, up, gp = in_proj(xp, tm_p, (cos_t, sin_a, sin_b), seq // tm_p)
    attn_p = _moba_prompt(qp.reshape(n_b, seq, width), kp.reshape(n_b, seq, width),
                          vp.reshape(n_b, seq, width))
    zp, hpr, hpi = _s5_prompt(up.reshape(n_b, seq, d), wb_re, wb_im, wc_re, wc_im,
                              ab_re_t, ab_im_t, d_row, t_c=256)
    y_prompt = tail(xp, zp.reshape(n_b * seq, d), attn_p.reshape(n_b * seq, width), gp, tm_p)

    xs = x_sample.reshape(n_dec, d)
    rope_s = tuple(jnp.broadcast_to(t[past_len:past_len + 1], (n_dec, LANES))
                   for t in (cos_t, sin_a, sin_b))
    qs, ks, vs, us, gs = in_proj(xs, n_dec, rope_s, 1)
    attn_s = _moba_decode(qs, ks, vs, cache_k[0], cache_v[0], page_table)
    n_state = n_groups * SSM_STATE
    zs, hsr, hsi = _s5_step(us, state_ssm_re[0].reshape(n_dec, n_state),
                            state_ssm_im[0].reshape(n_dec, n_state),
                            wb_re, wb_im, wc_re, wc_im,
                            ab_re.reshape(1, n_state), ab_im.reshape(1, n_state), d_row)
    y_sample = tail(xs, zs, attn_s, gs, n_dec)

    kv_p = (1, n_b, seq, N_HEADS, HEAD_DIM)
    kv_s = (1, n_dec, s_dec, N_HEADS, HEAD_DIM)
    st_p = (1, n_b, n_groups, SSM_STATE)
    st_s = (1, n_dec, n_groups, SSM_STATE)
    return (y_prompt.reshape(n_b, seq, d), y_sample.reshape(n_dec, s_dec, d),
            kp.reshape(kv_p), vp.reshape(kv_p), hpr.reshape(st_p), hpi.reshape(st_p),
            ks.reshape(kv_s), vs.reshape(kv_s), hsr.reshape(st_s), hsi.reshape(st_s))
```

```python
import functools
import math

import jax
import jax.numpy as jnp
from jax import lax
from jax.experimental import pallas as pl
from jax.experimental.pallas import tpu as pltpu

N_HEADS = 16
HEAD_DIM = 128
ROT_DIM = HEAD_DIM // 4
ROT_HALF = ROT_DIM // 2
ROPE_THETA = 500000.0
MOBA_BLOCK = 256
MOBA_TOPK = 3
SSM_GROUP = 16
SSM_STATE = 64
RMS_EPS = 1e-6
NEG = -1e30

LANES = 128
SUBLANES = 8
MXU_DIM = 256
VMEM_LIMIT_BYTES = 56 * 1024 * 1024

SSM_CH_TILE = MXU_DIM
SSM_GROUPS_PER_TILE = SSM_CH_TILE // SSM_GROUP
SSM_STATE_TILE = SSM_GROUPS_PER_TILE * SSM_STATE
SSM_STATE_ROWS = SSM_STATE_TILE // LANES

F32 = jnp.float32
BF16 = jnp.bfloat16


def _cparams(*sem):
    return pltpu.CompilerParams(dimension_semantics=sem, vmem_limit_bytes=VMEM_LIMIT_BYTES)


def _sigmoid(x):
    return 1.0 / (1.0 + jnp.exp(-x))


def _gelu_tanh(x):
    c = math.sqrt(2.0 / math.pi)
    return 0.5 * x * (1.0 + jnp.tanh(c * (x + 0.044715 * (x * x * x))))


def _rms_scale(x):
    return x * lax.rsqrt(jnp.mean(x * x, axis=-1, keepdims=True) + RMS_EPS)


def _rope_table_kernel(inv_ref, cos_ref, sin_a_ref, sin_b_ref):
    rows = cos_ref.shape[0]
    pos = lax.broadcasted_iota(jnp.int32, (rows, LANES), 0).astype(F32)
    lane = lax.broadcasted_iota(jnp.int32, (rows, LANES), 1)
    ang = pos * inv_ref[...]
    c = jnp.cos(ang)
    s = jnp.sin(ang)
    cos_ref[...] = c
    sin_a_ref[...] = jnp.where((lane >= ROT_HALF) & (lane < ROT_DIM), s, 0.0)
    sin_b_ref[...] = jnp.where(lane < ROT_HALF, -s, 0.0)


def _rope_tables(n_pos):
    rows = -(-n_pos // SUBLANES) * SUBLANES
    inv = ROPE_THETA ** (-jnp.arange(ROT_HALF, dtype=F32) / ROT_HALF)
    inv_row = jnp.concatenate([inv, inv, jnp.zeros((LANES - ROT_DIM,), F32)])[None, :]
    out = jax.ShapeDtypeStruct((rows, LANES), F32)
    return pl.pallas_call(_rope_table_kernel, out_shape=(out, out, out), name="rope_tables")(inv_row)


def _ssm_prep_kernel(ls_ref, lr_ref, li_ref, lrr_ref, lir_ref, br_ref, bi_ref,
                     abr_ref, abi_ref, bbr_ref, bbi_ref):
    step = jnp.exp(ls_ref[...])

    def disc(lr, li):
        mag = jnp.exp(lr * step)
        ang = li * step
        ab_re, ab_im = mag * jnp.cos(ang), mag * jnp.sin(ang)
        den = lr * lr + li * li
        nr, ni = ab_re - 1.0, ab_im
        f_re = (nr * lr + ni * li) / den
        f_im = (ni * lr - nr * li) / den
        return ab_re, ab_im, f_re, f_im

    ab_re, ab_im, _, _ = disc(lr_ref[...], li_ref[...])
    abr_ref[...] = ab_re
    abi_ref[...] = ab_im
    _, _, f_re, f_im = disc(lrr_ref[...], lir_ref[...])
    br, bi = br_ref[...], bi_ref[...]
    bbr_ref[...] = f_re * br - f_im * bi
    bbi_ref[...] = f_re * bi + f_im * br


def _ssm_prep(lambda_re, lambda_im, log_step, b_re, b_im):
    g, n = lambda_re.shape
    flat = g, n * SSM_GROUP
    outs = (jax.ShapeDtypeStruct((g, n), F32),) * 2 + (jax.ShapeDtypeStruct(flat, F32),) * 2
    return pl.pallas_call(_ssm_prep_kernel, out_shape=outs, name="ssm_prep")(
        log_step.reshape(g, 1), lambda_re, lambda_im,
        jnp.repeat(lambda_re, SSM_GROUP, axis=1), jnp.repeat(lambda_im, SSM_GROUP, axis=1),
        b_re.reshape(flat), b_im.reshape(flat))


def _block_diag_tiles(w):
    g, r, c = w.shape
    t = g // SSM_GROUPS_PER_TILE
    w = w.reshape(t, SSM_GROUPS_PER_TILE, r, 1, c)
    eye = jnp.eye(SSM_GROUPS_PER_TILE, dtype=w.dtype)[None, :, None, :, None]
    return (w * eye).reshape(t, SSM_GROUPS_PER_TILE * r, SSM_GROUPS_PER_TILE * c).astype(BF16)


def _norm_proj_kernel(*refs, mode):
    if mode == "rope":
        x_ref, g_ref, w_ref, cos_ref, sa_ref, sb_ref, o_ref, hn_ref = refs
    else:
        x_ref, g_ref, w_ref, o_ref, hn_ref = refs

    @pl.when(pl.program_id(1) == 0)
    def _():
        hn_ref[...] = (_rms_scale(x_ref[...]) * g_ref[...]).astype(BF16)

    acc = jnp.dot(hn_ref[...], w_ref[...], preferred_element_type=F32)
    if mode == "rope":
        cos, sa, sb = cos_ref[...], sa_ref[...], sb_ref[...]
        for h in range(acc.shape[1] // HEAD_DIM):
            xh = acc[:, h * HEAD_DIM:(h + 1) * HEAD_DIM]
            o_ref[:, h * HEAD_DIM:(h + 1) * HEAD_DIM] = (
                xh * cos + pltpu.roll(xh, ROT_HALF, 1) * sa
                + pltpu.roll(xh, HEAD_DIM - ROT_HALF, 1) * sb)
    elif mode == "sigmoid":
        o_ref[...] = _sigmoid(acc)
    else:
        o_ref[...] = acc


def _norm_proj(x, gain, w, col0, n_cols, mode, tm, tn, rope=None, rope_blocks=1):
    m, d = x.shape
    cb0 = col0 // tn
    in_specs = [pl.BlockSpec((tm, d), lambda i, j: (i, 0)),
                pl.BlockSpec((1, d), lambda i, j: (0, 0)),
                pl.BlockSpec((d, tn), lambda i, j: (0, cb0 + j))]
    args = [x, gain, w]
    if mode == "rope":
        in_specs += [pl.BlockSpec((tm, LANES), lambda i, j: (i % rope_blocks, 0))] * 3
        args += list(rope)
    return pl.pallas_call(
        functools.partial(_norm_proj_kernel, mode=mode),
        out_shape=jax.ShapeDtypeStruct((m, n_cols), F32),
        grid=(m // tm, n_cols // tn),
        in_specs=in_specs,
        out_specs=pl.BlockSpec((tm, tn), lambda i, j: (i, j)),
        scratch_shapes=[pltpu.VMEM((tm, d), BF16)],
        compiler_params=_cparams("parallel", "arbitrary"),
        name="norm_proj_" + mode,
    )(*args)


def _topk_selected(gates, i):
    return _topk_rank(gates, i) < float(MOBA_TOPK)


def _topk_rank(gates, i):
    gi = gates[i]
    rank = jnp.zeros(gi.shape, F32)
    for i2, g2 in enumerate(gates):
        if i2 == i:
            continue
        ahead = (g2 >= gi) if i2 < i else (g2 > gi)
        rank = rank + jnp.where(ahead, 1.0, 0.0)
    return rank


def _moba_prompt_kernel(q_ref, k_ref, v_ref, o_ref):
    seq = q_ref.shape[1]
    nb = seq // MOBA_BLOCK
    scale = HEAD_DIM ** -0.5
    q, k, v = q_ref[0], k_ref[0], v_ref[0]
    k_mean = jnp.concatenate(
        [jnp.mean(k[i * MOBA_BLOCK:(i + 1) * MOBA_BLOCK], axis=0, keepdims=True) for i in range(nb)],
        axis=0)
    gate = lax.dot_general(q, k_mean, (((1,), (1,)), ((), ())),
                           precision=lax.Precision.HIGHEST, preferred_element_type=F32)
    qb, kb, vb = q.astype(BF16), k.astype(BF16), v.astype(BF16)
    row = lax.broadcasted_iota(jnp.int32, (MOBA_BLOCK, MOBA_BLOCK), 0)
    col = lax.broadcasted_iota(jnp.int32, (MOBA_BLOCK, MOBA_BLOCK), 1)
    causal = col <= row
    for j in range(nb):
        rows = slice(j * MOBA_BLOCK, (j + 1) * MOBA_BLOCK)
        n_keys = (j + 1) * MOBA_BLOCK
        s = lax.dot_general(qb[rows], kb[:n_keys], (((1,), (1,)), ((), ())),
                            preferred_element_type=F32) * scale
        gates = [gate[rows, i:i + 1] for i in range(j)]
        pieces = []
        for i in range(j):
            s_i = s[:, i * MOBA_BLOCK:(i + 1) * MOBA_BLOCK]
            if j > MOBA_TOPK:
                rank = jnp.broadcast_to(_topk_rank(gates, i), s_i.shape)
                s_i = jnp.where(rank < float(MOBA_TOPK), s_i, NEG)
            pieces.append(s_i)
        pieces.append(jnp.where(causal, s[:, j * MOBA_BLOCK:], NEG))
        s = jnp.concatenate(pieces, axis=1)
        p = jnp.exp(s - jnp.max(s, axis=-1, keepdims=True))
        l = jnp.sum(p, axis=-1, keepdims=True)
        o = jnp.dot(p.astype(BF16), vb[:n_keys], preferred_element_type=F32)
        o_ref[0, rows, :] = o / l


def _moba_prompt(q, k, v):
    b, s, _ = q.shape
    spec = pl.BlockSpec((1, s, HEAD_DIM), lambda bi, h: (bi, 0, h))
    return pl.pallas_call(
        _moba_prompt_kernel,
        out_shape=jax.ShapeDtypeStruct(q.shape, F32),
        grid=(b, N_HEADS),
        in_specs=[spec, spec, spec],
        out_specs=spec,
        compiler_params=_cparams("parallel", "parallel"),
        name="moba_prompt",
    )(q, k, v)


def _moba_decode_kernel(pt_ref, q_ref, kn_ref, vn_ref, *refs, pages_per_step):
    del pt_ref
    k_refs, v_refs = refs[:pages_per_step], refs[pages_per_step:2 * pages_per_step]
    o_ref, g_s, m_s, l_s, o_s = refs[2 * pages_per_step:]
    step = pl.program_id(1)
    page = k_refs[0].shape[1]
    pages_per_blk = MOBA_BLOCK // page
    blks_per_step = pages_per_step // pages_per_blk
    rows_per_page = page * N_HEADS
    n_cols = pages_per_blk * rows_per_page
    scale = HEAD_DIM ** -0.5
    lanes = (N_HEADS, HEAD_DIM)
    q = q_ref[0]
    qb = q.astype(BF16)
    col_head = lax.broadcasted_iota(jnp.int32, (N_HEADS, n_cols), 1) % N_HEADS
    own = col_head == lax.broadcasted_iota(jnp.int32, (N_HEADS, n_cols), 0)

    for bi in range(blks_per_step):
        kp = [k_refs[bi * pages_per_blk + p][0] for p in range(pages_per_blk)]
        vp = [v_refs[bi * pages_per_blk + p][0] for p in range(pages_per_blk)]
        k_rows = jnp.concatenate([k.reshape(rows_per_page, HEAD_DIM) for k in kp], axis=0)
        v_rows = jnp.concatenate([v.reshape(rows_per_page, HEAD_DIM) for v in vp], axis=0)
        k_sum = kp[0].sum(axis=0)
        for k in kp[1:]:
            k_sum = k_sum + k.sum(axis=0)
        gate = jnp.sum(q * (k_sum * (1.0 / MOBA_BLOCK)), axis=-1, keepdims=True)
        s = lax.dot_general(qb, k_rows.astype(BF16), (((1,), (1,)), ((), ())),
                            preferred_element_type=F32) * scale
        s = jnp.where(own, s, NEG)
        m = jnp.max(s, axis=-1, keepdims=True)
        p = jnp.exp(s - m)
        l = jnp.sum(p, axis=-1, keepdims=True)
        o = jnp.dot(p.astype(BF16), v_rows.astype(BF16), preferred_element_type=F32)
        blk = step * blks_per_step + bi
        g_s[blk] = jnp.broadcast_to(gate, lanes)
        m_s[blk] = jnp.broadcast_to(m, lanes)
        l_s[blk] = jnp.broadcast_to(l, lanes)
        o_s[blk] = o

    @pl.when(step == pl.num_programs(1) - 1)
    def _():
        nb = g_s.shape[0]
        kn = kn_ref[0].astype(BF16).astype(F32)
        v_own = vn_ref[0].astype(BF16).astype(F32)
        s_own = jnp.sum(qb.astype(F32) * kn, axis=-1, keepdims=True) * scale
        s_own = jnp.broadcast_to(s_own, lanes)
        gates = [g_s[i] for i in range(nb)]
        sel = [_topk_selected(gates, i) for i in range(nb)]
        m_all = s_own
        for i in range(nb):
            m_all = jnp.maximum(m_all, jnp.where(sel[i], m_s[i], NEG))
        w_own = jnp.exp(s_own - m_all)
        den = w_own
        num = w_own.astype(BF16).astype(F32) * v_own
        for i in range(nb):
            w = jnp.where(sel[i], jnp.exp(m_s[i] - m_all), 0.0)
            den = den + w * l_s[i]
            num = num + w * o_s[i]
        o_ref[0] = num / den


def _moba_decode(q, k_new, v_new, cache_k, cache_v, page_table, pages_per_step):
    n_dec = q.shape[0]
    page = cache_k.shape[1]
    n_pages = page_table.shape[1]
    pages_per_blk = MOBA_BLOCK // page
    assert MOBA_BLOCK % page == 0 and (n_pages * page) % MOBA_BLOCK == 0
    assert pages_per_step % pages_per_blk == 0 and n_pages % pages_per_step == 0
    nb = n_pages // pages_per_blk
    row = pl.BlockSpec((1, N_HEADS, HEAD_DIM), lambda b, i, pt: (b, 0, 0))

    def page_spec(p):
        return pl.BlockSpec((1, page, N_HEADS, HEAD_DIM),
                            lambda b, i, pt: (pt[b, i * pages_per_step + p], 0, 0, 0))

    pages = [page_spec(p) for p in range(pages_per_step)]
    stat = pltpu.VMEM((nb, N_HEADS, HEAD_DIM), F32)
    return pl.pallas_call(
        functools.partial(_moba_decode_kernel, pages_per_step=pages_per_step),
        out_shape=jax.ShapeDtypeStruct((n_dec, N_HEADS, HEAD_DIM), F32),
        grid_spec=pltpu.PrefetchScalarGridSpec(
            num_scalar_prefetch=1,
            grid=(n_dec, n_pages // pages_per_step),
            in_specs=[row, row, row] + pages + pages,
            out_specs=row,
            scratch_shapes=[stat, stat, stat, stat]),
        compiler_params=_cparams("parallel", "arbitrary"),
        name="moba_decode",
    )(page_table, q, k_new, v_new, *([cache_k] * pages_per_step), *([cache_v] * pages_per_step))


def _s5_prompt_kernel(u_ref, wbr_ref, wbi_ref, wcr_ref, wci_ref, ar_ref, ai_ref, d_ref,
                      z_ref, hr_out, hi_out, xr_s, xi_s, hr_s, hi_s):
    c = pl.program_id(1)
    n_b, t_c = u_ref.shape[0], u_ref.shape[1]

    @pl.when(c == 0)
    def _():
        hr_s[...] = jnp.zeros_like(hr_s)
        hi_s[...] = jnp.zeros_like(hi_s)

    wbr, wbi = wbr_ref[0], wbi_ref[0]
    for b in range(n_b):
        ub = u_ref[b].astype(BF16)
        xr = jnp.dot(ub, wbr, preferred_element_type=F32)
        xi = jnp.dot(ub, wbi, preferred_element_type=F32)
        for j in range(SSM_STATE_ROWS):
            cols = slice(j * LANES, (j + 1) * LANES)
            xr_s[b, pl.ds(j, t_c, stride=SSM_STATE_ROWS), :] = xr[:, cols]
            xi_s[b, pl.ds(j, t_c, stride=SSM_STATE_ROWS), :] = xi[:, cols]

    ar, ai = ar_ref[0], ai_ref[0]

    def step(t, carry):
        rows = pl.ds(pl.multiple_of(t * SSM_STATE_ROWS, SSM_STATE_ROWS), SSM_STATE_ROWS)
        new = []
        for b in range(n_b):
            hr, hi = carry[2 * b], carry[2 * b + 1]
            nhr = ar * hr - ai * hi + xr_s[b, rows, :]
            nhi = ar * hi + ai * hr + xi_s[b, rows, :]
            xr_s[b, rows, :] = nhr
            xi_s[b, rows, :] = nhi
            new += [nhr, nhi]
        return tuple(new)

    carry0 = []
    for b in range(n_b):
        carry0 += [hr_s[b], hi_s[b]]
    carry = lax.fori_loop(0, t_c, step, tuple(carry0))
    for b in range(n_b):
        hr_s[b] = carry[2 * b]
        hi_s[b] = carry[2 * b + 1]

    @pl.when(c == pl.num_programs(1) - 1)
    def _():
        for b in range(n_b):
            hr_out[b, 0] = carry[2 * b]
            hi_out[b, 0] = carry[2 * b + 1]

    wcr, wci = wcr_ref[0], wci_ref[0]
    d = d_ref[...]
    for b in range(n_b):
        h_re = jnp.concatenate(
            [xr_s[b, pl.ds(j, t_c, stride=SSM_STATE_ROWS), :] for j in range(SSM_STATE_ROWS)], axis=1)
        h_im = jnp.concatenate(
            [xi_s[b, pl.ds(j, t_c, stride=SSM_STATE_ROWS), :] for j in range(SSM_STATE_ROWS)], axis=1)
        y = (jnp.dot(h_re.astype(BF16), wcr, preferred_element_type=F32)
             - jnp.dot(h_im.astype(BF16), wci, preferred_element_type=F32)
             + d * u_ref[b])
        z_ref[b] = _gelu_tanh(y).astype(z_ref.dtype)


def _s5_prompt(u, wb_re, wb_im, wc_re, wc_im, ab_re, ab_im, d_skip, t_c):
    n_b, seq, width = u.shape
    n_t = width // SSM_CH_TILE
    state = jax.ShapeDtypeStruct((n_b, n_t, SSM_STATE_ROWS, LANES), F32)
    wb_spec = pl.BlockSpec((1, SSM_CH_TILE, SSM_STATE_TILE), lambda kt, c: (kt, 0, 0))
    wc_spec = pl.BlockSpec((1, SSM_STATE_TILE, SSM_CH_TILE), lambda kt, c: (kt, 0, 0))
    a_spec = pl.BlockSpec((1, SSM_STATE_ROWS, LANES), lambda kt, c: (kt, 0, 0))
    u_spec = pl.BlockSpec((n_b, t_c, SSM_CH_TILE), lambda kt, c: (0, c, kt))
    h_spec = pl.BlockSpec((n_b, 1, SSM_STATE_ROWS, LANES), lambda kt, c: (0, kt, 0, 0))
    x_scr = pltpu.VMEM((n_b, t_c * SSM_STATE_ROWS, LANES), F32)
    h_scr = pltpu.VMEM((n_b, SSM_STATE_ROWS, LANES), F32)
    return pl.pallas_call(
        _s5_prompt_kernel,
        out_shape=(jax.ShapeDtypeStruct(u.shape, BF16), state, state),
        grid=(n_t, seq // t_c),
        in_specs=[u_spec, wb_spec, wb_spec, wc_spec, wc_spec, a_spec, a_spec,
                  pl.BlockSpec((1, SSM_CH_TILE), lambda kt, c: (0, kt))],
        out_specs=(u_spec, h_spec, h_spec),
        scratch_shapes=[x_scr, x_scr, h_scr, h_scr],
        compiler_params=_cparams("parallel", "arbitrary"),
        name="s5_prompt",
    )(u, wb_re, wb_im, wc_re, wc_im, ab_re, ab_im, d_skip)


def _s5_step_kernel(u_ref, h0r_ref, h0i_ref, wbr_ref, wbi_ref, wcr_ref, wci_ref, ar_ref, ai_ref,
                    d_ref, z_ref, hr_out, hi_out):
    u = u_ref[...]
    ub = u.astype(BF16)
    ar, ai = ar_ref[...], ai_ref[...]
    h0r, h0i = h0r_ref[...], h0i_ref[...]
    hr = jnp.dot(ub, wbr_ref[0], preferred_element_type=F32) + (ar * h0r - ai * h0i)
    hi = jnp.dot(ub, wbi_ref[0], preferred_element_type=F32) + (ar * h0i + ai * h0r)
    hr_out[...] = hr
    hi_out[...] = hi
    y = (jnp.dot(hr.astype(BF16), wcr_ref[0], preferred_element_type=F32)
         - jnp.dot(hi.astype(BF16), wci_ref[0], preferred_element_type=F32)
         + d_ref[...] * u)
    z_ref[...] = _gelu_tanh(y).astype(z_ref.dtype)


def _s5_step(u, h0_re, h0_im, wb_re, wb_im, wc_re, wc_im, ab_re, ab_im, d_skip):
    n_seq, width = u.shape
    n_t = width // SSM_CH_TILE
    n_state = h0_re.shape[1]
    state = jax.ShapeDtypeStruct((n_seq, n_state), F32)
    u_spec = pl.BlockSpec((n_seq, SSM_CH_TILE), lambda kt: (0, kt))
    h_spec = pl.BlockSpec((n_seq, SSM_STATE_TILE), lambda kt: (0, kt))
    wb_spec = pl.BlockSpec((1, SSM_CH_TILE, SSM_STATE_TILE), lambda kt: (kt, 0, 0))
    wc_spec = pl.BlockSpec((1, SSM_STATE_TILE, SSM_CH_TILE), lambda kt: (kt, 0, 0))
    a_spec = pl.BlockSpec((1, SSM_STATE_TILE), lambda kt: (0, kt))
    return pl.pallas_call(
        _s5_step_kernel,
        out_shape=(jax.ShapeDtypeStruct(u.shape, BF16), state, state),
        grid=(n_t,),
        in_specs=[u_spec, h_spec, h_spec, wb_spec, wb_spec, wc_spec, wc_spec, a_spec, a_spec,
                  pl.BlockSpec((1, SSM_CH_TILE), lambda kt: (0, kt))],
        out_specs=(u_spec, h_spec, h_spec),
        compiler_params=_cparams("parallel"),
        name="s5_step",
    )(u, h0_re, h0_im, wb_re, wb_im, wc_re, wc_im, ab_re, ab_im, d_skip)


def _glu_mix_kernel(z_ref, attn_ref, ga_ref, gs_ref, wv_ref, wg_ref, o_ref):
    z = z_ref[...]
    val = jnp.dot(z, wv_ref[...], preferred_element_type=F32)
    gat = jnp.dot(z, wg_ref[...], preferred_element_type=F32)
    mix = ga_ref[...] * attn_ref[...] + gs_ref[...] * (val * _sigmoid(gat))
    o_ref[...] = mix.astype(o_ref.dtype)


def _glu_mix(z, attn, gates, w_glu_v, w_glu_g, tm, tn):
    m, d = z.shape
    gs_off = d // tn
    col = pl.BlockSpec((tm, tn), lambda i, j: (i, j))
    w_spec = pl.BlockSpec((d, tn), lambda i, j: (0, j))
    return pl.pallas_call(
        _glu_mix_kernel,
        out_shape=jax.ShapeDtypeStruct((m, d), BF16),
        grid=(m // tm, d // tn),
        in_specs=[pl.BlockSpec((tm, d), lambda i, j: (i, 0)), col, col,
                  pl.BlockSpec((tm, tn), lambda i, j: (i, gs_off + j)), w_spec, w_spec],
        out_specs=col,
        compiler_params=_cparams("parallel", "arbitrary"),
        name="glu_mix",
    )(z, attn, gates, gates, w_glu_v, w_glu_g)


def _out_proj_kernel(mix_ref, x_ref, w_ref, o_ref):
    o_ref[...] = x_ref[...] + jnp.dot(mix_ref[...], w_ref[...], preferred_element_type=F32)


def _out_proj(mix, x, w_out, tm, tn):
    m, d = x.shape
    col = pl.BlockSpec((tm, tn), lambda i, j: (i, j))
    return pl.pallas_call(
        _out_proj_kernel,
        out_shape=jax.ShapeDtypeStruct((m, d), F32),
        grid=(m // tm, d // tn),
        in_specs=[pl.BlockSpec((tm, d), lambda i, j: (i, 0)), col,
                  pl.BlockSpec((d, tn), lambda i, j: (0, j))],
        out_specs=col,
        compiler_params=_cparams("parallel", "arbitrary"),
        name="out_proj",
    )(mix, x, w_out)


def _ffn_kernel(x_ref, g_ref, gf_ref, wu_ref, wd_ref, o_ref, hn_ref):
    f = pl.program_id(1)

    @pl.when(f == 0)
    def _():
        x = x_ref[...]
        hn_ref[...] = (_rms_scale(x) * g_ref[...]).astype(BF16)
        o_ref[...] = x

    up = jnp.dot(hn_ref[...], wu_ref[...], preferred_element_type=F32)
    act = jnp.square(jnp.maximum(up, 0.0))
    o_ref[...] += jnp.dot(act.astype(BF16), wd_ref[...], preferred_element_type=F32)

    @pl.when(f == pl.num_programs(1) - 1)
    def _():
        o_ref[...] = _rms_scale(o_ref[...]) * gf_ref[...]


def _ffn(x, norm_ffn, norm_final, w_up, w_down, tm, tf):
    m, d = x.shape
    d_ff = w_up.shape[1]
    row = pl.BlockSpec((tm, d), lambda i, f: (i, 0))
    vec = pl.BlockSpec((1, d), lambda i, f: (0, 0))
    return pl.pallas_call(
        _ffn_kernel,
        out_shape=jax.ShapeDtypeStruct((m, d), F32),
        grid=(m // tm, d_ff // tf),
        in_specs=[row, vec, vec,
                  pl.BlockSpec((d, tf), lambda i, f: (0, f)),
                  pl.BlockSpec((tf, d), lambda i, f: (f, 0))],
        out_specs=row,
        scratch_shapes=[pltpu.VMEM((tm, d), BF16)],
        compiler_params=_cparams("parallel", "arbitrary"),
        name="ffn",
    )(x, norm_ffn, norm_final, w_up, w_down)


def kernel(x_prompt, x_sample, cache_k, cache_v, state_ssm_re, state_ssm_im, page_table, norm_mix, w_in, lambda_re, lambda_im, log_step, b_re, b_im, c_re, c_im, d_skip, w_glu_v, w_glu_g, w_out, norm_ffn, w_up, w_down, norm_final):
    depth = w_in.shape[0]
    assert depth == 1, "single trunk layer"
    n_b, seq, d = x_prompt.shape
    n_dec, s_dec, _ = x_sample.shape
    assert s_dec == 1
    width = N_HEADS * HEAD_DIM
    n_pool, page = cache_k.shape[1], cache_k.shape[2]
    past_len = page_table.shape[1] * page
    n_groups = lambda_re.shape[1]

    cos_t, sin_a, sin_b = _rope_tables(max(seq, past_len + s_dec))
    ab_re, ab_im, bb_re, bb_im = _ssm_prep(lambda_re[0], lambda_im[0], log_step[0], b_re[0], b_im[0])
    to_in = lambda bb: _block_diag_tiles(
        bb.reshape(n_groups, SSM_STATE, SSM_GROUP).transpose(0, 2, 1))
    wb_re, wb_im = to_in(bb_re), to_in(bb_im)
    to_out = lambda cc: _block_diag_tiles(cc.transpose(0, 2, 1))
    wc_re, wc_im = to_out(c_re[0]), to_out(c_im[0])
    n_t = d // SSM_CH_TILE
    ab_re_t = ab_re.reshape(n_t, SSM_STATE_ROWS, LANES)
    ab_im_t = ab_im.reshape(n_t, SSM_STATE_ROWS, LANES)

    g_mix, g_ffn, g_fin = norm_mix.reshape(1, d), norm_ffn.reshape(1, d), norm_final.reshape(1, d)
    d_row = d_skip.reshape(1, d)
    as_bf16 = lambda w: w.reshape(w.shape[1:]).astype(BF16)
    w_in16, w_v16, w_g16, w_o16 = as_bf16(w_in), as_bf16(w_glu_v), as_bf16(w_glu_g), as_bf16(w_out)
    w_up16, w_dn16 = as_bf16(w_up), as_bf16(w_down)

    def in_proj(x, tm, rope, rope_blocks):
        proj = functools.partial(_norm_proj, x, g_mix, w_in16, tm=tm, tn=1024)
        q = proj(0, width, "rope", rope=rope, rope_blocks=rope_blocks)
        k = proj(width, width, "rope", rope=rope, rope_blocks=rope_blocks)
        v = proj(2 * width, width, "none")
        u = proj(3 * width, d, "none")
        gates = proj(3 * width + d, 2 * d, "sigmoid")
        return q, k, v, u, gates

    def tail(x, z, attn, gates, tm, tm_ffn):
        mix = _glu_mix(z, attn, gates, w_v16, w_g16, tm=tm, tn=512)
        x1 = _out_proj(mix, x, w_o16, tm=tm, tn=512)
        return _ffn(x1, g_ffn, g_fin, w_up16, w_dn16, tm=tm_ffn, tf=1024)

    tm_p = 1024
    xp = x_prompt.reshape(n_b * seq, d)
    qp, kp, vp, up, gp = in_proj(xp, tm_p, (cos_t, sin_a, sin_b), seq // tm_p)
    attn_p = _moba_prompt(qp.reshape(n_b, seq, width), kp.reshape(n_b, seq, width),
                          vp.reshape(n_b, seq, width))
    zp, hpr, hpi = _s5_prompt(up.reshape(n_b, seq, d), wb_re, wb_im, wc_re, wc_im,
                              ab_re_t, ab_im_t, d_row, t_c=256)
    y_prompt = tail(xp, zp.reshape(n_b * seq, d), attn_p.reshape(n_b * seq, width), gp, tm_p, 512)

    xs = x_sample.reshape(n_dec, d)
    rope_s = tuple(jnp.broadcast_to(t[past_len:past_len + 1], (n_dec, LANES))
                   for t in (cos_t, sin_a, sin_b))
    qs, ks, vs, us, gs = in_proj(xs, n_dec, rope_s, 1)
    heads = (n_dec, N_HEADS, HEAD_DIM)
    cache_shape = (n_pool, page, N_HEADS, HEAD_DIM)
    attn_s = _moba_decode(qs.reshape(heads), ks.reshape(heads), vs.reshape(heads),
                          cache_k.reshape(cache_shape), cache_v.reshape(cache_shape), page_table,
                          pages_per_step=4)
    n_state = n_groups * SSM_STATE
    zs, hsr, hsi = _s5_step(us, state_ssm_re.reshape(n_dec, n_state),
                            state_ssm_im.reshape(n_dec, n_state),
                            wb_re, wb_im, wc_re, wc_im,
                            ab_re.reshape(1, n_state), ab_im.reshape(1, n_state), d_row)
    y_sample = tail(xs, zs, attn_s.reshape(n_dec, width), gs, n_dec, n_dec)

    kv_p = (1, n_b, seq, N_HEADS, HEAD_DIM)
    kv_s = (1, n_dec, s_dec, N_HEADS, HEAD_DIM)
    st_p = (1, n_b, n_groups, SSM_STATE)
    st_s = (1, n_dec, n_groups, SSM_STATE)
    return (y_prompt.reshape(n_b, seq, d), y_sample.reshape(n_dec, s_dec, d),
            kp.reshape(kv_p), vp.reshape(kv_p), hpr.reshape(st_p), hpi.reshape(st_p),
            ks.reshape(kv_s), vs.reshape(kv_s), hsr.reshape(st_s), hsi.reshape(st_s))
```

```python
import functools
import math

import jax
import jax.numpy as jnp
from jax import lax
from jax.experimental import pallas as pl
from jax.experimental.pallas import tpu as pltpu

N_HEADS = 16
HEAD_DIM = 128
ROT_DIM = HEAD_DIM // 4
ROT_HALF = ROT_DIM // 2
ROPE_THETA = 500000.0
MOBA_BLOCK = 256
MOBA_TOPK = 3
SSM_GROUP = 16
SSM_STATE = 64
RMS_EPS = 1e-6
NEG = -1e30
MASK_BIAS = -(2.0 ** 100)

LANES = 128
SUBLANES = 8
MXU_DIM = 256
VMEM_LIMIT_BYTES = 60 * 1024 * 1024

SSM_CH_TILE = MXU_DIM
SSM_GROUPS_PER_TILE = SSM_CH_TILE // SSM_GROUP
SSM_STATE_TILE = SSM_GROUPS_PER_TILE * SSM_STATE
SSM_STATE_ROWS = SSM_STATE_TILE // LANES

F32 = jnp.float32
BF16 = jnp.bfloat16


def _cparams(*sem):
    return pltpu.CompilerParams(dimension_semantics=sem, vmem_limit_bytes=VMEM_LIMIT_BYTES)


def _sigmoid(x):
    return 1.0 / (1.0 + jnp.exp(-x))


def _gelu_tanh(x):
    c = math.sqrt(2.0 / math.pi)
    return 0.5 * x * (1.0 + jnp.tanh(c * (x + 0.044715 * (x * x * x))))


def _rms_scale(x):
    return x * lax.rsqrt(jnp.mean(x * x, axis=-1, keepdims=True) + RMS_EPS)


def _rope_table_kernel(inv_ref, cos_ref, sin_a_ref, sin_b_ref):
    rows = cos_ref.shape[0]
    pos = lax.broadcasted_iota(jnp.int32, (rows, LANES), 0).astype(F32)
    lane = lax.broadcasted_iota(jnp.int32, (rows, LANES), 1)
    ang = pos * inv_ref[...]
    c = jnp.cos(ang)
    s = jnp.sin(ang)
    cos_ref[...] = c
    sin_a_ref[...] = jnp.where((lane >= ROT_HALF) & (lane < ROT_DIM), s, 0.0)
    sin_b_ref[...] = jnp.where(lane < ROT_HALF, -s, 0.0)


def _rope_tables(n_pos):
    rows = -(-n_pos // SUBLANES) * SUBLANES
    inv = ROPE_THETA ** (-jnp.arange(ROT_HALF, dtype=F32) / ROT_HALF)
    inv_row = jnp.concatenate([inv, inv, jnp.zeros((LANES - ROT_DIM,), F32)])[None, :]
    out = jax.ShapeDtypeStruct((rows, LANES), F32)
    return pl.pallas_call(_rope_table_kernel, out_shape=(out, out, out), name="rope_tables")(inv_row)


def _ssm_prep_kernel(ls_ref, lr_ref, li_ref, lrr_ref, lir_ref, br_ref, bi_ref,
                     abr_ref, abi_ref, bbr_ref, bbi_ref):
    step = jnp.exp(ls_ref[...])

    def disc(lr, li):
        mag = jnp.exp(lr * step)
        ang = li * step
        ab_re, ab_im = mag * jnp.cos(ang), mag * jnp.sin(ang)
        den = lr * lr + li * li
        nr, ni = ab_re - 1.0, ab_im
        f_re = (nr * lr + ni * li) / den
        f_im = (ni * lr - nr * li) / den
        return ab_re, ab_im, f_re, f_im

    ab_re, ab_im, _, _ = disc(lr_ref[...], li_ref[...])
    abr_ref[...] = ab_re
    abi_ref[...] = ab_im
    _, _, f_re, f_im = disc(lrr_ref[...], lir_ref[...])
    br, bi = br_ref[...], bi_ref[...]
    bbr_ref[...] = f_re * br - f_im * bi
    bbi_ref[...] = f_re * bi + f_im * br


def _ssm_prep(lambda_re, lambda_im, log_step, b_re, b_im):
    g, n = lambda_re.shape
    flat = g, n * SSM_GROUP
    outs = (jax.ShapeDtypeStruct((g, n), F32),) * 2 + (jax.ShapeDtypeStruct(flat, F32),) * 2
    return pl.pallas_call(_ssm_prep_kernel, out_shape=outs, name="ssm_prep")(
        log_step.reshape(g, 1), lambda_re, lambda_im,
        jnp.repeat(lambda_re, SSM_GROUP, axis=1), jnp.repeat(lambda_im, SSM_GROUP, axis=1),
        b_re.reshape(flat), b_im.reshape(flat))


def _block_diag_tiles(w):
    g, r, c = w.shape
    t = g // SSM_GROUPS_PER_TILE
    w = w.reshape(t, SSM_GROUPS_PER_TILE, r, 1, c)
    eye = jnp.eye(SSM_GROUPS_PER_TILE, dtype=w.dtype)[None, :, None, :, None]
    return (w * eye).reshape(t, SSM_GROUPS_PER_TILE * r, SSM_GROUPS_PER_TILE * c).astype(BF16)


def _in_proj_kernel(x_ref, g_ref, w_ref, cos_ref, sa_ref, sb_ref,
                    q_ref, k_ref, v_ref, u_ref, gt_ref, hn_ref, *, ends):
    j = pl.program_id(1)
    q_end, k_end, v_end, u_end = ends

    @pl.when(j == 0)
    def _():
        hn_ref[...] = (_rms_scale(x_ref[...]) * g_ref[...]).astype(BF16)

    acc = jnp.dot(hn_ref[...], w_ref[...], preferred_element_type=F32)

    def rope_into(o_ref):
        cos, sa, sb = cos_ref[...], sa_ref[...], sb_ref[...]
        for h in range(acc.shape[1] // HEAD_DIM):
            xh = acc[:, h * HEAD_DIM:(h + 1) * HEAD_DIM]
            o_ref[:, h * HEAD_DIM:(h + 1) * HEAD_DIM] = (
                xh * cos + pltpu.roll(xh, ROT_HALF, 1) * sa
                + pltpu.roll(xh, HEAD_DIM - ROT_HALF, 1) * sb)

    @pl.when(j < q_end)
    def _():
        rope_into(q_ref)

    @pl.when((j >= q_end) & (j < k_end))
    def _():
        rope_into(k_ref)

    @pl.when((j >= k_end) & (j < v_end))
    def _():
        v_ref[...] = acc

    @pl.when((j >= v_end) & (j < u_end))
    def _():
        u_ref[...] = acc

    @pl.when(j >= u_end)
    def _():
        gt_ref[...] = _sigmoid(acc)


def _in_proj(x, gain, w, rope, rope_blocks, widths, tm, tn):
    m, d = x.shape
    assert all(wd % tn == 0 for wd in widths)
    tiles = [wd // tn for wd in widths]
    starts = [sum(tiles[:n]) for n in range(len(tiles))]

    def out_spec(n):
        return pl.BlockSpec((tm, tn), lambda i, j: (i, jnp.clip(j - starts[n], 0, tiles[n] - 1)))

    table = pl.BlockSpec((tm, LANES), lambda i, j: (i % rope_blocks, 0))
    return pl.pallas_call(
        functools.partial(_in_proj_kernel, ends=tuple(starts[1:])),
        out_shape=tuple(jax.ShapeDtypeStruct((m, wd), F32) for wd in widths),
        grid=(m // tm, sum(tiles)),
        in_specs=[pl.BlockSpec((tm, d), lambda i, j: (i, 0)),
                  pl.BlockSpec((1, d), lambda i, j: (0, 0)),
                  pl.BlockSpec((d, tn), lambda i, j: (0, j)),
                  table, table, table],
        out_specs=tuple(out_spec(n) for n in range(len(widths))),
        scratch_shapes=[pltpu.VMEM((tm, d), BF16)],
        compiler_params=_cparams("parallel", "arbitrary"),
        name="in_proj",
    )(x, gain, w, *rope)


def _topk_selected(gates, i):
    return _topk_rank(gates, i) < float(MOBA_TOPK)


def _topk_rank(gates, i):
    gi = gates[i]
    rank = jnp.zeros(gi.shape, F32)
    for i2, g2 in enumerate(gates):
        if i2 == i:
            continue
        ahead = (g2 >= gi) if i2 < i else (g2 > gi)
        rank = rank + jnp.where(ahead, 1.0, 0.0)
    return rank


def _moba_prompt_kernel(q_ref, k_ref, v_ref, kbias_ref, o_ref):
    seq = q_ref.shape[1]
    nb = seq // MOBA_BLOCK
    q, k, v = q_ref[0], k_ref[0], v_ref[0]
    k_mean = jnp.concatenate(
        [jnp.mean(k[i * MOBA_BLOCK:(i + 1) * MOBA_BLOCK], axis=0, keepdims=True) for i in range(nb)],
        axis=0)
    gate_t = lax.dot_general(k_mean, q, (((1,), (1,)), ((), ())),
                             precision=lax.Precision.HIGHEST, preferred_element_type=F32)
    blk = lax.broadcasted_iota(jnp.int32, (nb, seq), 0)
    own_blk = lax.broadcasted_iota(jnp.int32, (nb, seq), 1) // MOBA_BLOCK
    past = blk < own_blk
    dropped = jnp.zeros((nb, seq), F32)
    for i in range(nb - 1):
        gi = gate_t[i:i + 1, :]
        ahead = (gate_t > gi) | ((gate_t == gi) & (blk < i))
        rank = jnp.sum(jnp.where(ahead & past, 1.0, 0.0), axis=0, keepdims=True)
        dropped = jnp.where((blk == i) & (rank >= float(MOBA_TOPK)), 1.0, dropped)
    dropped = jnp.where(past, dropped, 0.0)
    drop_cols = jnp.concatenate([dropped, jnp.zeros((HEAD_DIM - nb, seq), F32)], axis=0).T

    c = HEAD_DIM ** -0.5 * math.log2(math.e)
    q_aug = jnp.concatenate([(q * c).astype(BF16), drop_cols.astype(BF16)], axis=1)
    k_aug = jnp.concatenate([k.astype(BF16), kbias_ref[...]], axis=1)
    vb = v.astype(BF16)
    row = lax.broadcasted_iota(jnp.int32, (MOBA_BLOCK, MOBA_BLOCK), 0)
    col = lax.broadcasted_iota(jnp.int32, (MOBA_BLOCK, MOBA_BLOCK), 1)
    causal = col <= row
    for j in range(nb):
        rows = slice(j * MOBA_BLOCK, (j + 1) * MOBA_BLOCK)
        n_keys = (j + 1) * MOBA_BLOCK
        s = lax.dot_general(q_aug[rows], k_aug[:n_keys], (((1,), (1,)), ((), ())),
                            preferred_element_type=F32)
        s_own = jnp.where(causal, s[:, j * MOBA_BLOCK:], NEG)
        s = jnp.concatenate([s[:, :j * MOBA_BLOCK], s_own], axis=1) if j else s_own
        p = jnp.exp2(s - jnp.max(s, axis=-1, keepdims=True))
        l = jnp.sum(p, axis=-1, keepdims=True)
        o = jnp.dot(p.astype(BF16), vb[:n_keys], preferred_element_type=F32)
        o_ref[0, rows, :] = o / l


def _moba_prompt(q, k, v):
    b, s, _ = q.shape
    nb = s // MOBA_BLOCK
    assert nb <= HEAD_DIM
    spec = pl.BlockSpec((1, s, HEAD_DIM), lambda bi, h: (bi, 0, h))
    key_blk = jnp.arange(s, dtype=jnp.int32)[:, None] // MOBA_BLOCK
    kbias = jnp.where(key_blk == jnp.arange(HEAD_DIM, dtype=jnp.int32)[None, :], MASK_BIAS, 0.0)
    return pl.pallas_call(
        _moba_prompt_kernel,
        out_shape=jax.ShapeDtypeStruct(q.shape, F32),
        grid=(b, N_HEADS),
        in_specs=[spec, spec, spec, pl.BlockSpec((s, HEAD_DIM), lambda bi, h: (0, 0))],
        out_specs=spec,
        compiler_params=_cparams("parallel", "parallel"),
        name="moba_prompt",
    )(q, k, v, kbias.astype(BF16))


def _moba_decode_kernel(pt_ref, q_ref, kn_ref, vn_ref, *refs, pages_per_step):
    del pt_ref
    k_refs, v_refs = refs[:pages_per_step], refs[pages_per_step:2 * pages_per_step]
    o_ref, g_s, m_s, l_s, o_s = refs[2 * pages_per_step:]
    step = pl.program_id(1)
    page = k_refs[0].shape[1]
    pages_per_blk = MOBA_BLOCK // page
    blks_per_step = pages_per_step // pages_per_blk
    rows_per_page = page * N_HEADS
    n_cols = pages_per_blk * rows_per_page
    scale = HEAD_DIM ** -0.5
    lanes = (N_HEADS, HEAD_DIM)
    q = q_ref[0]
    qb = q.astype(BF16)
    col_head = lax.broadcasted_iota(jnp.int32, (N_HEADS, n_cols), 1) % N_HEADS
    own = col_head == lax.broadcasted_iota(jnp.int32, (N_HEADS, n_cols), 0)

    for bi in range(blks_per_step):
        kp = [k_refs[bi * pages_per_blk + p][0] for p in range(pages_per_blk)]
        vp = [v_refs[bi * pages_per_blk + p][0] for p in range(pages_per_blk)]
        k_rows = jnp.concatenate([k.reshape(rows_per_page, HEAD_DIM) for k in kp], axis=0)
        v_rows = jnp.concatenate([v.reshape(rows_per_page, HEAD_DIM) for v in vp], axis=0)
        k_sum = kp[0].sum(axis=0)
        for k in kp[1:]:
            k_sum = k_sum + k.sum(axis=0)
        gate = jnp.sum(q * (k_sum * (1.0 / MOBA_BLOCK)), axis=-1, keepdims=True)
        s = lax.dot_general(qb, k_rows.astype(BF16), (((1,), (1,)), ((), ())),
                            preferred_element_type=F32) * scale
        s = jnp.where(own, s, NEG)
        m = jnp.max(s, axis=-1, keepdims=True)
        p = jnp.exp(s - m)
        l = jnp.sum(p, axis=-1, keepdims=True)
        o = jnp.dot(p.astype(BF16), v_rows.astype(BF16), preferred_element_type=F32)
        blk = step * blks_per_step + bi
        g_s[blk] = jnp.broadcast_to(gate, lanes)
        m_s[blk] = jnp.broadcast_to(m, lanes)
        l_s[blk] = jnp.broadcast_to(l, lanes)
        o_s[blk] = o

    @pl.when(step == pl.num_programs(1) - 1)
    def _():
        nb = g_s.shape[0]
        kn = kn_ref[0].astype(BF16).astype(F32)
        v_own = vn_ref[0].astype(BF16).astype(F32)
        s_own = jnp.sum(qb.astype(F32) * kn, axis=-1, keepdims=True) * scale
        s_own = jnp.broadcast_to(s_own, lanes)
        gates = [g_s[i] for i in range(nb)]
        sel = [_topk_selected(gates, i) for i in range(nb)]
        m_all = s_own
        for i in range(nb):
            m_all = jnp.maximum(m_all, jnp.where(sel[i], m_s[i], NEG))
        w_own = jnp.exp(s_own - m_all)
        den = w_own
        num = w_own.astype(BF16).astype(F32) * v_own
        for i in range(nb):
            w = jnp.where(sel[i], jnp.exp(m_s[i] - m_all), 0.0)
            den = den + w * l_s[i]
            num = num + w * o_s[i]
        o_ref[0] = num / den


def _moba_decode(q, k_new, v_new, cache_k, cache_v, page_table, pages_per_step):
    n_dec = q.shape[0]
    page = cache_k.shape[1]
    n_pages = page_table.shape[1]
    pages_per_blk = MOBA_BLOCK // page
    assert MOBA_BLOCK % page == 0 and (n_pages * page) % MOBA_BLOCK == 0
    assert pages_per_step % pages_per_blk == 0 and n_pages % pages_per_step == 0
    nb = n_pages // pages_per_blk
    row = pl.BlockSpec((1, N_HEADS, HEAD_DIM), lambda b, i, pt: (b, 0, 0))

    def page_spec(p):
        return pl.BlockSpec((1, page, N_HEADS, HEAD_DIM),
                            lambda b, i, pt: (pt[b, i * pages_per_step + p], 0, 0, 0))

    pages = [page_spec(p) for p in range(pages_per_step)]
    stat = pltpu.VMEM((nb, N_HEADS, HEAD_DIM), F32)
    return pl.pallas_call(
        functools.partial(_moba_decode_kernel, pages_per_step=pages_per_step),
        out_shape=jax.ShapeDtypeStruct((n_dec, N_HEADS, HEAD_DIM), F32),
        grid_spec=pltpu.PrefetchScalarGridSpec(
            num_scalar_prefetch=1,
            grid=(n_dec, n_pages // pages_per_step),
            in_specs=[row, row, row] + pages + pages,
            out_specs=row,
            scratch_shapes=[stat, stat, stat, stat]),
        compiler_params=_cparams("parallel", "arbitrary"),
        name="moba_decode",
    )(page_table, q, k_new, v_new, *([cache_k] * pages_per_step), *([cache_v] * pages_per_step))


def _s5_prompt_kernel(u_ref, wbr_ref, wbi_ref, wcr_ref, wci_ref, ar_ref, ai_ref, d_ref,
                      z_ref, hr_out, hi_out, xr_s, xi_s, hr_s, hi_s):
    c = pl.program_id(1)
    n_b, t_c = u_ref.shape[0], u_ref.shape[1]

    @pl.when(c == 0)
    def _():
        hr_s[...] = jnp.zeros_like(hr_s)
        hi_s[...] = jnp.zeros_like(hi_s)

    u_all = u_ref[...].reshape(n_b * t_c, u_ref.shape[2])
    ub = u_all.astype(BF16)
    xr = jnp.dot(ub, wbr_ref[0], preferred_element_type=F32)
    xi = jnp.dot(ub, wbi_ref[0], preferred_element_type=F32)
    for b in range(n_b):
        rows_b = slice(b * t_c, (b + 1) * t_c)
        for j in range(SSM_STATE_ROWS):
            cols = slice(j * LANES, (j + 1) * LANES)
            xr_s[b, pl.ds(j, t_c, stride=SSM_STATE_ROWS), :] = xr[rows_b, cols]
            xi_s[b, pl.ds(j, t_c, stride=SSM_STATE_ROWS), :] = xi[rows_b, cols]

    ar, ai = ar_ref[0], ai_ref[0]

    def step(t, carry):
        rows = pl.ds(pl.multiple_of(t * SSM_STATE_ROWS, SSM_STATE_ROWS), SSM_STATE_ROWS)
        new = []
        for b in range(n_b):
            hr, hi = carry[2 * b], carry[2 * b + 1]
            nhr = ar * hr - ai * hi + xr_s[b, rows, :]
            nhi = ar * hi + ai * hr + xi_s[b, rows, :]
            xr_s[b, rows, :] = nhr
            xi_s[b, rows, :] = nhi
            new += [nhr, nhi]
        return tuple(new)

    carry0 = []
    for b in range(n_b):
        carry0 += [hr_s[b], hi_s[b]]
    carry = lax.fori_loop(0, t_c, step, tuple(carry0))
    for b in range(n_b):
        hr_s[b] = carry[2 * b]
        hi_s[b] = carry[2 * b + 1]

    @pl.when(c == pl.num_programs(1) - 1)
    def _():
        for b in range(n_b):
            hr_out[b, 0] = carry[2 * b]
            hi_out[b, 0] = carry[2 * b + 1]

    def gather_states(s_ref):
        return jnp.concatenate(
            [jnp.concatenate([s_ref[b, pl.ds(j, t_c, stride=SSM_STATE_ROWS), :].astype(BF16)
                              for j in range(SSM_STATE_ROWS)], axis=1) for b in range(n_b)], axis=0)

    y = (jnp.dot(gather_states(xr_s), wcr_ref[0], preferred_element_type=F32)
         - jnp.dot(gather_states(xi_s), wci_ref[0], preferred_element_type=F32)
         + d_ref[...] * u_all)
    z_ref[...] = _gelu_tanh(y).astype(z_ref.dtype).reshape(z_ref.shape)


def _s5_prompt(u, wb_re, wb_im, wc_re, wc_im, ab_re, ab_im, d_skip, t_c):
    n_b, seq, width = u.shape
    n_t = width // SSM_CH_TILE
    state = jax.ShapeDtypeStruct((n_b, n_t, SSM_STATE_ROWS, LANES), F32)
    wb_spec = pl.BlockSpec((1, SSM_CH_TILE, SSM_STATE_TILE), lambda kt, c: (kt, 0, 0))
    wc_spec = pl.BlockSpec((1, SSM_STATE_TILE, SSM_CH_TILE), lambda kt, c: (kt, 0, 0))
    a_spec = pl.BlockSpec((1, SSM_STATE_ROWS, LANES), lambda kt, c: (kt, 0, 0))
    u_spec = pl.BlockSpec((n_b, t_c, SSM_CH_TILE), lambda kt, c: (0, c, kt))
    h_spec = pl.BlockSpec((n_b, 1, SSM_STATE_ROWS, LANES), lambda kt, c: (0, kt, 0, 0))
    x_scr = pltpu.VMEM((n_b, t_c * SSM_STATE_ROWS, LANES), F32)
    h_scr = pltpu.VMEM((n_b, SSM_STATE_ROWS, LANES), F32)
    return pl.pallas_call(
        _s5_prompt_kernel,
        out_shape=(jax.ShapeDtypeStruct(u.shape, BF16), state, state),
        grid=(n_t, seq // t_c),
        in_specs=[u_spec, wb_spec, wb_spec, wc_spec, wc_spec, a_spec, a_spec,
                  pl.BlockSpec((1, SSM_CH_TILE), lambda kt, c: (0, kt))],
        out_specs=(u_spec, h_spec, h_spec),
        scratch_shapes=[x_scr, x_scr, h_scr, h_scr],
        compiler_params=_cparams("parallel", "arbitrary"),
        name="s5_prompt",
    )(u, wb_re, wb_im, wc_re, wc_im, ab_re, ab_im, d_skip)


def _s5_step_kernel(u_ref, h0r_ref, h0i_ref, wbr_ref, wbi_ref, wcr_ref, wci_ref, ar_ref, ai_ref,
                    d_ref, z_ref, hr_out, hi_out):
    u = u_ref[...]
    ub = u.astype(BF16)
    ar, ai = ar_ref[...], ai_ref[...]
    h0r, h0i = h0r_ref[...], h0i_ref[...]
    hr = jnp.dot(ub, wbr_ref[0], preferred_element_type=F32) + (ar * h0r - ai * h0i)
    hi = jnp.dot(ub, wbi_ref[0], preferred_element_type=F32) + (ar * h0i + ai * h0r)
    hr_out[...] = hr
    hi_out[...] = hi
    y = (jnp.dot(hr.astype(BF16), wcr_ref[0], preferred_element_type=F32)
         - jnp.dot(hi.astype(BF16), wci_ref[0], preferred_element_type=F32)
         + d_ref[...] * u)
    z_ref[...] = _gelu_tanh(y).astype(z_ref.dtype)


def _s5_step(u, h0_re, h0_im, wb_re, wb_im, wc_re, wc_im, ab_re, ab_im, d_skip):
    n_seq, width = u.shape
    n_t = width // SSM_CH_TILE
    n_state = h0_re.shape[1]
    state = jax.ShapeDtypeStruct((n_seq, n_state), F32)
    u_spec = pl.BlockSpec((n_seq, SSM_CH_TILE), lambda kt: (0, kt))
    h_spec = pl.BlockSpec((n_seq, SSM_STATE_TILE), lambda kt: (0, kt))
    wb_spec = pl.BlockSpec((1, SSM_CH_TILE, SSM_STATE_TILE), lambda kt: (kt, 0, 0))
    wc_spec = pl.BlockSpec((1, SSM_STATE_TILE, SSM_CH_TILE), lambda kt: (kt, 0, 0))
    a_spec = pl.BlockSpec((1, SSM_STATE_TILE), lambda kt: (0, kt))
    return pl.pallas_call(
        _s5_step_kernel,
        out_shape=(jax.ShapeDtypeStruct(u.shape, BF16), state, state),
        grid=(n_t,),
        in_specs=[u_spec, h_spec, h_spec, wb_spec, wb_spec, wc_spec, wc_spec, a_spec, a_spec,
                  pl.BlockSpec((1, SSM_CH_TILE), lambda kt: (0, kt))],
        out_specs=(u_spec, h_spec, h_spec),
        compiler_params=_cparams("parallel"),
        name="s5_step",
    )(u, h0_re, h0_im, wb_re, wb_im, wc_re, wc_im, ab_re, ab_im, d_skip)


def _glu_mix_kernel(z_ref, attn_ref, ga_ref, gs_ref, wv_ref, wg_ref, o_ref):
    z = z_ref[...]
    val = jnp.dot(z, wv_ref[...], preferred_element_type=F32)
    gat = jnp.dot(z, wg_ref[...], preferred_element_type=F32)
    mix = ga_ref[...] * attn_ref[...] + gs_ref[...] * (val * _sigmoid(gat))
    o_ref[...] = mix.astype(o_ref.dtype)


def _glu_mix(z, attn, gates, w_glu_v, w_glu_g, tm, tn):
    m, d = z.shape
    gs_off = d // tn
    col = pl.BlockSpec((tm, tn), lambda i, j: (i, j))
    w_spec = pl.BlockSpec((d, tn), lambda i, j: (0, j))
    return pl.pallas_call(
        _glu_mix_kernel,
        out_shape=jax.ShapeDtypeStruct((m, d), BF16),
        grid=(m // tm, d // tn),
        in_specs=[pl.BlockSpec((tm, d), lambda i, j: (i, 0)), col, col,
                  pl.BlockSpec((tm, tn), lambda i, j: (i, gs_off + j)), w_spec, w_spec],
        out_specs=col,
        compiler_params=_cparams("parallel", "arbitrary"),
        name="glu_mix",
    )(z, attn, gates, gates, w_glu_v, w_glu_g)


def _out_proj_kernel(mix_ref, x_ref, w_ref, o_ref):
    o_ref[...] = x_ref[...] + jnp.dot(mix_ref[...], w_ref[...], preferred_element_type=F32)


def _out_proj(mix, x, w_out, tm, tn):
    m, d = x.shape
    col = pl.BlockSpec((tm, tn), lambda i, j: (i, j))
    return pl.pallas_call(
        _out_proj_kernel,
        out_shape=jax.ShapeDtypeStruct((m, d), F32),
        grid=(m // tm, d // tn),
        in_specs=[pl.BlockSpec((tm, d), lambda i, j: (i, 0)), col,
                  pl.BlockSpec((d, tn), lambda i, j: (0, j))],
        out_specs=col,
        compiler_params=_cparams("parallel", "arbitrary"),
        name="out_proj",
    )(mix, x, w_out)


def _ffn_kernel(x_ref, g_ref, gf_ref, wu_ref, wd_ref, o_ref, hn_ref):
    f = pl.program_id(1)

    @pl.when(f == 0)
    def _():
        x = x_ref[...]
        hn_ref[...] = (_rms_scale(x) * g_ref[...]).astype(BF16)
        o_ref[...] = x

    up = jnp.dot(hn_ref[...], wu_ref[...], preferred_element_type=F32)
    act = jnp.square(jnp.maximum(up, 0.0))
    o_ref[...] += jnp.dot(act.astype(BF16), wd_ref[...], preferred_element_type=F32)

    @pl.when(f == pl.num_programs(1) - 1)
    def _():
        o_ref[...] = _rms_scale(o_ref[...]) * gf_ref[...]


def _ffn(x, norm_ffn, norm_final, w_up, w_down, tm, tf):
    m, d = x.shape
    d_ff = w_up.shape[1]
    row = pl.BlockSpec((tm, d), lambda i, f: (i, 0))
    vec = pl.BlockSpec((1, d), lambda i, f: (0, 0))
    return pl.pallas_call(
        _ffn_kernel,
        out_shape=jax.ShapeDtypeStruct((m, d), F32),
        grid=(m // tm, d_ff // tf),
        in_specs=[row, vec, vec,
                  pl.BlockSpec((d, tf), lambda i, f: (0, f)),
                  pl.BlockSpec((tf, d), lambda i, f: (f, 0))],
        out_specs=row,
        scratch_shapes=[pltpu.VMEM((tm, d), BF16)],
        compiler_params=_cparams("parallel", "arbitrary"),
        name="ffn",
    )(x, norm_ffn, norm_final, w_up, w_down)


def kernel(x_prompt, x_sample, cache_k, cache_v, state_ssm_re, state_ssm_im, page_table, norm_mix, w_in, lambda_re, lambda_im, log_step, b_re, b_im, c_re, c_im, d_skip, w_glu_v, w_glu_g, w_out, norm_ffn, w_up, w_down, norm_final):
    depth = w_in.shape[0]
    assert depth == 1, "single trunk layer"
    n_b, seq, d = x_prompt.shape
    n_dec, s_dec, _ = x_sample.shape
    assert s_dec == 1
    width = N_HEADS * HEAD_DIM
    n_pool, page = cache_k.shape[1], cache_k.shape[2]
    past_len = page_table.shape[1] * page
    n_groups = lambda_re.shape[1]

    cos_t, sin_a, sin_b = _rope_tables(max(seq, past_len + s_dec))
    ab_re, ab_im, bb_re, bb_im = _ssm_prep(lambda_re[0], lambda_im[0], log_step[0], b_re[0], b_im[0])
    to_in = lambda bb: _block_diag_tiles(
        bb.reshape(n_groups, SSM_STATE, SSM_GROUP).transpose(0, 2, 1))
    wb_re, wb_im = to_in(bb_re), to_in(bb_im)
    to_out = lambda cc: _block_diag_tiles(cc.transpose(0, 2, 1))
    wc_re, wc_im = to_out(c_re[0]), to_out(c_im[0])
    n_t = d // SSM_CH_TILE
    ab_re_t = ab_re.reshape(n_t, SSM_STATE_ROWS, LANES)
    ab_im_t = ab_im.reshape(n_t, SSM_STATE_ROWS, LANES)

    g_mix, g_ffn, g_fin = norm_mix.reshape(1, d), norm_ffn.reshape(1, d), norm_final.reshape(1, d)
    d_row = d_skip.reshape(1, d)
    as_bf16 = lambda w: w.reshape(w.shape[1:]).astype(BF16)
    w_in16, w_v16, w_g16, w_o16 = as_bf16(w_in), as_bf16(w_glu_v), as_bf16(w_glu_g), as_bf16(w_out)
    w_up16, w_dn16 = as_bf16(w_up), as_bf16(w_down)

    def in_proj(x, tm, rope, rope_blocks):
        return _in_proj(x, g_mix, w_in16, rope, rope_blocks, (width, width, width, d, 2 * d),
                        tm=tm, tn=512)

    def tail(x, z, attn, gates, tm, tm_ffn):
        mix = _glu_mix(z, attn, gates, w_v16, w_g16, tm=tm, tn=512)
        x1 = _out_proj(mix, x, w_o16, tm=tm, tn=512)
        return _ffn(x1, g_ffn, g_fin, w_up16, w_dn16, tm=tm_ffn, tf=1024)

    tm_p = 1024
    xp = x_prompt.reshape(n_b * seq, d)
    qp, kp, vp, up, gp = in_proj(xp, tm_p, (cos_t, sin_a, sin_b), seq // tm_p)
    attn_p = _moba_prompt(qp.reshape(n_b, seq, width), kp.reshape(n_b, seq, width),
                          vp.reshape(n_b, seq, width))
    zp, hpr, hpi = _s5_prompt(up.reshape(n_b, seq, d), wb_re, wb_im, wc_re, wc_im,
                              ab_re_t, ab_im_t, d_row, t_c=256)
    y_prompt = tail(xp, zp.reshape(n_b * seq, d), attn_p.reshape(n_b * seq, width), gp, tm_p, 512)

    xs = x_sample.reshape(n_dec, d)
    rope_s = tuple(jnp.broadcast_to(t[past_len:past_len + 1], (n_dec, LANES))
                   for t in (cos_t, sin_a, sin_b))
    qs, ks, vs, us, gs = in_proj(xs, n_dec, rope_s, 1)
    heads = (n_dec, N_HEADS, HEAD_DIM)
    cache_shape = (n_pool, page, N_HEADS, HEAD_DIM)
    attn_s = _moba_decode(qs.reshape(heads), ks.reshape(heads), vs.reshape(heads),
                          cache_k.reshape(cache_shape), cache_v.reshape(cache_shape), page_table,
                          pages_per_step=4)
    n_state = n_groups * SSM_STATE
    zs, hsr, hsi = _s5_step(us, state_ssm_re.reshape(n_dec, n_state),
                            state_ssm_im.reshape(n_dec, n_state),
                            wb_re, wb_im, wc_re, wc_im,
                            ab_re.reshape(1, n_state), ab_im.reshape(1, n_state), d_row)
    y_sample = tail(xs, zs, attn_s.reshape(n_dec, width), gs, n_dec, n_dec)

    kv_p = (1, n_b, seq, N_HEADS, HEAD_DIM)
    kv_s = (1, n_dec, s_dec, N_HEADS, HEAD_DIM)
    st_p = (1, n_b, n_groups, SSM_STATE)
    st_s = (1, n_dec, n_groups, SSM_STATE)
    return (y_prompt.reshape(n_b, seq, d), y_sample.reshape(n_dec, s_dec, d),
            kp.reshape(kv_p), vp.reshape(kv_p), hpr.reshape(st_p), hpi.reshape(st_p),
            ks.reshape(kv_s), vs.reshape(kv_s), hsr.reshape(st_s), hsi.reshape(st_s))
```

```python
import functools
import math

import jax
import jax.numpy as jnp
from jax import lax
from jax.experimental import pallas as pl
from jax.experimental.pallas import tpu as pltpu

N_HEADS = 16
HEAD_DIM = 128
ROT_DIM = HEAD_DIM // 4
ROT_HALF = ROT_DIM // 2
ROPE_THETA = 500000.0
MOBA_BLOCK = 256
MOBA_TOPK = 3
SSM_GROUP = 16
SSM_STATE = 64
RMS_EPS = 1e-6
NEG = -1e30
MASK_BIAS = -(2.0 ** 100)

LANES = 128
SUBLANES = 8
MXU_DIM = 256
VMEM_LIMIT_BYTES = 60 * 1024 * 1024

SSM_CH_TILE = MXU_DIM
SSM_GROUPS_PER_TILE = SSM_CH_TILE // SSM_GROUP
SSM_STATE_TILE = SSM_GROUPS_PER_TILE * SSM_STATE
SSM_STATE_ROWS = SSM_STATE_TILE // LANES

ROPE_ROW_CHUNK = 8 * SUBLANES

F32 = jnp.float32
BF16 = jnp.bfloat16


def _cparams(*sem):
    return pltpu.CompilerParams(dimension_semantics=sem, vmem_limit_bytes=VMEM_LIMIT_BYTES)


def _sigmoid(x):
    return 0.5 * jnp.tanh(0.5 * x) + 0.5


def _gelu_tanh(x):
    c = math.sqrt(2.0 / math.pi)
    return 0.5 * x * (1.0 + jnp.tanh(c * (x + 0.044715 * (x * x * x))))


def _rms_scale(x):
    return x * lax.rsqrt(jnp.mean(x * x, axis=-1, keepdims=True) + RMS_EPS)


def _rope_table_kernel(inv_ref, cos_ref, sin_a_ref, sin_b_ref):
    rows = cos_ref.shape[0]
    pos = lax.broadcasted_iota(jnp.int32, (rows, LANES), 0).astype(F32)
    lane = lax.broadcasted_iota(jnp.int32, (rows, LANES), 1)
    ang = pos * inv_ref[...]
    c = jnp.cos(ang)
    s = jnp.sin(ang)
    cos_ref[...] = c
    sin_a_ref[...] = jnp.where((lane >= ROT_HALF) & (lane < ROT_DIM), s, 0.0)
    sin_b_ref[...] = jnp.where(lane < ROT_HALF, -s, 0.0)


def _rope_tables(n_pos):
    rows = -(-n_pos // SUBLANES) * SUBLANES
    inv = ROPE_THETA ** (-jnp.arange(ROT_HALF, dtype=F32) / ROT_HALF)
    inv_row = jnp.concatenate([inv, inv, jnp.zeros((LANES - ROT_DIM,), F32)])[None, :]
    out = jax.ShapeDtypeStruct((rows, LANES), F32)
    return pl.pallas_call(_rope_table_kernel, out_shape=(out, out, out), name="rope_tables")(inv_row)


def _ssm_prep_kernel(ls_ref, lr_ref, li_ref, lrr_ref, lir_ref, br_ref, bi_ref,
                     abr_ref, abi_ref, bbr_ref, bbi_ref):
    step = jnp.exp(ls_ref[...])

    def disc(lr, li):
        mag = jnp.exp(lr * step)
        ang = li * step
        ab_re, ab_im = mag * jnp.cos(ang), mag * jnp.sin(ang)
        den = lr * lr + li * li
        nr, ni = ab_re - 1.0, ab_im
        f_re = (nr * lr + ni * li) / den
        f_im = (ni * lr - nr * li) / den
        return ab_re, ab_im, f_re, f_im

    ab_re, ab_im, _, _ = disc(lr_ref[...], li_ref[...])
    abr_ref[...] = ab_re
    abi_ref[...] = ab_im
    _, _, f_re, f_im = disc(lrr_ref[...], lir_ref[...])
    br, bi = br_ref[...], bi_ref[...]
    bbr_ref[...] = f_re * br - f_im * bi
    bbi_ref[...] = f_re * bi + f_im * br


def _ssm_prep(lambda_re, lambda_im, log_step, b_re, b_im):
    g, n = lambda_re.shape
    flat = g, n * SSM_GROUP
    outs = (jax.ShapeDtypeStruct((g, n), F32),) * 2 + (jax.ShapeDtypeStruct(flat, F32),) * 2
    return pl.pallas_call(_ssm_prep_kernel, out_shape=outs, name="ssm_prep")(
        log_step.reshape(g, 1), lambda_re, lambda_im,
        jnp.repeat(lambda_re, SSM_GROUP, axis=1), jnp.repeat(lambda_im, SSM_GROUP, axis=1),
        b_re.reshape(flat), b_im.reshape(flat))


def _block_diag_tiles(w):
    g, r, c = w.shape
    t, n = g // SSM_GROUPS_PER_TILE, SSM_GROUPS_PER_TILE
    tiled = jnp.tile(w.reshape(t, n * r, c), (1, 1, n))
    row_grp = jnp.arange(n * r, dtype=jnp.int32)[:, None] // r
    col_grp = jnp.arange(n * c, dtype=jnp.int32)[None, :] // c
    return jnp.where(row_grp == col_grp, tiled, 0.0).astype(BF16)


def _in_proj_kernel(x_ref, g_ref, w_ref, cos_ref, sa_ref, sb_ref,
                    q_ref, k_ref, v_ref, u_ref, gt_ref, hn_ref, *, ends):
    j = pl.program_id(1)
    q_end, k_end, v_end, u_end = ends

    @pl.when(j == 0)
    def _():
        hn_ref[...] = (_rms_scale(x_ref[...]) * g_ref[...]).astype(BF16)

    acc = jnp.dot(hn_ref[...], w_ref[...], preferred_element_type=F32)

    def rope_into(o_ref):
        chunk = min(ROPE_ROW_CHUNK, acc.shape[0])
        lane = lax.broadcasted_iota(jnp.int32, (chunk, HEAD_DIM), 1)
        partner = jnp.where(lane < ROT_HALF, lane + ROT_HALF,
                            jnp.where(lane < ROT_DIM, lane - ROT_HALF, lane))
        for r0 in range(0, acc.shape[0], chunk):
            rows = slice(r0, r0 + chunk)
            cos, sin = cos_ref[rows, :], sa_ref[rows, :] + sb_ref[rows, :]
            for h in range(acc.shape[1] // HEAD_DIM):
                cols = slice(h * HEAD_DIM, (h + 1) * HEAD_DIM)
                xh = acc[rows, cols]
                o_ref[rows, cols] = xh * cos + jnp.take_along_axis(xh, partner, axis=1) * sin

    @pl.when(j < q_end)
    def _():
        rope_into(q_ref)

    @pl.when((j >= q_end) & (j < k_end))
    def _():
        rope_into(k_ref)

    @pl.when((j >= k_end) & (j < v_end))
    def _():
        v_ref[...] = acc

    @pl.when((j >= v_end) & (j < u_end))
    def _():
        u_ref[...] = acc

    @pl.when(j >= u_end)
    def _():
        gt_ref[...] = _sigmoid(acc)


def _in_proj(x, gain, w, rope, rope_blocks, widths, tm, tn):
    m, d = x.shape
    assert all(wd % tn == 0 for wd in widths)
    tiles = [wd // tn for wd in widths]
    starts = [sum(tiles[:n]) for n in range(len(tiles))]

    def out_spec(n):
        return pl.BlockSpec((tm, tn), lambda i, j: (i, jnp.clip(j - starts[n], 0, tiles[n] - 1)))

    table = pl.BlockSpec((tm, LANES), lambda i, j: (i % rope_blocks, 0))
    return pl.pallas_call(
        functools.partial(_in_proj_kernel, ends=tuple(starts[1:])),
        out_shape=tuple(jax.ShapeDtypeStruct((m, wd), F32) for wd in widths),
        grid=(m // tm, sum(tiles)),
        in_specs=[pl.BlockSpec((tm, d), lambda i, j: (i, 0)),
                  pl.BlockSpec((1, d), lambda i, j: (0, 0)),
                  pl.BlockSpec((d, tn), lambda i, j: (0, j)),
                  table, table, table],
        out_specs=tuple(out_spec(n) for n in range(len(widths))),
        scratch_shapes=[pltpu.VMEM((tm, d), BF16)],
        compiler_params=_cparams("parallel", "arbitrary"),
        name="in_proj",
    )(x, gain, w, *rope)


def _topk_selected(gates, i):
    return _topk_rank(gates, i) < float(MOBA_TOPK)


def _topk_rank(gates, i):
    gi = gates[i]
    rank = jnp.zeros(gi.shape, F32)
    for i2, g2 in enumerate(gates):
        if i2 == i:
            continue
        ahead = (g2 >= gi) if i2 < i else (g2 > gi)
        rank = rank + jnp.where(ahead, 1.0, 0.0)
    return rank


def _moba_prompt_kernel(q_ref, k_ref, v_ref, kbias_ref, o_ref):
    seq = q_ref.shape[1]
    nb = seq // MOBA_BLOCK
    q, k, v = q_ref[0], k_ref[0], v_ref[0]
    k_mean = jnp.concatenate(
        [jnp.mean(k[i * MOBA_BLOCK:(i + 1) * MOBA_BLOCK], axis=0, keepdims=True) for i in range(nb)],
        axis=0)
    gate_t = lax.dot_general(k_mean, q, (((1,), (1,)), ((), ())),
                             precision=lax.Precision.HIGHEST, preferred_element_type=F32)
    blk = lax.broadcasted_iota(jnp.int32, (nb, seq), 0)
    own_blk = lax.broadcasted_iota(jnp.int32, (nb, seq), 1) // MOBA_BLOCK
    past = blk < own_blk
    dropped = jnp.zeros((nb, seq), F32)
    for i in range(nb - 1):
        gi = gate_t[i:i + 1, :]
        ahead = (gate_t > gi) | ((gate_t == gi) & (blk < i))
        rank = jnp.sum(jnp.where(ahead & past, 1.0, 0.0), axis=0, keepdims=True)
        dropped = jnp.where((blk == i) & (rank >= float(MOBA_TOPK)), 1.0, dropped)
    dropped = jnp.where(past, dropped, 0.0)
    drop_cols = jnp.concatenate([dropped, jnp.zeros((HEAD_DIM - nb, seq), F32)], axis=0).T

    c = HEAD_DIM ** -0.5 * math.log2(math.e)
    q_aug = jnp.concatenate([(q * c).astype(BF16), drop_cols.astype(BF16)], axis=1)
    k_aug = jnp.concatenate([k.astype(BF16), kbias_ref[...]], axis=1)
    vb = v.astype(BF16)
    row = lax.broadcasted_iota(jnp.int32, (MOBA_BLOCK, MOBA_BLOCK), 0)
    col = lax.broadcasted_iota(jnp.int32, (MOBA_BLOCK, MOBA_BLOCK), 1)
    causal = col <= row
    for j in range(nb):
        rows = slice(j * MOBA_BLOCK, (j + 1) * MOBA_BLOCK)
        n_keys = (j + 1) * MOBA_BLOCK
        s = lax.dot_general(q_aug[rows], k_aug[:n_keys], (((1,), (1,)), ((), ())),
                            preferred_element_type=F32)
        s_own = jnp.where(causal, s[:, j * MOBA_BLOCK:], NEG)
        s = jnp.concatenate([s[:, :j * MOBA_BLOCK], s_own], axis=1) if j else s_own
        p = jnp.exp2(s - jnp.max(s, axis=-1, keepdims=True))
        l = jnp.sum(p, axis=-1, keepdims=True)
        o = jnp.dot(p.astype(BF16), vb[:n_keys], preferred_element_type=F32)
        o_ref[0, rows, :] = o / l


def _moba_prompt(q, k, v):
    b, s, _ = q.shape
    nb = s // MOBA_BLOCK
    assert nb <= HEAD_DIM
    spec = pl.BlockSpec((1, s, HEAD_DIM), lambda bi, h: (bi, 0, h))
    key_blk = jnp.arange(s, dtype=jnp.int32)[:, None] // MOBA_BLOCK
    kbias = jnp.where(key_blk == jnp.arange(HEAD_DIM, dtype=jnp.int32)[None, :], MASK_BIAS, 0.0)
    return pl.pallas_call(
        _moba_prompt_kernel,
        out_shape=jax.ShapeDtypeStruct(q.shape, F32),
        grid=(b, N_HEADS),
        in_specs=[spec, spec, spec, pl.BlockSpec((s, HEAD_DIM), lambda bi, h: (0, 0))],
        out_specs=spec,
        compiler_params=_cparams("parallel", "parallel"),
        name="moba_prompt",
    )(q, k, v, kbias.astype(BF16))


def _moba_decode_kernel(pt_ref, q_ref, kn_ref, vn_ref, *refs, pages_per_step):
    del pt_ref
    k_refs, v_refs = refs[:pages_per_step], refs[pages_per_step:2 * pages_per_step]
    o_ref, g_s, m_s, l_s, o_s = refs[2 * pages_per_step:]
    step = pl.program_id(1)
    page = k_refs[0].shape[1]
    pages_per_blk = MOBA_BLOCK // page
    blks_per_step = pages_per_step // pages_per_blk
    rows_per_page = page * N_HEADS
    n_cols = pages_per_blk * rows_per_page
    scale = HEAD_DIM ** -0.5
    lanes = (N_HEADS, HEAD_DIM)
    q = q_ref[0]
    qb = q.astype(BF16)
    col_head = lax.broadcasted_iota(jnp.int32, (N_HEADS, n_cols), 1) % N_HEADS
    own = col_head == lax.broadcasted_iota(jnp.int32, (N_HEADS, n_cols), 0)

    for bi in range(blks_per_step):
        kp = [k_refs[bi * pages_per_blk + p][0] for p in range(pages_per_blk)]
        vp = [v_refs[bi * pages_per_blk + p][0] for p in range(pages_per_blk)]
        k_rows = jnp.concatenate([k.reshape(rows_per_page, HEAD_DIM) for k in kp], axis=0)
        v_rows = jnp.concatenate([v.reshape(rows_per_page, HEAD_DIM) for v in vp], axis=0)
        k_sum = kp[0].sum(axis=0)
        for k in kp[1:]:
            k_sum = k_sum + k.sum(axis=0)
        gate = jnp.sum(q * (k_sum * (1.0 / MOBA_BLOCK)), axis=-1, keepdims=True)
        s = lax.dot_general(qb, k_rows.astype(BF16), (((1,), (1,)), ((), ())),
                            preferred_element_type=F32) * scale
        s = jnp.where(own, s, NEG)
        m = jnp.max(s, axis=-1, keepdims=True)
        p = jnp.exp(s - m)
        l = jnp.sum(p, axis=-1, keepdims=True)
        o = jnp.dot(p.astype(BF16), v_rows.astype(BF16), preferred_element_type=F32)
        blk = step * blks_per_step + bi
        g_s[blk] = jnp.broadcast_to(gate, lanes)
        m_s[blk] = jnp.broadcast_to(m, lanes)
        l_s[blk] = jnp.broadcast_to(l, lanes)
        o_s[blk] = o

    @pl.when(step == pl.num_programs(1) - 1)
    def _():
        nb = g_s.shape[0]
        kn = kn_ref[0].astype(BF16).astype(F32)
        v_own = vn_ref[0].astype(BF16).astype(F32)
        s_own = jnp.sum(qb.astype(F32) * kn, axis=-1, keepdims=True) * scale
        s_own = jnp.broadcast_to(s_own, lanes)
        gates = [g_s[i] for i in range(nb)]
        sel = [_topk_selected(gates, i) for i in range(nb)]
        m_all = s_own
        for i in range(nb):
            m_all = jnp.maximum(m_all, jnp.where(sel[i], m_s[i], NEG))
        w_own = jnp.exp(s_own - m_all)
        den = w_own
        num = w_own.astype(BF16).astype(F32) * v_own
        for i in range(nb):
            w = jnp.where(sel[i], jnp.exp(m_s[i] - m_all), 0.0)
            den = den + w * l_s[i]
            num = num + w * o_s[i]
        o_ref[0] = num / den


def _moba_decode(q, k_new, v_new, cache_k, cache_v, page_table, pages_per_step):
    n_dec = q.shape[0]
    page = cache_k.shape[1]
    n_pages = page_table.shape[1]
    pages_per_blk = MOBA_BLOCK // page
    assert MOBA_BLOCK % page == 0 and (n_pages * page) % MOBA_BLOCK == 0
    assert pages_per_step % pages_per_blk == 0 and n_pages % pages_per_step == 0
    nb = n_pages // pages_per_blk
    row = pl.BlockSpec((1, N_HEADS, HEAD_DIM), lambda b, i, pt: (b, 0, 0))

    def page_spec(p):
        return pl.BlockSpec((1, page, N_HEADS, HEAD_DIM),
                            lambda b, i, pt: (pt[b, i * pages_per_step + p], 0, 0, 0))

    pages = [page_spec(p) for p in range(pages_per_step)]
    stat = pltpu.VMEM((nb, N_HEADS, HEAD_DIM), F32)
    return pl.pallas_call(
        functools.partial(_moba_decode_kernel, pages_per_step=pages_per_step),
        out_shape=jax.ShapeDtypeStruct((n_dec, N_HEADS, HEAD_DIM), F32),
        grid_spec=pltpu.PrefetchScalarGridSpec(
            num_scalar_prefetch=1,
            grid=(n_dec, n_pages // pages_per_step),
            in_specs=[row, row, row] + pages + pages,
            out_specs=row,
            scratch_shapes=[stat, stat, stat, stat]),
        compiler_params=_cparams("parallel", "arbitrary"),
        name="moba_decode",
    )(page_table, q, k_new, v_new, *([cache_k] * pages_per_step), *([cache_v] * pages_per_step))


def _s5_prompt_kernel(u_ref, wbr_ref, wbi_ref, wcr_ref, wci_ref, ar_ref, ai_ref, d_ref,
                      z_ref, hr_out, hi_out, xr_s, xi_s, hr_s, hi_s):
    c = pl.program_id(1)
    n_b, t_c = u_ref.shape[0], u_ref.shape[1]

    @pl.when(c == 0)
    def _():
        hr_s[...] = jnp.zeros_like(hr_s)
        hi_s[...] = jnp.zeros_like(hi_s)

    u_all = u_ref[...].reshape(n_b * t_c, u_ref.shape[2])
    ub = u_all.astype(BF16)
    xr = jnp.dot(ub, wbr_ref[0], preferred_element_type=F32)
    xi = jnp.dot(ub, wbi_ref[0], preferred_element_type=F32)
    for b in range(n_b):
        rows_b = slice(b * t_c, (b + 1) * t_c)
        for j in range(SSM_STATE_ROWS):
            cols = slice(j * LANES, (j + 1) * LANES)
            xr_s[b, pl.ds(j, t_c, stride=SSM_STATE_ROWS), :] = xr[rows_b, cols]
            xi_s[b, pl.ds(j, t_c, stride=SSM_STATE_ROWS), :] = xi[rows_b, cols]

    ar, ai = ar_ref[0], ai_ref[0]

    def step(t, carry):
        rows = pl.ds(pl.multiple_of(t * SSM_STATE_ROWS, SSM_STATE_ROWS), SSM_STATE_ROWS)
        new = []
        for b in range(n_b):
            hr, hi = carry[2 * b], carry[2 * b + 1]
            nhr = ar * hr - ai * hi + xr_s[b, rows, :]
            nhi = ar * hi + ai * hr + xi_s[b, rows, :]
            xr_s[b, rows, :] = nhr
            xi_s[b, rows, :] = nhi
            new += [nhr, nhi]
        return tuple(new)

    carry0 = []
    for b in range(n_b):
        carry0 += [hr_s[b], hi_s[b]]
    carry = lax.fori_loop(0, t_c, step, tuple(carry0))
    for b in range(n_b):
        hr_s[b] = carry[2 * b]
        hi_s[b] = carry[2 * b + 1]

    @pl.when(c == pl.num_programs(1) - 1)
    def _():
        for b in range(n_b):
            hr_out[b, 0] = carry[2 * b]
            hi_out[b, 0] = carry[2 * b + 1]

    def gather_states(s_ref):
        return jnp.concatenate(
            [jnp.concatenate([s_ref[b, pl.ds(j, t_c, stride=SSM_STATE_ROWS), :].astype(BF16)
                              for j in range(SSM_STATE_ROWS)], axis=1) for b in range(n_b)], axis=0)

    y = (jnp.dot(gather_states(xr_s), wcr_ref[0], preferred_element_type=F32)
         - jnp.dot(gather_states(xi_s), wci_ref[0], preferred_element_type=F32)
         + d_ref[...] * u_all)
    z_ref[...] = _gelu_tanh(y).astype(z_ref.dtype).reshape(z_ref.shape)


def _s5_prompt(u, wb_re, wb_im, wc_re, wc_im, ab_re, ab_im, d_skip, t_c):
    n_b, seq, width = u.shape
    n_t = width // SSM_CH_TILE
    state = jax.ShapeDtypeStruct((n_b, n_t, SSM_STATE_ROWS, LANES), F32)
    wb_spec = pl.BlockSpec((1, SSM_CH_TILE, SSM_STATE_TILE), lambda kt, c: (kt, 0, 0))
    wc_spec = pl.BlockSpec((1, SSM_STATE_TILE, SSM_CH_TILE), lambda kt, c: (kt, 0, 0))
    a_spec = pl.BlockSpec((1, SSM_STATE_ROWS, LANES), lambda kt, c: (kt, 0, 0))
    u_spec = pl.BlockSpec((n_b, t_c, SSM_CH_TILE), lambda kt, c: (0, c, kt))
    h_spec = pl.BlockSpec((n_b, 1, SSM_STATE_ROWS, LANES), lambda kt, c: (0, kt, 0, 0))
    x_scr = pltpu.VMEM((n_b, t_c * SSM_STATE_ROWS, LANES), F32)
    h_scr = pltpu.VMEM((n_b, SSM_STATE_ROWS, LANES), F32)
    return pl.pallas_call(
        _s5_prompt_kernel,
        out_shape=(jax.ShapeDtypeStruct(u.shape, BF16), state, state),
        grid=(n_t, seq // t_c),
        in_specs=[u_spec, wb_spec, wb_spec, wc_spec, wc_spec, a_spec, a_spec,
                  pl.BlockSpec((1, SSM_CH_TILE), lambda kt, c: (0, kt))],
        out_specs=(u_spec, h_spec, h_spec),
        scratch_shapes=[x_scr, x_scr, h_scr, h_scr],
        compiler_params=_cparams("parallel", "arbitrary"),
        name="s5_prompt",
    )(u, wb_re, wb_im, wc_re, wc_im, ab_re, ab_im, d_skip)


def _s5_step_kernel(u_ref, h0r_ref, h0i_ref, wbr_ref, wbi_ref, wcr_ref, wci_ref, ar_ref, ai_ref,
                    d_ref, z_ref, hr_out, hi_out):
    u = u_ref[...]
    ub = u.astype(BF16)
    ar, ai = ar_ref[...], ai_ref[...]
    h0r, h0i = h0r_ref[...], h0i_ref[...]
    hr = jnp.dot(ub, wbr_ref[0], preferred_element_type=F32) + (ar * h0r - ai * h0i)
    hi = jnp.dot(ub, wbi_ref[0], preferred_element_type=F32) + (ar * h0i + ai * h0r)
    hr_out[...] = hr
    hi_out[...] = hi
    y = (jnp.dot(hr.astype(BF16), wcr_ref[0], preferred_element_type=F32)
         - jnp.dot(hi.astype(BF16), wci_ref[0], preferred_element_type=F32)
         + d_ref[...] * u)
    z_ref[...] = _gelu_tanh(y).astype(z_ref.dtype)


def _s5_step(u, h0_re, h0_im, wb_re, wb_im, wc_re, wc_im, ab_re, ab_im, d_skip):
    n_seq, width = u.shape
    n_t = width // SSM_CH_TILE
    n_state = h0_re.shape[1]
    state = jax.ShapeDtypeStruct((n_seq, n_state), F32)
    u_spec = pl.BlockSpec((n_seq, SSM_CH_TILE), lambda kt: (0, kt))
    h_spec = pl.BlockSpec((n_seq, SSM_STATE_TILE), lambda kt: (0, kt))
    wb_spec = pl.BlockSpec((1, SSM_CH_TILE, SSM_STATE_TILE), lambda kt: (kt, 0, 0))
    wc_spec = pl.BlockSpec((1, SSM_STATE_TILE, SSM_CH_TILE), lambda kt: (kt, 0, 0))
    a_spec = pl.BlockSpec((1, SSM_STATE_TILE), lambda kt: (0, kt))
    return pl.pallas_call(
        _s5_step_kernel,
        out_shape=(jax.ShapeDtypeStruct(u.shape, BF16), state, state),
        grid=(n_t,),
        in_specs=[u_spec, h_spec, h_spec, wb_spec, wb_spec, wc_spec, wc_spec, a_spec, a_spec,
                  pl.BlockSpec((1, SSM_CH_TILE), lambda kt: (0, kt))],
        out_specs=(u_spec, h_spec, h_spec),
        compiler_params=_cparams("parallel"),
        name="s5_step",
    )(u, h0_re, h0_im, wb_re, wb_im, wc_re, wc_im, ab_re, ab_im, d_skip)


def _glu_mix_kernel(z_ref, attn_ref, ga_ref, gs_ref, wv_ref, wg_ref, o_ref):
    z = z_ref[...]
    val = jnp.dot(z, wv_ref[...], preferred_element_type=F32)
    gat = jnp.dot(z, wg_ref[...], preferred_element_type=F32)
    mix = ga_ref[...] * attn_ref[...] + gs_ref[...] * (val * _sigmoid(gat))
    o_ref[...] = mix.astype(o_ref.dtype)


def _glu_mix(z, attn, gates, w_glu_v, w_glu_g, tm, tn):
    m, d = z.shape
    gs_off = d // tn
    col = pl.BlockSpec((tm, tn), lambda i, j: (i, j))
    w_spec = pl.BlockSpec((d, tn), lambda i, j: (0, j))
    return pl.pallas_call(
        _glu_mix_kernel,
        out_shape=jax.ShapeDtypeStruct((m, d), BF16),
        grid=(m // tm, d // tn),
        in_specs=[pl.BlockSpec((tm, d), lambda i, j: (i, 0)), col, col,
                  pl.BlockSpec((tm, tn), lambda i, j: (i, gs_off + j)), w_spec, w_spec],
        out_specs=col,
        compiler_params=_cparams("parallel", "arbitrary"),
        name="glu_mix",
    )(z, attn, gates, gates, w_glu_v, w_glu_g)


def _out_proj_kernel(mix_ref, x_ref, w_ref, o_ref):
    o_ref[...] = x_ref[...] + jnp.dot(mix_ref[...], w_ref[...], preferred_element_type=F32)


def _out_proj(mix, x, w_out, tm, tn):
    m, d = x.shape
    col = pl.BlockSpec((tm, tn), lambda i, j: (i, j))
    return pl.pallas_call(
        _out_proj_kernel,
        out_shape=jax.ShapeDtypeStruct((m, d), F32),
        grid=(m // tm, d // tn),
        in_specs=[pl.BlockSpec((tm, d), lambda i, j: (i, 0)), col,
                  pl.BlockSpec((d, tn), lambda i, j: (0, j))],
        out_specs=col,
        compiler_params=_cparams("parallel", "arbitrary"),
        name="out_proj",
    )(mix, x, w_out)


def _ffn_kernel(x_ref, g_ref, gf_ref, wu_ref, wd_ref, o_ref, hn_ref):
    f = pl.program_id(1)

    @pl.when(f == 0)
    def _():
        x = x_ref[...]
        hn_ref[...] = (_rms_scale(x) * g_ref[...]).astype(BF16)
        o_ref[...] = x

    up = jnp.dot(hn_ref[...], wu_ref[...], preferred_element_type=F32)
    act = jnp.square(jnp.maximum(up, 0.0))
    o_ref[...] += jnp.dot(act.astype(BF16), wd_ref[...], preferred_element_type=F32)

    @pl.when(f == pl.num_programs(1) - 1)
    def _():
        o_ref[...] = _rms_scale(o_ref[...]) * gf_ref[...]


def _ffn(x, norm_ffn, norm_final, w_up, w_down, tm, tf):
    m, d = x.shape
    d_ff = w_up.shape[1]
    row = pl.BlockSpec((tm, d), lambda i, f: (i, 0))
    vec = pl.BlockSpec((1, d), lambda i, f: (0, 0))
    return pl.pallas_call(
        _ffn_kernel,
        out_shape=jax.ShapeDtypeStruct((m, d), F32),
        grid=(m // tm, d_ff // tf),
        in_specs=[row, vec, vec,
                  pl.BlockSpec((d, tf), lambda i, f: (0, f)),
                  pl.BlockSpec((tf, d), lambda i, f: (f, 0))],
        out_specs=row,
        scratch_shapes=[pltpu.VMEM((tm, d), BF16)],
        compiler_params=_cparams("parallel", "arbitrary"),
        name="ffn",
    )(x, norm_ffn, norm_final, w_up, w_down)


def kernel(x_prompt, x_sample, cache_k, cache_v, state_ssm_re, state_ssm_im, page_table, norm_mix, w_in, lambda_re, lambda_im, log_step, b_re, b_im, c_re, c_im, d_skip, w_glu_v, w_glu_g, w_out, norm_ffn, w_up, w_down, norm_final):
    depth = w_in.shape[0]
    assert depth == 1, "single trunk layer"
    n_b, seq, d = x_prompt.shape
    n_dec, s_dec, _ = x_sample.shape
    assert s_dec == 1
    width = N_HEADS * HEAD_DIM
    n_pool, page = cache_k.shape[1], cache_k.shape[2]
    past_len = page_table.shape[1] * page
    n_groups = lambda_re.shape[1]

    cos_t, sin_a, sin_b = _rope_tables(max(seq, past_len + s_dec))
    ab_re, ab_im, bb_re, bb_im = _ssm_prep(lambda_re[0], lambda_im[0], log_step[0], b_re[0], b_im[0])
    to_in = lambda bb: _block_diag_tiles(
        bb.reshape(n_groups, SSM_STATE, SSM_GROUP).transpose(0, 2, 1))
    wb_re, wb_im = to_in(bb_re), to_in(bb_im)
    to_out = lambda cc: _block_diag_tiles(cc.transpose(0, 2, 1))
    wc_re, wc_im = to_out(c_re[0]), to_out(c_im[0])
    n_t = d // SSM_CH_TILE
    ab_re_t = ab_re.reshape(n_t, SSM_STATE_ROWS, LANES)
    ab_im_t = ab_im.reshape(n_t, SSM_STATE_ROWS, LANES)

    g_mix, g_ffn, g_fin = norm_mix.reshape(1, d), norm_ffn.reshape(1, d), norm_final.reshape(1, d)
    d_row = d_skip.reshape(1, d)
    as_bf16 = lambda w: w.reshape(w.shape[1:]).astype(BF16)
    w_in16, w_v16, w_g16, w_o16 = as_bf16(w_in), as_bf16(w_glu_v), as_bf16(w_glu_g), as_bf16(w_out)
    w_up16, w_dn16 = as_bf16(w_up), as_bf16(w_down)

    def in_proj(x, tm, rope, rope_blocks):
        return _in_proj(x, g_mix, w_in16, rope, rope_blocks, (width, width, width, d, 2 * d),
                        tm=tm, tn=512)

    def tail(x, z, attn, gates, tm, tm_ffn):
        mix = _glu_mix(z, attn, gates, w_v16, w_g16, tm=tm, tn=512)
        x1 = _out_proj(mix, x, w_o16, tm=tm, tn=512)
        return _ffn(x1, g_ffn, g_fin, w_up16, w_dn16, tm=tm_ffn, tf=1024)

    tm_p = 1024
    xp = x_prompt.reshape(n_b * seq, d)
    qp, kp, vp, up, gp = in_proj(xp, tm_p, (cos_t, sin_a, sin_b), seq // tm_p)
    attn_p = _moba_prompt(qp.reshape(n_b, seq, width), kp.reshape(n_b, seq, width),
                          vp.reshape(n_b, seq, width))
    zp, hpr, hpi = _s5_prompt(up.reshape(n_b, seq, d), wb_re, wb_im, wc_re, wc_im,
                              ab_re_t, ab_im_t, d_row, t_c=256)
    y_prompt = tail(xp, zp.reshape(n_b * seq, d), attn_p.reshape(n_b * seq, width), gp, tm_p, 512)

    xs = x_sample.reshape(n_dec, d)
    rope_s = tuple(jnp.broadcast_to(t[past_len:past_len + 1], (n_dec, LANES))
                   for t in (cos_t, sin_a, sin_b))
    qs, ks, vs, us, gs = in_proj(xs, n_dec, rope_s, 1)
    heads = (n_dec, N_HEADS, HEAD_DIM)
    cache_shape = (n_pool, page, N_HEADS, HEAD_DIM)
    attn_s = _moba_decode(qs.reshape(heads), ks.reshape(heads), vs.reshape(heads),
                          cache_k.reshape(cache_shape), cache_v.reshape(cache_shape), page_table,
                          pages_per_step=8)
    n_state = n_groups * SSM_STATE
    zs, hsr, hsi = _s5_step(us, state_ssm_re.reshape(n_dec, n_state),
                            state_ssm_im.reshape(n_dec, n_state),
                            wb_re, wb_im, wc_re, wc_im,
                            ab_re.reshape(1, n_state), ab_im.reshape(1, n_state), d_row)
    y_sample = tail(xs, zs, attn_s.reshape(n_dec, width), gs, n_dec, n_dec)

    kv_p = (1, n_b, seq, N_HEADS, HEAD_DIM)
    kv_s = (1, n_dec, s_dec, N_HEADS, HEAD_DIM)
    st_p = (1, n_b, n_groups, SSM_STATE)
    st_s = (1, n_dec, n_groups, SSM_STATE)
    return (y_prompt.reshape(n_b, seq, d), y_sample.reshape(n_dec, s_dec, d),
            kp.reshape(kv_p), vp.reshape(kv_p), hpr.reshape(st_p), hpi.reshape(st_p),
            ks.reshape(kv_s), vs.reshape(kv_s), hsr.reshape(st_s), hsi.reshape(st_s))
```

```python
import functools
import math

import jax
import jax.numpy as jnp
from jax import lax
from jax.experimental import pallas as pl
from jax.experimental.pallas import tpu as pltpu

N_HEADS = 16
HEAD_DIM = 128
ROT_DIM = HEAD_DIM // 4
ROT_HALF = ROT_DIM // 2
ROPE_THETA = 500000.0
MOBA_BLOCK = 256
MOBA_TOPK = 3
SSM_GROUP = 16
SSM_STATE = 64
RMS_EPS = 1e-6
NEG = -1e30
MASK_BIAS = -(2.0 ** 100)

LANES = 128
SUBLANES = 8
MXU_DIM = 256
VMEM_LIMIT_BYTES = 60 * 1024 * 1024

SSM_CH_TILE = MXU_DIM
SSM_GROUPS_PER_TILE = SSM_CH_TILE // SSM_GROUP
SSM_STATE_TILE = SSM_GROUPS_PER_TILE * SSM_STATE
SSM_STATE_ROWS = SSM_STATE_TILE // LANES

ROPE_ROW_CHUNK = 8 * SUBLANES

F32 = jnp.float32
BF16 = jnp.bfloat16


def _cparams(*sem):
    return pltpu.CompilerParams(dimension_semantics=sem, vmem_limit_bytes=VMEM_LIMIT_BYTES)


def _sigmoid(x):
    return 0.5 * jnp.tanh(0.5 * x) + 0.5


def _gelu_tanh(x):
    c = math.sqrt(2.0 / math.pi)
    return 0.5 * x * (1.0 + jnp.tanh(c * (x + 0.044715 * (x * x * x))))


def _rms_scale(x):
    return x * lax.rsqrt(jnp.mean(x * x, axis=-1, keepdims=True) + RMS_EPS)


def _rope_table_kernel(inv_ref, cos_ref, sin_a_ref, sin_b_ref):
    rows = cos_ref.shape[0]
    pos = lax.broadcasted_iota(jnp.int32, (rows, LANES), 0).astype(F32)
    lane = lax.broadcasted_iota(jnp.int32, (rows, LANES), 1)
    ang = pos * inv_ref[...]
    c = jnp.cos(ang)
    s = jnp.sin(ang)
    cos_ref[...] = c
    sin_a_ref[...] = jnp.where((lane >= ROT_HALF) & (lane < ROT_DIM), s, 0.0)
    sin_b_ref[...] = jnp.where(lane < ROT_HALF, -s, 0.0)


def _rope_tables(n_pos):
    rows = -(-n_pos // SUBLANES) * SUBLANES
    inv = ROPE_THETA ** (-jnp.arange(ROT_HALF, dtype=F32) / ROT_HALF)
    inv_row = jnp.concatenate([inv, inv, jnp.zeros((LANES - ROT_DIM,), F32)])[None, :]
    out = jax.ShapeDtypeStruct((rows, LANES), F32)
    return pl.pallas_call(_rope_table_kernel, out_shape=(out, out, out), name="rope_tables")(inv_row)


def _ssm_prep_kernel(ls_ref, lr_ref, li_ref, lrr_ref, lir_ref, br_ref, bi_ref,
                     abr_ref, abi_ref, bbr_ref, bbi_ref):
    step = jnp.exp(ls_ref[...])

    def disc(lr, li):
        mag = jnp.exp(lr * step)
        ang = li * step
        ab_re, ab_im = mag * jnp.cos(ang), mag * jnp.sin(ang)
        den = lr * lr + li * li
        nr, ni = ab_re - 1.0, ab_im
        f_re = (nr * lr + ni * li) / den
        f_im = (ni * lr - nr * li) / den
        return ab_re, ab_im, f_re, f_im

    ab_re, ab_im, _, _ = disc(lr_ref[...], li_ref[...])
    abr_ref[...] = ab_re
    abi_ref[...] = ab_im
    _, _, f_re, f_im = disc(lrr_ref[...], lir_ref[...])
    br, bi = br_ref[...], bi_ref[...]
    bbr_ref[...] = f_re * br - f_im * bi
    bbi_ref[...] = f_re * bi + f_im * br


def _ssm_prep(lambda_re, lambda_im, log_step, b_re, b_im):
    g, n = lambda_re.shape
    flat = g, n * SSM_GROUP
    outs = (jax.ShapeDtypeStruct((g, n), F32),) * 2 + (jax.ShapeDtypeStruct(flat, F32),) * 2
    return pl.pallas_call(_ssm_prep_kernel, out_shape=outs, name="ssm_prep")(
        log_step.reshape(g, 1), lambda_re, lambda_im,
        jnp.repeat(lambda_re, SSM_GROUP, axis=1), jnp.repeat(lambda_im, SSM_GROUP, axis=1),
        b_re.reshape(flat), b_im.reshape(flat))


def _block_diag_tiles(w):
    g, r, c = w.shape
    t, n = g // SSM_GROUPS_PER_TILE, SSM_GROUPS_PER_TILE
    tiled = jnp.tile(w.reshape(t, n * r, c), (1, 1, n))
    row_grp = jnp.arange(n * r, dtype=jnp.int32)[:, None] // r
    col_grp = jnp.arange(n * c, dtype=jnp.int32)[None, :] // c
    return jnp.where(row_grp == col_grp, tiled, 0.0).astype(BF16)


def _in_proj_kernel(x_ref, g_ref, w_ref, cos_ref, sa_ref, sb_ref,
                    q_ref, k_ref, v_ref, u_ref, gt_ref, hn_ref, *, ends):
    j = pl.program_id(1)
    q_end, k_end, v_end, u_end = ends

    @pl.when(j == 0)
    def _():
        hn_ref[...] = (_rms_scale(x_ref[...]) * g_ref[...]).astype(BF16)

    def proj():
        return jnp.dot(hn_ref[...], w_ref[...], preferred_element_type=F32)

    def rope_into(o_ref):
        acc = proj()
        chunk = min(ROPE_ROW_CHUNK, acc.shape[0])
        lane = lax.broadcasted_iota(jnp.int32, (chunk, HEAD_DIM), 1)
        partner = jnp.where(lane < ROT_HALF, lane + ROT_HALF,
                            jnp.where(lane < ROT_DIM, lane - ROT_HALF, lane))
        for r0 in range(0, acc.shape[0], chunk):
            rows = slice(r0, r0 + chunk)
            cos, sin = cos_ref[rows, :], sa_ref[rows, :] + sb_ref[rows, :]
            for h in range(acc.shape[1] // HEAD_DIM):
                cols = slice(h * HEAD_DIM, (h + 1) * HEAD_DIM)
                xh = acc[rows, cols]
                o_ref[rows, cols] = xh * cos + jnp.take_along_axis(xh, partner, axis=1) * sin

    @pl.when(j < q_end)
    def _():
        rope_into(q_ref)

    @pl.when((j >= q_end) & (j < k_end))
    def _():
        rope_into(k_ref)

    @pl.when((j >= k_end) & (j < v_end))
    def _():
        v_ref[...] = proj()

    @pl.when((j >= v_end) & (j < u_end))
    def _():
        u_ref[...] = proj()

    @pl.when(j >= u_end)
    def _():
        gt_ref[...] = _sigmoid(proj())


def _in_proj(x, gain, w, rope, rope_blocks, widths, tm, tn):
    m, d = x.shape
    assert all(wd % tn == 0 for wd in widths)
    tiles = [wd // tn for wd in widths]
    starts = [sum(tiles[:n]) for n in range(len(tiles))]

    def out_spec(n):
        return pl.BlockSpec((tm, tn), lambda i, j: (i, jnp.clip(j - starts[n], 0, tiles[n] - 1)))

    table = pl.BlockSpec((tm, LANES), lambda i, j: (i % rope_blocks, 0))
    return pl.pallas_call(
        functools.partial(_in_proj_kernel, ends=tuple(starts[1:])),
        out_shape=tuple(jax.ShapeDtypeStruct((m, wd), F32) for wd in widths),
        grid=(m // tm, sum(tiles)),
        in_specs=[pl.BlockSpec((tm, d), lambda i, j: (i, 0)),
                  pl.BlockSpec((1, d), lambda i, j: (0, 0)),
                  pl.BlockSpec((d, tn), lambda i, j: (0, j)),
                  table, table, table],
        out_specs=tuple(out_spec(n) for n in range(len(widths))),
        scratch_shapes=[pltpu.VMEM((tm, d), BF16)],
        compiler_params=_cparams("parallel", "arbitrary"),
        name="in_proj",
    )(x, gain, w, *rope)


def _topk_selected(gates, i):
    return _topk_rank(gates, i) < float(MOBA_TOPK)


def _topk_rank(gates, i):
    gi = gates[i]
    rank = jnp.zeros(gi.shape, F32)
    for i2, g2 in enumerate(gates):
        if i2 == i:
            continue
        ahead = (g2 >= gi) if i2 < i else (g2 > gi)
        rank = rank + jnp.where(ahead, 1.0, 0.0)
    return rank


def _moba_prompt_kernel(q_ref, k_ref, v_ref, kbias_ref, o_ref):
    seq = q_ref.shape[1]
    nb = seq // MOBA_BLOCK
    q, k, v = q_ref[0], k_ref[0], v_ref[0]
    k_mean = jnp.concatenate(
        [jnp.mean(k[i * MOBA_BLOCK:(i + 1) * MOBA_BLOCK], axis=0, keepdims=True) for i in range(nb)],
        axis=0)
    gate_t = lax.dot_general(k_mean, q, (((1,), (1,)), ((), ())),
                             precision=lax.Precision.HIGHEST, preferred_element_type=F32)
    blk = lax.broadcasted_iota(jnp.int32, (nb, seq), 0)
    own_blk = lax.broadcasted_iota(jnp.int32, (nb, seq), 1) // MOBA_BLOCK
    past = blk < own_blk
    dropped = jnp.zeros((nb, seq), F32)
    for i in range(nb - 1):
        gi = gate_t[i:i + 1, :]
        ahead = (gate_t > gi) | ((gate_t == gi) & (blk < i))
        rank = jnp.sum(jnp.where(ahead & past, 1.0, 0.0), axis=0, keepdims=True)
        dropped = jnp.where((blk == i) & (rank >= float(MOBA_TOPK)), 1.0, dropped)
    dropped = jnp.where(past, dropped, 0.0)
    drop_cols = jnp.concatenate([dropped, jnp.zeros((HEAD_DIM - nb, seq), F32)], axis=0).T

    c = HEAD_DIM ** -0.5 * math.log2(math.e)
    q_aug = jnp.concatenate([(q * c).astype(BF16), drop_cols.astype(BF16)], axis=1)
    k_aug = jnp.concatenate([k.astype(BF16), kbias_ref[...]], axis=1)
    vb = v.astype(BF16)
    row = lax.broadcasted_iota(jnp.int32, (MOBA_BLOCK, MOBA_BLOCK), 0)
    col = lax.broadcasted_iota(jnp.int32, (MOBA_BLOCK, MOBA_BLOCK), 1)
    causal = col <= row
    for j in range(nb):
        rows = slice(j * MOBA_BLOCK, (j + 1) * MOBA_BLOCK)
        n_keys = (j + 1) * MOBA_BLOCK
        s = lax.dot_general(q_aug[rows], k_aug[:n_keys], (((1,), (1,)), ((), ())),
                            preferred_element_type=F32)
        s_own = jnp.where(causal, s[:, j * MOBA_BLOCK:], NEG)
        s = jnp.concatenate([s[:, :j * MOBA_BLOCK], s_own], axis=1) if j else s_own
        p = jnp.exp2(s - jnp.max(s, axis=-1, keepdims=True))
        l = jnp.sum(p, axis=-1, keepdims=True)
        o = jnp.dot(p.astype(BF16), vb[:n_keys], preferred_element_type=F32)
        o_ref[0, rows, :] = o / l


def _moba_prompt(q, k, v):
    b, s, _ = q.shape
    nb = s // MOBA_BLOCK
    assert nb <= HEAD_DIM
    spec = pl.BlockSpec((1, s, HEAD_DIM), lambda bi, h: (bi, 0, h))
    key_blk = jnp.arange(s, dtype=jnp.int32)[:, None] // MOBA_BLOCK
    kbias = jnp.where(key_blk == jnp.arange(HEAD_DIM, dtype=jnp.int32)[None, :], MASK_BIAS, 0.0)
    return pl.pallas_call(
        _moba_prompt_kernel,
        out_shape=jax.ShapeDtypeStruct(q.shape, F32),
        grid=(b, N_HEADS),
        in_specs=[spec, spec, spec, pl.BlockSpec((s, HEAD_DIM), lambda bi, h: (0, 0))],
        out_specs=spec,
        compiler_params=_cparams("parallel", "parallel"),
        name="moba_prompt",
    )(q, k, v, kbias.astype(BF16))


def _value_slab_copy(pt_ref, cv_ref, vbuf, sem, seq, slot, rank, head, blk, p):
    pages_per_blk = vbuf.shape[2]
    pg = pt_ref[seq, blk * pages_per_blk + p]
    return pltpu.make_async_copy(cv_ref.at[pg, :, head, :], vbuf.at[slot, rank, p, :, head, :],
                                 sem.at[slot])


def _moba_decode_kernel(pt_ref, q_ref, kn_ref, vn_ref, *refs, pages_per_step, n_seq):
    k_refs = refs[:pages_per_step]
    cv_ref, o_ref, g_s, m_s, l_s, p_s, rank_s, own_s, vbuf, sem = refs[pages_per_step:]
    b, step = pl.program_id(0), pl.program_id(1)
    last_step = step == pl.num_programs(1) - 1
    slot = lax.rem(b, 2)
    page = k_refs[0].shape[1]
    pages_per_blk = MOBA_BLOCK // page
    blks_per_step = pages_per_step // pages_per_blk
    rows_per_page = page * N_HEADS
    n_cols = pages_per_blk * rows_per_page
    nb = g_s.shape[0]
    scale = HEAD_DIM ** -0.5
    lanes = (N_HEADS, HEAD_DIM)

    @pl.when(b < n_seq)
    def _key_pass():
        q = q_ref[0]
        qb = q.astype(BF16)
        col_head = lax.broadcasted_iota(jnp.int32, (N_HEADS, n_cols), 1) % N_HEADS
        own = col_head == lax.broadcasted_iota(jnp.int32, (N_HEADS, n_cols), 0)
        for bi in range(blks_per_step):
            kp = [k_refs[bi * pages_per_blk + p][0] for p in range(pages_per_blk)]
            k_rows = jnp.concatenate([k.reshape(rows_per_page, HEAD_DIM) for k in kp], axis=0)
            k_sum = kp[0].sum(axis=0)
            for k in kp[1:]:
                k_sum = k_sum + k.sum(axis=0)
            gate = jnp.sum(q * (k_sum * (1.0 / MOBA_BLOCK)), axis=-1, keepdims=True)
            s = lax.dot_general(qb, k_rows.astype(BF16), (((1,), (1,)), ((), ())),
                                preferred_element_type=F32) * scale
            s = jnp.where(own, s, NEG)
            m = jnp.max(s, axis=-1, keepdims=True)
            p = jnp.exp(s - m)
            blk = step * blks_per_step + bi
            g_s[blk] = jnp.broadcast_to(gate, lanes)
            m_s[slot, blk] = jnp.broadcast_to(m, lanes)
            l_s[slot, blk] = jnp.broadcast_to(jnp.sum(p, axis=-1, keepdims=True), lanes)
            p_s[slot, blk] = p

    @pl.when(last_step & (b < n_seq))
    def _select_and_fetch():
        qb = q_ref[0].astype(BF16).astype(F32)
        kn = kn_ref[0].astype(BF16).astype(F32)
        s_own = jnp.sum(qb * kn, axis=-1, keepdims=True) * scale
        own_s[slot, 0] = jnp.broadcast_to(s_own, lanes)
        own_s[slot, 1] = vn_ref[0].astype(BF16).astype(F32)
        gates = [g_s[i] for i in range(nb)]
        ranks = [_topk_rank(gates, i) for i in range(nb)]
        for i in range(nb):
            rank_s[slot, i] = ranks[i]
        for r in range(MOBA_TOPK):
            blk_of_head = jnp.zeros(lanes, F32)
            for i in range(nb):
                blk_of_head = jnp.where(ranks[i] == float(r), float(i), blk_of_head)
            blk_of_head = blk_of_head.astype(jnp.int32)
            for h in range(N_HEADS):
                blk = blk_of_head[h, 0]
                for p in range(pages_per_blk):
                    _value_slab_copy(pt_ref, cv_ref, vbuf, sem, b, slot, r, h, blk, p).start()

    @pl.when(last_step & (b >= 1))
    def _merge_previous():
        prev = 1 - slot
        for r in range(MOBA_TOPK):
            for h in range(N_HEADS):
                for p in range(pages_per_blk):
                    _value_slab_copy(pt_ref, cv_ref, vbuf, sem, 0, prev, r, h, 0, p).wait()
        s_own, v_own = own_s[prev, 0], own_s[prev, 1]
        ranks = [rank_s[prev, i] for i in range(nb)]
        m_blk = [m_s[prev, i] for i in range(nb)]
        m_all = s_own
        for i in range(nb):
            m_all = jnp.maximum(m_all, jnp.where(ranks[i] < float(MOBA_TOPK), m_blk[i], NEG))
        w_own = jnp.exp(s_own - m_all)
        den = w_own
        num = w_own.astype(BF16).astype(F32) * v_own
        for r in range(MOBA_TOPK):
            w_r = jnp.zeros(lanes, F32)
            p_r = jnp.zeros((N_HEADS, n_cols), F32)
            for i in range(nb):
                mine = jnp.where(ranks[i] == float(r), 1.0, 0.0)
                w_i = jnp.where(ranks[i] == float(r), jnp.exp(m_blk[i] - m_all), 0.0)
                w_r = w_r + w_i
                den = den + w_i * l_s[prev, i]
                p_r = p_r + jnp.broadcast_to(mine[:, :1], p_r.shape) * p_s[prev, i]
            v_rows = vbuf[prev, r].reshape(n_cols, HEAD_DIM)
            o_r = jnp.dot(p_r.astype(BF16), v_rows.astype(BF16), preferred_element_type=F32)
            num = num + w_r * o_r
        o_ref[0] = num / den


def _moba_decode(q, k_new, v_new, cache_k, cache_v, page_table, pages_per_step):
    n_dec = q.shape[0]
    page = cache_k.shape[1]
    n_pages = page_table.shape[1]
    pages_per_blk = MOBA_BLOCK // page
    assert MOBA_BLOCK % page == 0 and (n_pages * page) % MOBA_BLOCK == 0
    assert pages_per_step % pages_per_blk == 0 and n_pages % pages_per_step == 0
    nb = n_pages // pages_per_blk
    assert nb >= MOBA_TOPK
    last = n_dec - 1
    row = pl.BlockSpec((1, N_HEADS, HEAD_DIM), lambda b, i, pt: (jnp.minimum(b, last), 0, 0))

    def page_spec(p):
        return pl.BlockSpec((1, page, N_HEADS, HEAD_DIM),
                            lambda b, i, pt: (pt[jnp.minimum(b, last), i * pages_per_step + p], 0, 0, 0))

    n_cols = MOBA_BLOCK * N_HEADS
    stat = (N_HEADS, HEAD_DIM)
    return pl.pallas_call(
        functools.partial(_moba_decode_kernel, pages_per_step=pages_per_step, n_seq=n_dec),
        out_shape=jax.ShapeDtypeStruct((n_dec, N_HEADS, HEAD_DIM), F32),
        grid_spec=pltpu.PrefetchScalarGridSpec(
            num_scalar_prefetch=1,
            grid=(n_dec + 1, n_pages // pages_per_step),
            in_specs=[row, row, row] + [page_spec(p) for p in range(pages_per_step)]
            + [pl.BlockSpec(memory_space=pl.ANY)],
            out_specs=pl.BlockSpec((1, N_HEADS, HEAD_DIM), lambda b, i, pt: (jnp.maximum(b - 1, 0), 0, 0)),
            scratch_shapes=[
                pltpu.VMEM((nb,) + stat, F32),
                pltpu.VMEM((2, nb) + stat, F32),
                pltpu.VMEM((2, nb) + stat, F32),
                pltpu.VMEM((2, nb, N_HEADS, n_cols), F32),
                pltpu.VMEM((2, nb) + stat, F32),
                pltpu.VMEM((2, 2) + stat, F32),
                pltpu.VMEM((2, MOBA_TOPK, pages_per_blk, page) + stat, F32),
                pltpu.SemaphoreType.DMA((2,))]),
        compiler_params=_cparams("arbitrary", "arbitrary"),
        name="moba_decode",
    )(page_table, q, k_new, v_new, *([cache_k] * pages_per_step), cache_v)


def _s5_prompt_kernel(u_ref, wbr_ref, wbi_ref, wcr_ref, wci_ref, ar_ref, ai_ref, d_ref,
                      z_ref, hr_out, hi_out, xr_s, xi_s, hr_s, hi_s):
    c = pl.program_id(1)
    n_b, t_c = u_ref.shape[0], u_ref.shape[1]

    @pl.when(c == 0)
    def _():
        hr_s[...] = jnp.zeros_like(hr_s)
        hi_s[...] = jnp.zeros_like(hi_s)

    u_all = u_ref[...].reshape(n_b * t_c, u_ref.shape[2])
    ub = u_all.astype(BF16)
    xr = jnp.dot(ub, wbr_ref[0], preferred_element_type=F32)
    xi = jnp.dot(ub, wbi_ref[0], preferred_element_type=F32)
    for b in range(n_b):
        rows_b = slice(b * t_c, (b + 1) * t_c)
        for j in range(SSM_STATE_ROWS):
            cols = slice(j * LANES, (j + 1) * LANES)
            xr_s[b, pl.ds(j, t_c, stride=SSM_STATE_ROWS), :] = xr[rows_b, cols]
            xi_s[b, pl.ds(j, t_c, stride=SSM_STATE_ROWS), :] = xi[rows_b, cols]

    ar, ai = ar_ref[0], ai_ref[0]

    def step(t, carry):
        rows = pl.ds(pl.multiple_of(t * SSM_STATE_ROWS, SSM_STATE_ROWS), SSM_STATE_ROWS)
        new = []
        for b in range(n_b):
            hr, hi = carry[2 * b], carry[2 * b + 1]
            nhr = ar * hr - ai * hi + xr_s[b, rows, :]
            nhi = ar * hi + ai * hr + xi_s[b, rows, :]
            xr_s[b, rows, :] = nhr
            xi_s[b, rows, :] = nhi
            new += [nhr, nhi]
        return tuple(new)

    carry0 = []
    for b in range(n_b):
        carry0 += [hr_s[b], hi_s[b]]
    carry = lax.fori_loop(0, t_c, step, tuple(carry0))
    for b in range(n_b):
        hr_s[b] = carry[2 * b]
        hi_s[b] = carry[2 * b + 1]

    @pl.when(c == pl.num_programs(1) - 1)
    def _():
        for b in range(n_b):
            hr_out[b, 0] = carry[2 * b]
            hi_out[b, 0] = carry[2 * b + 1]

    def gather_states(s_ref):
        return jnp.concatenate(
            [jnp.concatenate([s_ref[b, pl.ds(j, t_c, stride=SSM_STATE_ROWS), :].astype(BF16)
                              for j in range(SSM_STATE_ROWS)], axis=1) for b in range(n_b)], axis=0)

    y = (jnp.dot(gather_states(xr_s), wcr_ref[0], preferred_element_type=F32)
         - jnp.dot(gather_states(xi_s), wci_ref[0], preferred_element_type=F32)
         + d_ref[...] * u_all)
    z_ref[...] = _gelu_tanh(y).astype(z_ref.dtype).reshape(z_ref.shape)


def _s5_prompt(u, wb_re, wb_im, wc_re, wc_im, ab_re, ab_im, d_skip, t_c):
    n_b, seq, width = u.shape
    n_t = width // SSM_CH_TILE
    state = jax.ShapeDtypeStruct((n_b, n_t, SSM_STATE_ROWS, LANES), F32)
    wb_spec = pl.BlockSpec((1, SSM_CH_TILE, SSM_STATE_TILE), lambda kt, c: (kt, 0, 0))
    wc_spec = pl.BlockSpec((1, SSM_STATE_TILE, SSM_CH_TILE), lambda kt, c: (kt, 0, 0))
    a_spec = pl.BlockSpec((1, SSM_STATE_ROWS, LANES), lambda kt, c: (kt, 0, 0))
    u_spec = pl.BlockSpec((n_b, t_c, SSM_CH_TILE), lambda kt, c: (0, c, kt))
    h_spec = pl.BlockSpec((n_b, 1, SSM_STATE_ROWS, LANES), lambda kt, c: (0, kt, 0, 0))
    x_scr = pltpu.VMEM((n_b, t_c * SSM_STATE_ROWS, LANES), F32)
    h_scr = pltpu.VMEM((n_b, SSM_STATE_ROWS, LANES), F32)
    return pl.pallas_call(
        _s5_prompt_kernel,
        out_shape=(jax.ShapeDtypeStruct(u.shape, BF16), state, state),
        grid=(n_t, seq // t_c),
        in_specs=[u_spec, wb_spec, wb_spec, wc_spec, wc_spec, a_spec, a_spec,
                  pl.BlockSpec((1, SSM_CH_TILE), lambda kt, c: (0, kt))],
        out_specs=(u_spec, h_spec, h_spec),
        scratch_shapes=[x_scr, x_scr, h_scr, h_scr],
        compiler_params=_cparams("parallel", "arbitrary"),
        name="s5_prompt",
    )(u, wb_re, wb_im, wc_re, wc_im, ab_re, ab_im, d_skip)


def _s5_step_kernel(u_ref, h0r_ref, h0i_ref, wbr_ref, wbi_ref, wcr_ref, wci_ref, ar_ref, ai_ref,
                    d_ref, z_ref, hr_out, hi_out):
    u = u_ref[...]
    ub = u.astype(BF16)
    ar, ai = ar_ref[...], ai_ref[...]
    h0r, h0i = h0r_ref[...], h0i_ref[...]
    hr = jnp.dot(ub, wbr_ref[0], preferred_element_type=F32) + (ar * h0r - ai * h0i)
    hi = jnp.dot(ub, wbi_ref[0], preferred_element_type=F32) + (ar * h0i + ai * h0r)
    hr_out[...] = hr
    hi_out[...] = hi
    y = (jnp.dot(hr.astype(BF16), wcr_ref[0], preferred_element_type=F32)
         - jnp.dot(hi.astype(BF16), wci_ref[0], preferred_element_type=F32)
         + d_ref[...] * u)
    z_ref[...] = _gelu_tanh(y).astype(z_ref.dtype)


def _s5_step(u, h0_re, h0_im, wb_re, wb_im, wc_re, wc_im, ab_re, ab_im, d_skip):
    n_seq, width = u.shape
    n_t = width // SSM_CH_TILE
    n_state = h0_re.shape[1]
    state = jax.ShapeDtypeStruct((n_seq, n_state), F32)
    u_spec = pl.BlockSpec((n_seq, SSM_CH_TILE), lambda kt: (0, kt))
    h_spec = pl.BlockSpec((n_seq, SSM_STATE_TILE), lambda kt: (0, kt))
    wb_spec = pl.BlockSpec((1, SSM_CH_TILE, SSM_STATE_TILE), lambda kt: (kt, 0, 0))
    wc_spec = pl.BlockSpec((1, SSM_STATE_TILE, SSM_CH_TILE), lambda kt: (kt, 0, 0))
    a_spec = pl.BlockSpec((1, SSM_STATE_TILE), lambda kt: (0, kt))
    return pl.pallas_call(
        _s5_step_kernel,
        out_shape=(jax.ShapeDtypeStruct(u.shape, BF16), state, state),
        grid=(n_t,),
        in_specs=[u_spec, h_spec, h_spec, wb_spec, wb_spec, wc_spec, wc_spec, a_spec, a_spec,
                  pl.BlockSpec((1, SSM_CH_TILE), lambda kt: (0, kt))],
        out_specs=(u_spec, h_spec, h_spec),
        compiler_params=_cparams("parallel"),
        name="s5_step",
    )(u, h0_re, h0_im, wb_re, wb_im, wc_re, wc_im, ab_re, ab_im, d_skip)


def _glu_mix_kernel(z_ref, attn_ref, ga_ref, gs_ref, wv_ref, wg_ref, o_ref):
    z = z_ref[...]
    val = jnp.dot(z, wv_ref[...], preferred_element_type=F32)
    gat = jnp.dot(z, wg_ref[...], preferred_element_type=F32)
    mix = ga_ref[...] * attn_ref[...] + gs_ref[...] * (val * _sigmoid(gat))
    o_ref[...] = mix.astype(o_ref.dtype)


def _glu_mix(z, attn, gates, w_glu_v, w_glu_g, tm, tn):
    m, d = z.shape
    gs_off = d // tn
    col = pl.BlockSpec((tm, tn), lambda i, j: (i, j))
    w_spec = pl.BlockSpec((d, tn), lambda i, j: (0, j))
    return pl.pallas_call(
        _glu_mix_kernel,
        out_shape=jax.ShapeDtypeStruct((m, d), BF16),
        grid=(m // tm, d // tn),
        in_specs=[pl.BlockSpec((tm, d), lambda i, j: (i, 0)), col, col,
                  pl.BlockSpec((tm, tn), lambda i, j: (i, gs_off + j)), w_spec, w_spec],
        out_specs=col,
        compiler_params=_cparams("parallel", "arbitrary"),
        name="glu_mix",
    )(z, attn, gates, gates, w_glu_v, w_glu_g)


def _out_proj_kernel(mix_ref, x_ref, w_ref, o_ref):
    o_ref[...] = x_ref[...] + jnp.dot(mix_ref[...], w_ref[...], preferred_element_type=F32)


def _out_proj(mix, x, w_out, tm, tn):
    m, d = x.shape
    col = pl.BlockSpec((tm, tn), lambda i, j: (i, j))
    return pl.pallas_call(
        _out_proj_kernel,
        out_shape=jax.ShapeDtypeStruct((m, d), F32),
        grid=(m // tm, d // tn),
        in_specs=[pl.BlockSpec((tm, d), lambda i, j: (i, 0)), col,
                  pl.BlockSpec((d, tn), lambda i, j: (0, j))],
        out_specs=col,
        compiler_params=_cparams("parallel", "arbitrary"),
        name="out_proj",
    )(mix, x, w_out)


def _ffn_kernel(x_ref, g_ref, gf_ref, wu_ref, wd_ref, o_ref, hn_ref):
    f = pl.program_id(1)

    @pl.when(f == 0)
    def _():
        x = x_ref[...]
        hn_ref[...] = (_rms_scale(x) * g_ref[...]).astype(BF16)
        o_ref[...] = x

    up = jnp.dot(hn_ref[...], wu_ref[...], preferred_element_type=F32)
    act = jnp.square(jnp.maximum(up, 0.0))
    o_ref[...] += jnp.dot(act.astype(BF16), wd_ref[...], preferred_element_type=F32)

    @pl.when(f == pl.num_programs(1) - 1)
    def _():
        o_ref[...] = _rms_scale(o_ref[...]) * gf_ref[...]


def _ffn(x, norm_ffn, norm_final, w_up, w_down, tm, tf):
    m, d = x.shape
    d_ff = w_up.shape[1]
    row = pl.BlockSpec((tm, d), lambda i, f: (i, 0))
    vec = pl.BlockSpec((1, d), lambda i, f: (0, 0))
    return pl.pallas_call(
        _ffn_kernel,
        out_shape=jax.ShapeDtypeStruct((m, d), F32),
        grid=(m // tm, d_ff // tf),
        in_specs=[row, vec, vec,
                  pl.BlockSpec((d, tf), lambda i, f: (0, f)),
                  pl.BlockSpec((tf, d), lambda i, f: (f, 0))],
        out_specs=row,
        scratch_shapes=[pltpu.VMEM((tm, d), BF16)],
        compiler_params=_cparams("parallel", "arbitrary"),
        name="ffn",
    )(x, norm_ffn, norm_final, w_up, w_down)


def kernel(x_prompt, x_sample, cache_k, cache_v, state_ssm_re, state_ssm_im, page_table, norm_mix, w_in, lambda_re, lambda_im, log_step, b_re, b_im, c_re, c_im, d_skip, w_glu_v, w_glu_g, w_out, norm_ffn, w_up, w_down, norm_final):
    depth = w_in.shape[0]
    assert depth == 1, "single trunk layer"
    n_b, seq, d = x_prompt.shape
    n_dec, s_dec, _ = x_sample.shape
    assert s_dec == 1
    width = N_HEADS * HEAD_DIM
    n_pool, page = cache_k.shape[1], cache_k.shape[2]
    past_len = page_table.shape[1] * page
    n_groups = lambda_re.shape[1]

    cos_t, sin_a, sin_b = _rope_tables(max(seq, past_len + s_dec))
    ab_re, ab_im, bb_re, bb_im = _ssm_prep(lambda_re[0], lambda_im[0], log_step[0], b_re[0], b_im[0])
    to_in = lambda bb: _block_diag_tiles(
        bb.reshape(n_groups, SSM_STATE, SSM_GROUP).transpose(0, 2, 1))
    wb_re, wb_im = to_in(bb_re), to_in(bb_im)
    to_out = lambda cc: _block_diag_tiles(cc.transpose(0, 2, 1))
    wc_re, wc_im = to_out(c_re[0]), to_out(c_im[0])
    n_t = d // SSM_CH_TILE
    ab_re_t = ab_re.reshape(n_t, SSM_STATE_ROWS, LANES)
    ab_im_t = ab_im.reshape(n_t, SSM_STATE_ROWS, LANES)

    g_mix, g_ffn, g_fin = norm_mix.reshape(1, d), norm_ffn.reshape(1, d), norm_final.reshape(1, d)
    d_row = d_skip.reshape(1, d)
    as_bf16 = lambda w: w.reshape(w.shape[1:]).astype(BF16)
    w_in16, w_v16, w_g16, w_o16 = as_bf16(w_in), as_bf16(w_glu_v), as_bf16(w_glu_g), as_bf16(w_out)
    w_up16, w_dn16 = as_bf16(w_up), as_bf16(w_down)

    def in_proj(x, tm, rope, rope_blocks):
        return _in_proj(x, g_mix, w_in16, rope, rope_blocks, (width, width, width, d, 2 * d),
                        tm=tm, tn=512)

    def tail(x, z, attn, gates, tm, tm_ffn):
        mix = _glu_mix(z, attn, gates, w_v16, w_g16, tm=tm, tn=512)
        x1 = _out_proj(mix, x, w_o16, tm=tm, tn=512)
        return _ffn(x1, g_ffn, g_fin, w_up16, w_dn16, tm=tm_ffn, tf=1024)

    tm_p = 1024
    xp = x_prompt.reshape(n_b * seq, d)
    qp, kp, vp, up, gp = in_proj(xp, tm_p, (cos_t, sin_a, sin_b), seq // tm_p)
    attn_p = _moba_prompt(qp.reshape(n_b, seq, width), kp.reshape(n_b, seq, width),
                          vp.reshape(n_b, seq, width))
    zp, hpr, hpi = _s5_prompt(up.reshape(n_b, seq, d), wb_re, wb_im, wc_re, wc_im,
                              ab_re_t, ab_im_t, d_row, t_c=256)
    y_prompt = tail(xp, zp.reshape(n_b * seq, d), attn_p.reshape(n_b * seq, width), gp, tm_p, 512)

    xs = x_sample.reshape(n_dec, d)
    rope_s = tuple(jnp.broadcast_to(t[past_len:past_len + 1], (n_dec, LANES))
                   for t in (cos_t, sin_a, sin_b))
    qs, ks, vs, us, gs = in_proj(xs, n_dec, rope_s, 1)
    heads = (n_dec, N_HEADS, HEAD_DIM)
    cache_shape = (n_pool, page, N_HEADS, HEAD_DIM)
    attn_s = _moba_decode(qs.reshape(heads), ks.reshape(heads), vs.reshape(heads),
                          cache_k.reshape(cache_shape), cache_v.reshape(cache_shape), page_table,
                          pages_per_step=8)
    n_state = n_groups * SSM_STATE
    zs, hsr, hsi = _s5_step(us, state_ssm_re.reshape(n_dec, n_state),
                            state_ssm_im.reshape(n_dec, n_state),
                            wb_re, wb_im, wc_re, wc_im,
                            ab_re.reshape(1, n_state), ab_im.reshape(1, n_state), d_row)
    y_sample = tail(xs, zs, attn_s.reshape(n_dec, width), gs, n_dec, n_dec)

    kv_p = (1, n_b, seq, N_HEADS, HEAD_DIM)
    kv_s = (1, n_dec, s_dec, N_HEADS, HEAD_DIM)
    st_p = (1, n_b, n_groups, SSM_STATE)
    st_s = (1, n_dec, n_groups, SSM_STATE)
    return (y_prompt.reshape(n_b, seq, d), y_sample.reshape(n_dec, s_dec, d),
            kp.reshape(kv_p), vp.reshape(kv_p), hpr.reshape(st_p), hpi.reshape(st_p),
            ks.reshape(kv_s), vs.reshape(kv_s), hsr.reshape(st_s), hsi.reshape(st_s))
```

```python
import functools
import math

import jax
import jax.numpy as jnp
from jax import lax
from jax.experimental import pallas as pl
from jax.experimental.pallas import tpu as pltpu

N_HEADS = 16
HEAD_DIM = 128
ROT_DIM = HEAD_DIM // 4
ROT_HALF = ROT_DIM // 2
ROPE_THETA = 500000.0
MOBA_BLOCK = 256
MOBA_TOPK = 3
SSM_GROUP = 16
SSM_STATE = 64
RMS_EPS = 1e-6
NEG = -1e30
MASK_BIAS = -(2.0 ** 100)

LANES = 128
SUBLANES = 8
MXU_DIM = 256
VMEM_LIMIT_BYTES = 60 * 1024 * 1024

SSM_CH_TILE = MXU_DIM
SSM_GROUPS_PER_TILE = SSM_CH_TILE // SSM_GROUP
SSM_STATE_TILE = SSM_GROUPS_PER_TILE * SSM_STATE
SSM_STATE_ROWS = SSM_STATE_TILE // LANES

ROPE_ROW_CHUNK = 8 * SUBLANES

F32 = jnp.float32
BF16 = jnp.bfloat16


def _cparams(*sem):
    return pltpu.CompilerParams(dimension_semantics=sem, vmem_limit_bytes=VMEM_LIMIT_BYTES)


def _sigmoid(x):
    return 0.5 * jnp.tanh(0.5 * x) + 0.5


def _gelu_tanh(x):
    c = math.sqrt(2.0 / math.pi)
    return 0.5 * x * (1.0 + jnp.tanh(c * (x + 0.044715 * (x * x * x))))


def _rms_scale(x):
    return x * lax.rsqrt(jnp.mean(x * x, axis=-1, keepdims=True) + RMS_EPS)


def _rope_table_kernel(inv_ref, cos_ref, sin_a_ref, sin_b_ref):
    rows = cos_ref.shape[0]
    pos = lax.broadcasted_iota(jnp.int32, (rows, LANES), 0).astype(F32)
    lane = lax.broadcasted_iota(jnp.int32, (rows, LANES), 1)
    ang = pos * inv_ref[...]
    c = jnp.cos(ang)
    s = jnp.sin(ang)
    cos_ref[...] = c
    sin_a_ref[...] = jnp.where((lane >= ROT_HALF) & (lane < ROT_DIM), s, 0.0)
    sin_b_ref[...] = jnp.where(lane < ROT_HALF, -s, 0.0)


def _rope_tables(n_pos):
    rows = -(-n_pos // SUBLANES) * SUBLANES
    inv = ROPE_THETA ** (-jnp.arange(ROT_HALF, dtype=F32) / ROT_HALF)
    inv_row = jnp.concatenate([inv, inv, jnp.zeros((LANES - ROT_DIM,), F32)])[None, :]
    out = jax.ShapeDtypeStruct((rows, LANES), F32)
    return pl.pallas_call(_rope_table_kernel, out_shape=(out, out, out), name="rope_tables")(inv_row)


def _ssm_prep_kernel(ls_ref, lr_ref, li_ref, lrr_ref, lir_ref, br_ref, bi_ref,
                     abr_ref, abi_ref, bbr_ref, bbi_ref):
    step = jnp.exp(ls_ref[...])

    def disc(lr, li):
        mag = jnp.exp(lr * step)
        ang = li * step
        ab_re, ab_im = mag * jnp.cos(ang), mag * jnp.sin(ang)
        den = lr * lr + li * li
        nr, ni = ab_re - 1.0, ab_im
        f_re = (nr * lr + ni * li) / den
        f_im = (ni * lr - nr * li) / den
        return ab_re, ab_im, f_re, f_im

    ab_re, ab_im, _, _ = disc(lr_ref[...], li_ref[...])
    abr_ref[...] = ab_re
    abi_ref[...] = ab_im
    _, _, f_re, f_im = disc(lrr_ref[...], lir_ref[...])
    br, bi = br_ref[...], bi_ref[...]
    bbr_ref[...] = f_re * br - f_im * bi
    bbi_ref[...] = f_re * bi + f_im * br


def _ssm_prep(lambda_re, lambda_im, log_step, b_re, b_im):
    g, n = lambda_re.shape
    flat = g, n * SSM_GROUP
    outs = (jax.ShapeDtypeStruct((g, n), F32),) * 2 + (jax.ShapeDtypeStruct(flat, F32),) * 2
    return pl.pallas_call(_ssm_prep_kernel, out_shape=outs, name="ssm_prep")(
        log_step.reshape(g, 1), lambda_re, lambda_im,
        jnp.repeat(lambda_re, SSM_GROUP, axis=1), jnp.repeat(lambda_im, SSM_GROUP, axis=1),
        b_re.reshape(flat), b_im.reshape(flat))


def _block_diag_tiles(w):
    g, r, c = w.shape
    t, n = g // SSM_GROUPS_PER_TILE, SSM_GROUPS_PER_TILE
    tiled = jnp.tile(w.reshape(t, n * r, c), (1, 1, n))
    row_grp = jnp.arange(n * r, dtype=jnp.int32)[:, None] // r
    col_grp = jnp.arange(n * c, dtype=jnp.int32)[None, :] // c
    return jnp.where(row_grp == col_grp, tiled, 0.0).astype(BF16)


def _weight_tile(w_ref, w16_ref):
    if w16_ref is None:
        return w_ref[...]
    w16_ref[...] = w_ref[...].astype(BF16)
    return w16_ref[...]


def _cast_mode(w, m, tm):
    cast = w.dtype == F32
    assert not cast or m == tm
    return cast


def _in_proj_kernel(x_ref, g_ref, w_ref, cos_ref, sa_ref, sb_ref,
                    q_ref, k_ref, v_ref, u_ref, gt_ref, *rest, ends):
    w16_ref, hn_ref = (rest[0], rest[1]) if len(rest) == 2 else (None, rest[0])
    j = pl.program_id(1)
    q_end, k_end, v_end, u_end = ends

    @pl.when(j == 0)
    def _():
        hn_ref[...] = (_rms_scale(x_ref[...]) * g_ref[...]).astype(BF16)

    def proj():
        return jnp.dot(hn_ref[...], _weight_tile(w_ref, w16_ref), preferred_element_type=F32)

    def rope_into(o_ref):
        acc = proj()
        chunk = min(ROPE_ROW_CHUNK, acc.shape[0])
        lane = lax.broadcasted_iota(jnp.int32, (chunk, HEAD_DIM), 1)
        partner = jnp.where(lane < ROT_HALF, lane + ROT_HALF,
                            jnp.where(lane < ROT_DIM, lane - ROT_HALF, lane))
        for r0 in range(0, acc.shape[0], chunk):
            rows = slice(r0, r0 + chunk)
            cos, sin = cos_ref[rows, :], sa_ref[rows, :] + sb_ref[rows, :]
            for h in range(acc.shape[1] // HEAD_DIM):
                cols = slice(h * HEAD_DIM, (h + 1) * HEAD_DIM)
                xh = acc[rows, cols]
                o_ref[rows, cols] = xh * cos + jnp.take_along_axis(xh, partner, axis=1) * sin

    @pl.when(j < q_end)
    def _():
        rope_into(q_ref)

    @pl.when((j >= q_end) & (j < k_end))
    def _():
        rope_into(k_ref)

    @pl.when((j >= k_end) & (j < v_end))
    def _():
        v_ref[...] = proj()

    @pl.when((j >= v_end) & (j < u_end))
    def _():
        u_ref[...] = proj()

    @pl.when(j >= u_end)
    def _():
        gt_ref[...] = _sigmoid(proj())


def _in_proj(x, gain, w, rope, rope_blocks, widths, tm, tn):
    m, d = x.shape
    assert all(wd % tn == 0 for wd in widths)
    tiles = [wd // tn for wd in widths]
    starts = [sum(tiles[:n]) for n in range(len(tiles))]

    def out_spec(n):
        return pl.BlockSpec((tm, tn), lambda i, j: (i, jnp.clip(j - starts[n], 0, tiles[n] - 1)))

    table = pl.BlockSpec((tm, LANES), lambda i, j: (i % rope_blocks, 0))
    w_spec = pl.BlockSpec((d, tn), lambda i, j: (0, j))
    out_shape = [jax.ShapeDtypeStruct((m, wd), F32) for wd in widths]
    out_specs = [out_spec(n) for n in range(len(widths))]
    if _cast_mode(w, m, tm):
        out_shape.append(jax.ShapeDtypeStruct(w.shape, BF16))
        out_specs.append(w_spec)
    return pl.pallas_call(
        functools.partial(_in_proj_kernel, ends=tuple(starts[1:])),
        out_shape=tuple(out_shape),
        grid=(m // tm, sum(tiles)),
        in_specs=[pl.BlockSpec((tm, d), lambda i, j: (i, 0)),
                  pl.BlockSpec((1, d), lambda i, j: (0, 0)),
                  w_spec, table, table, table],
        out_specs=tuple(out_specs),
        scratch_shapes=[pltpu.VMEM((tm, d), BF16)],
        compiler_params=_cparams("parallel", "arbitrary"),
        name="in_proj",
    )(x, gain, w, *rope)


def _topk_selected(gates, i):
    return _topk_rank(gates, i) < float(MOBA_TOPK)


def _topk_rank(gates, i):
    gi = gates[i]
    rank = jnp.zeros(gi.shape, F32)
    for i2, g2 in enumerate(gates):
        if i2 == i:
            continue
        ahead = (g2 >= gi) if i2 < i else (g2 > gi)
        rank = rank + jnp.where(ahead, 1.0, 0.0)
    return rank


def _moba_prompt_kernel(q_ref, k_ref, v_ref, kbias_ref, o_ref):
    seq = q_ref.shape[1]
    nb = seq // MOBA_BLOCK
    q, k, v = q_ref[0], k_ref[0], v_ref[0]
    k_mean = jnp.concatenate(
        [jnp.mean(k[i * MOBA_BLOCK:(i + 1) * MOBA_BLOCK], axis=0, keepdims=True) for i in range(nb)],
        axis=0)
    gate_t = lax.dot_general(k_mean, q, (((1,), (1,)), ((), ())),
                             precision=lax.Precision.HIGHEST, preferred_element_type=F32)
    blk = lax.broadcasted_iota(jnp.int32, (nb, seq), 0)
    own_blk = lax.broadcasted_iota(jnp.int32, (nb, seq), 1) // MOBA_BLOCK
    past = blk < own_blk
    dropped = jnp.zeros((nb, seq), F32)
    for i in range(nb - 1):
        gi = gate_t[i:i + 1, :]
        ahead = (gate_t > gi) | ((gate_t == gi) & (blk < i))
        rank = jnp.sum(jnp.where(ahead & past, 1.0, 0.0), axis=0, keepdims=True)
        dropped = jnp.where((blk == i) & (rank >= float(MOBA_TOPK)), 1.0, dropped)
    dropped = jnp.where(past, dropped, 0.0)
    drop_cols = jnp.concatenate([dropped, jnp.zeros((HEAD_DIM - nb, seq), F32)], axis=0).T

    c = HEAD_DIM ** -0.5 * math.log2(math.e)
    q_aug = jnp.concatenate([(q * c).astype(BF16), drop_cols.astype(BF16)], axis=1)
    k_aug = jnp.concatenate([k.astype(BF16), kbias_ref[...]], axis=1)
    vb = v.astype(BF16)
    row = lax.broadcasted_iota(jnp.int32, (MOBA_BLOCK, MOBA_BLOCK), 0)
    col = lax.broadcasted_iota(jnp.int32, (MOBA_BLOCK, MOBA_BLOCK), 1)
    causal = col <= row
    for j in range(nb):
        rows = slice(j * MOBA_BLOCK, (j + 1) * MOBA_BLOCK)
        n_keys = (j + 1) * MOBA_BLOCK
        s = lax.dot_general(q_aug[rows], k_aug[:n_keys], (((1,), (1,)), ((), ())),
                            preferred_element_type=F32)
        s_own = jnp.where(causal, s[:, j * MOBA_BLOCK:], NEG)
        s = jnp.concatenate([s[:, :j * MOBA_BLOCK], s_own], axis=1) if j else s_own
        p = jnp.exp2(s - jnp.max(s, axis=-1, keepdims=True))
        l = jnp.sum(p, axis=-1, keepdims=True)
        o = jnp.dot(p.astype(BF16), vb[:n_keys], preferred_element_type=F32)
        o_ref[0, rows, :] = o / l


def _moba_prompt(q, k, v):
    b, s, _ = q.shape
    nb = s // MOBA_BLOCK
    assert nb <= HEAD_DIM
    spec = pl.BlockSpec((1, s, HEAD_DIM), lambda bi, h: (bi, 0, h))
    key_blk = jnp.arange(s, dtype=jnp.int32)[:, None] // MOBA_BLOCK
    kbias = jnp.where(key_blk == jnp.arange(HEAD_DIM, dtype=jnp.int32)[None, :], MASK_BIAS, 0.0)
    return pl.pallas_call(
        _moba_prompt_kernel,
        out_shape=jax.ShapeDtypeStruct(q.shape, F32),
        grid=(b, N_HEADS),
        in_specs=[spec, spec, spec, pl.BlockSpec((s, HEAD_DIM), lambda bi, h: (0, 0))],
        out_specs=spec,
        compiler_params=_cparams("parallel", "parallel"),
        name="moba_prompt",
    )(q, k, v, kbias.astype(BF16))


def _value_slab_copy(pt_ref, cv_ref, vbuf, sem, seq, slot, rank, head, blk, p):
    pages_per_blk = vbuf.shape[2]
    pg = pt_ref[seq, blk * pages_per_blk + p]
    return pltpu.make_async_copy(cv_ref.at[pg, :, head, :], vbuf.at[slot, rank, p, :, head, :],
                                 sem.at[slot])


def _moba_decode_kernel(pt_ref, q_ref, kn_ref, vn_ref, *refs, pages_per_step, n_seq):
    k_refs = refs[:pages_per_step]
    cv_ref, o_ref, g_s, m_s, l_s, p_s, rank_s, own_s, vbuf, sem = refs[pages_per_step:]
    b, step = pl.program_id(0), pl.program_id(1)
    last_step = step == pl.num_programs(1) - 1
    slot = lax.rem(b, 2)
    page = k_refs[0].shape[1]
    pages_per_blk = MOBA_BLOCK // page
    blks_per_step = pages_per_step // pages_per_blk
    rows_per_page = page * N_HEADS
    n_cols = pages_per_blk * rows_per_page
    nb = g_s.shape[0]
    scale = HEAD_DIM ** -0.5
    lanes = (N_HEADS, HEAD_DIM)

    @pl.when(b < n_seq)
    def _key_pass():
        q = q_ref[0]
        qb = q.astype(BF16)
        col_head = lax.broadcasted_iota(jnp.int32, (N_HEADS, n_cols), 1) % N_HEADS
        own = col_head == lax.broadcasted_iota(jnp.int32, (N_HEADS, n_cols), 0)
        for bi in range(blks_per_step):
            kp = [k_refs[bi * pages_per_blk + p][0] for p in range(pages_per_blk)]
            k_rows = jnp.concatenate([k.reshape(rows_per_page, HEAD_DIM) for k in kp], axis=0)
            k_sum = kp[0].sum(axis=0)
            for k in kp[1:]:
                k_sum = k_sum + k.sum(axis=0)
            gate = jnp.sum(q * (k_sum * (1.0 / MOBA_BLOCK)), axis=-1, keepdims=True)
            s = lax.dot_general(qb, k_rows.astype(BF16), (((1,), (1,)), ((), ())),
                                preferred_element_type=F32) * scale
            s = jnp.where(own, s, NEG)
            m = jnp.max(s, axis=-1, keepdims=True)
            p = jnp.exp(s - m)
            blk = step * blks_per_step + bi
            g_s[blk] = jnp.broadcast_to(gate, lanes)
            m_s[slot, blk] = jnp.broadcast_to(m, lanes)
            l_s[slot, blk] = jnp.broadcast_to(jnp.sum(p, axis=-1, keepdims=True), lanes)
            p_s[slot, blk] = p

    @pl.when(last_step & (b < n_seq))
    def _select_and_fetch():
        qb = q_ref[0].astype(BF16).astype(F32)
        kn = kn_ref[0].astype(BF16).astype(F32)
        s_own = jnp.sum(qb * kn, axis=-1, keepdims=True) * scale
        own_s[slot, 0] = jnp.broadcast_to(s_own, lanes)
        own_s[slot, 1] = vn_ref[0].astype(BF16).astype(F32)
        gates = [g_s[i] for i in range(nb)]
        ranks = [_topk_rank(gates, i) for i in range(nb)]
        for i in range(nb):
            rank_s[slot, i] = ranks[i]
        for r in range(MOBA_TOPK):
            blk_of_head = jnp.zeros(lanes, F32)
            for i in range(nb):
                blk_of_head = jnp.where(ranks[i] == float(r), float(i), blk_of_head)
            blk_of_head = blk_of_head.astype(jnp.int32)
            for h in range(N_HEADS):
                blk = blk_of_head[h, 0]
                for p in range(pages_per_blk):
                    _value_slab_copy(pt_ref, cv_ref, vbuf, sem, b, slot, r, h, blk, p).start()

    @pl.when(last_step & (b >= 1))
    def _merge_previous():
        prev = 1 - slot
        for r in range(MOBA_TOPK):
            for h in range(N_HEADS):
                for p in range(pages_per_blk):
                    _value_slab_copy(pt_ref, cv_ref, vbuf, sem, 0, prev, r, h, 0, p).wait()
        s_own, v_own = own_s[prev, 0], own_s[prev, 1]
        ranks = [rank_s[prev, i] for i in range(nb)]
        m_blk = [m_s[prev, i] for i in range(nb)]
        m_all = s_own
        for i in range(nb):
            m_all = jnp.maximum(m_all, jnp.where(ranks[i] < float(MOBA_TOPK), m_blk[i], NEG))
        w_own = jnp.exp(s_own - m_all)
        den = w_own
        num = w_own.astype(BF16).astype(F32) * v_own
        for r in range(MOBA_TOPK):
            w_r = jnp.zeros(lanes, F32)
            p_r = jnp.zeros((N_HEADS, n_cols), F32)
            for i in range(nb):
                mine = jnp.where(ranks[i] == float(r), 1.0, 0.0)
                w_i = jnp.where(ranks[i] == float(r), jnp.exp(m_blk[i] - m_all), 0.0)
                w_r = w_r + w_i
                den = den + w_i * l_s[prev, i]
                p_r = p_r + jnp.broadcast_to(mine[:, :1], p_r.shape) * p_s[prev, i]
            v_rows = vbuf[prev, r].reshape(n_cols, HEAD_DIM)
            o_r = jnp.dot(p_r.astype(BF16), v_rows.astype(BF16), preferred_element_type=F32)
            num = num + w_r * o_r
        o_ref[0] = num / den


def _moba_decode(q, k_new, v_new, cache_k, cache_v, page_table, pages_per_step):
    n_dec = q.shape[0]
    page = cache_k.shape[1]
    n_pages = page_table.shape[1]
    pages_per_blk = MOBA_BLOCK // page
    assert MOBA_BLOCK % page == 0 and (n_pages * page) % MOBA_BLOCK == 0
    assert pages_per_step % pages_per_blk == 0 and n_pages % pages_per_step == 0
    nb = n_pages // pages_per_blk
    assert nb >= MOBA_TOPK
    last = n_dec - 1
    row = pl.BlockSpec((1, N_HEADS, HEAD_DIM), lambda b, i, pt: (jnp.minimum(b, last), 0, 0))

    def page_spec(p):
        return pl.BlockSpec((1, page, N_HEADS, HEAD_DIM),
                            lambda b, i, pt: (pt[jnp.minimum(b, last), i * pages_per_step + p], 0, 0, 0))

    n_cols = MOBA_BLOCK * N_HEADS
    stat = (N_HEADS, HEAD_DIM)
    return pl.pallas_call(
        functools.partial(_moba_decode_kernel, pages_per_step=pages_per_step, n_seq=n_dec),
        out_shape=jax.ShapeDtypeStruct((n_dec, N_HEADS, HEAD_DIM), F32),
        grid_spec=pltpu.PrefetchScalarGridSpec(
            num_scalar_prefetch=1,
            grid=(n_dec + 1, n_pages // pages_per_step),
            in_specs=[row, row, row] + [page_spec(p) for p in range(pages_per_step)]
            + [pl.BlockSpec(memory_space=pl.ANY)],
            out_specs=pl.BlockSpec((1, N_HEADS, HEAD_DIM), lambda b, i, pt: (jnp.maximum(b - 1, 0), 0, 0)),
            scratch_shapes=[
                pltpu.VMEM((nb,) + stat, F32),
                pltpu.VMEM((2, nb) + stat, F32),
                pltpu.VMEM((2, nb) + stat, F32),
                pltpu.VMEM((2, nb, N_HEADS, n_cols), F32),
                pltpu.VMEM((2, nb) + stat, F32),
                pltpu.VMEM((2, 2) + stat, F32),
                pltpu.VMEM((2, MOBA_TOPK, pages_per_blk, page) + stat, F32),
                pltpu.SemaphoreType.DMA((2,))]),
        compiler_params=_cparams("arbitrary", "arbitrary"),
        name="moba_decode",
    )(page_table, q, k_new, v_new, *([cache_k] * pages_per_step), cache_v)


def _s5_prompt_kernel(u_ref, wbr_ref, wbi_ref, wcr_ref, wci_ref, ar_ref, ai_ref, d_ref,
                      z_ref, hr_out, hi_out, xr_s, xi_s, hr_s, hi_s):
    c = pl.program_id(1)
    n_b, t_c = u_ref.shape[0], u_ref.shape[1]
    n_k = wbr_ref.shape[0]

    @pl.when(c == 0)
    def _():
        hr_s[...] = jnp.zeros_like(hr_s)
        hi_s[...] = jnp.zeros_like(hi_s)

    u_all = u_ref[...].reshape(n_b * t_c, u_ref.shape[2])

    def u_tile(k):
        return u_all[:, k * SSM_CH_TILE:(k + 1) * SSM_CH_TILE]

    def project_in(k):
        ub = u_tile(k).astype(BF16)
        for w_ref, x_s in ((wbr_ref, xr_s), (wbi_ref, xi_s)):
            x = jnp.dot(ub, w_ref[k], preferred_element_type=F32)
            for b in range(n_b):
                for j in range(SSM_STATE_ROWS):
                    x_s[k, b, pl.ds(j, t_c, stride=SSM_STATE_ROWS), :] = (
                        x[b * t_c:(b + 1) * t_c, j * LANES:(j + 1) * LANES])

    def scan(k):
        ar, ai = ar_ref[k], ai_ref[k]
        h = [(hr_s[k, b], hi_s[k, b]) for b in range(n_b)]
        for t in range(t_c):
            rows = slice(t * SSM_STATE_ROWS, (t + 1) * SSM_STATE_ROWS)
            for b in range(n_b):
                hr, hi = h[b]
                nhr = ar * hr - ai * hi + xr_s[k, b, rows, :]
                nhi = ar * hi + ai * hr + xi_s[k, b, rows, :]
                xr_s[k, b, rows, :] = nhr
                xi_s[k, b, rows, :] = nhi
                h[b] = (nhr, nhi)
        for b in range(n_b):
            hr_s[k, b], hi_s[k, b] = h[b]

    def project_out(k):
        def gather_states(s_ref):
            return jnp.concatenate(
                [jnp.concatenate([s_ref[k, b, pl.ds(j, t_c, stride=SSM_STATE_ROWS), :].astype(BF16)
                                  for j in range(SSM_STATE_ROWS)], axis=1) for b in range(n_b)], axis=0)

        y = (jnp.dot(gather_states(xr_s), wcr_ref[k], preferred_element_type=F32)
             - jnp.dot(gather_states(xi_s), wci_ref[k], preferred_element_type=F32)
             + d_ref[:, k * SSM_CH_TILE:(k + 1) * SSM_CH_TILE] * u_tile(k))
        z = _gelu_tanh(y).astype(z_ref.dtype).reshape(n_b, t_c, SSM_CH_TILE)
        z_ref[:, :, k * SSM_CH_TILE:(k + 1) * SSM_CH_TILE] = z

    project_in(0)
    for k in range(n_k):
        if k + 1 < n_k:
            project_in(k + 1)
        scan(k)
        if k > 0:
            project_out(k - 1)
    project_out(n_k - 1)

    @pl.when(c == pl.num_programs(1) - 1)
    def _():
        for k in range(n_k):
            for b in range(n_b):
                hr_out[b, k] = hr_s[k, b]
                hi_out[b, k] = hi_s[k, b]


def _s5_prompt(u, wb_re, wb_im, wc_re, wc_im, ab_re, ab_im, d_skip, t_c, tiles_per_step):
    n_b, seq, width = u.shape
    n_t = width // SSM_CH_TILE
    n_k = tiles_per_step
    assert n_t % n_k == 0
    state = jax.ShapeDtypeStruct((n_b, n_t, SSM_STATE_ROWS, LANES), F32)
    wb_spec = pl.BlockSpec((n_k, SSM_CH_TILE, SSM_STATE_TILE), lambda kt, c: (kt, 0, 0))
    wc_spec = pl.BlockSpec((n_k, SSM_STATE_TILE, SSM_CH_TILE), lambda kt, c: (kt, 0, 0))
    a_spec = pl.BlockSpec((n_k, SSM_STATE_ROWS, LANES), lambda kt, c: (kt, 0, 0))
    u_spec = pl.BlockSpec((n_b, t_c, n_k * SSM_CH_TILE), lambda kt, c: (0, c, kt))
    h_spec = pl.BlockSpec((n_b, n_k, SSM_STATE_ROWS, LANES), lambda kt, c: (0, kt, 0, 0))
    x_scr = pltpu.VMEM((n_k, n_b, t_c * SSM_STATE_ROWS, LANES), F32)
    h_scr = pltpu.VMEM((n_k, n_b, SSM_STATE_ROWS, LANES), F32)
    return pl.pallas_call(
        _s5_prompt_kernel,
        out_shape=(jax.ShapeDtypeStruct(u.shape, BF16), state, state),
        grid=(n_t // n_k, seq // t_c),
        in_specs=[u_spec, wb_spec, wb_spec, wc_spec, wc_spec, a_spec, a_spec,
                  pl.BlockSpec((1, n_k * SSM_CH_TILE), lambda kt, c: (0, kt))],
        out_specs=(u_spec, h_spec, h_spec),
        scratch_shapes=[x_scr, x_scr, h_scr, h_scr],
        compiler_params=_cparams("parallel", "arbitrary"),
        name="s5_prompt",
    )(u, wb_re, wb_im, wc_re, wc_im, ab_re, ab_im, d_skip)


def _s5_step_kernel(u_ref, h0r_ref, h0i_ref, wbr_ref, wbi_ref, wcr_ref, wci_ref, ar_ref, ai_ref,
                    d_ref, z_ref, hr_out, hi_out):
    u = u_ref[...]
    ub = u.astype(BF16)
    ar, ai = ar_ref[...], ai_ref[...]
    h0r, h0i = h0r_ref[...], h0i_ref[...]
    hr = jnp.dot(ub, wbr_ref[0], preferred_element_type=F32) + (ar * h0r - ai * h0i)
    hi = jnp.dot(ub, wbi_ref[0], preferred_element_type=F32) + (ar * h0i + ai * h0r)
    hr_out[...] = hr
    hi_out[...] = hi
    y = (jnp.dot(hr.astype(BF16), wcr_ref[0], preferred_element_type=F32)
         - jnp.dot(hi.astype(BF16), wci_ref[0], preferred_element_type=F32)
         + d_ref[...] * u)
    z_ref[...] = _gelu_tanh(y).astype(z_ref.dtype)


def _s5_step(u, h0_re, h0_im, wb_re, wb_im, wc_re, wc_im, ab_re, ab_im, d_skip):
    n_seq, width = u.shape
    n_t = width // SSM_CH_TILE
    n_state = h0_re.shape[1]
    state = jax.ShapeDtypeStruct((n_seq, n_state), F32)
    u_spec = pl.BlockSpec((n_seq, SSM_CH_TILE), lambda kt: (0, kt))
    h_spec = pl.BlockSpec((n_seq, SSM_STATE_TILE), lambda kt: (0, kt))
    wb_spec = pl.BlockSpec((1, SSM_CH_TILE, SSM_STATE_TILE), lambda kt: (kt, 0, 0))
    wc_spec = pl.BlockSpec((1, SSM_STATE_TILE, SSM_CH_TILE), lambda kt: (kt, 0, 0))
    a_spec = pl.BlockSpec((1, SSM_STATE_TILE), lambda kt: (0, kt))
    return pl.pallas_call(
        _s5_step_kernel,
        out_shape=(jax.ShapeDtypeStruct(u.shape, BF16), state, state),
        grid=(n_t,),
        in_specs=[u_spec, h_spec, h_spec, wb_spec, wb_spec, wc_spec, wc_spec, a_spec, a_spec,
                  pl.BlockSpec((1, SSM_CH_TILE), lambda kt: (0, kt))],
        out_specs=(u_spec, h_spec, h_spec),
        compiler_params=_cparams("parallel"),
        name="s5_step",
    )(u, h0_re, h0_im, wb_re, wb_im, wc_re, wc_im, ab_re, ab_im, d_skip)


def _glu_mix_kernel(z_ref, attn_ref, ga_ref, gs_ref, wv_ref, wg_ref, o_ref, wv16_ref=None, wg16_ref=None):
    z = z_ref[...]
    val = jnp.dot(z, _weight_tile(wv_ref, wv16_ref), preferred_element_type=F32)
    gat = jnp.dot(z, _weight_tile(wg_ref, wg16_ref), preferred_element_type=F32)
    mix = ga_ref[...] * attn_ref[...] + gs_ref[...] * (val * _sigmoid(gat))
    o_ref[...] = mix.astype(o_ref.dtype)


def _glu_mix(z, attn, gates, w_glu_v, w_glu_g, tm, tn):
    m, d = z.shape
    gs_off = d // tn
    col = pl.BlockSpec((tm, tn), lambda i, j: (i, j))
    w_spec = pl.BlockSpec((d, tn), lambda i, j: (0, j))
    out_shape, out_specs = [jax.ShapeDtypeStruct((m, d), BF16)], [col]
    if _cast_mode(w_glu_v, m, tm):
        out_shape += [jax.ShapeDtypeStruct(w_glu_v.shape, BF16)] * 2
        out_specs += [w_spec, w_spec]
    return pl.pallas_call(
        _glu_mix_kernel,
        out_shape=tuple(out_shape),
        grid=(m // tm, d // tn),
        in_specs=[pl.BlockSpec((tm, d), lambda i, j: (i, 0)), col, col,
                  pl.BlockSpec((tm, tn), lambda i, j: (i, gs_off + j)), w_spec, w_spec],
        out_specs=tuple(out_specs),
        compiler_params=_cparams("parallel", "arbitrary"),
        name="glu_mix",
    )(z, attn, gates, gates, w_glu_v, w_glu_g)


def _out_proj_kernel(mix_ref, x_ref, w_ref, o_ref, w16_ref=None):
    o_ref[...] = x_ref[...] + jnp.dot(mix_ref[...], _weight_tile(w_ref, w16_ref),
                                      preferred_element_type=F32)


def _out_proj(mix, x, w_out, tm, tn):
    m, d = x.shape
    col = pl.BlockSpec((tm, tn), lambda i, j: (i, j))
    w_spec = pl.BlockSpec((d, tn), lambda i, j: (0, j))
    out_shape, out_specs = [jax.ShapeDtypeStruct((m, d), F32)], [col]
    if _cast_mode(w_out, m, tm):
        out_shape.append(jax.ShapeDtypeStruct(w_out.shape, BF16))
        out_specs.append(w_spec)
    return pl.pallas_call(
        _out_proj_kernel,
        out_shape=tuple(out_shape),
        grid=(m // tm, d // tn),
        in_specs=[pl.BlockSpec((tm, d), lambda i, j: (i, 0)), col, w_spec],
        out_specs=tuple(out_specs),
        compiler_params=_cparams("parallel", "arbitrary"),
        name="out_proj",
    )(mix, x, w_out)


def _ffn_kernel(x_ref, g_ref, gf_ref, wu_ref, wd_ref, o_ref, *rest):
    wu16_ref, wd16_ref, hn_ref = rest if len(rest) == 3 else (None, None, rest[0])
    f = pl.program_id(1)

    @pl.when(f == 0)
    def _():
        x = x_ref[...]
        hn_ref[...] = (_rms_scale(x) * g_ref[...]).astype(BF16)
        o_ref[...] = x

    up = jnp.dot(hn_ref[...], _weight_tile(wu_ref, wu16_ref), preferred_element_type=F32)
    act = jnp.square(jnp.maximum(up, 0.0))
    o_ref[...] += jnp.dot(act.astype(BF16), _weight_tile(wd_ref, wd16_ref),
                          preferred_element_type=F32)

    @pl.when(f == pl.num_programs(1) - 1)
    def _():
        o_ref[...] = _rms_scale(o_ref[...]) * gf_ref[...]


def _ffn(x, norm_ffn, norm_final, w_up, w_down, tm, tf):
    m, d = x.shape
    d_ff = w_up.shape[1]
    row = pl.BlockSpec((tm, d), lambda i, f: (i, 0))
    vec = pl.BlockSpec((1, d), lambda i, f: (0, 0))
    up_spec = pl.BlockSpec((d, tf), lambda i, f: (0, f))
    down_spec = pl.BlockSpec((tf, d), lambda i, f: (f, 0))
    out_shape, out_specs = [jax.ShapeDtypeStruct((m, d), F32)], [row]
    if _cast_mode(w_up, m, tm):
        out_shape += [jax.ShapeDtypeStruct(w_up.shape, BF16), jax.ShapeDtypeStruct(w_down.shape, BF16)]
        out_specs += [up_spec, down_spec]
    return pl.pallas_call(
        _ffn_kernel,
        out_shape=tuple(out_shape),
        grid=(m // tm, d_ff // tf),
        in_specs=[row, vec, vec, up_spec, down_spec],
        out_specs=tuple(out_specs),
        scratch_shapes=[pltpu.VMEM((tm, d), BF16)],
        compiler_params=_cparams("parallel", "arbitrary"),
        name="ffn",
    )(x, norm_ffn, norm_final, w_up, w_down)


def kernel(x_prompt, x_sample, cache_k, cache_v, state_ssm_re, state_ssm_im, page_table, norm_mix, w_in, lambda_re, lambda_im, log_step, b_re, b_im, c_re, c_im, d_skip, w_glu_v, w_glu_g, w_out, norm_ffn, w_up, w_down, norm_final):
    depth = w_in.shape[0]
    assert depth == 1, "single trunk layer"
    n_b, seq, d = x_prompt.shape
    n_dec, s_dec, _ = x_sample.shape
    assert s_dec == 1
    width = N_HEADS * HEAD_DIM
    n_pool, page = cache_k.shape[1], cache_k.shape[2]
    past_len = page_table.shape[1] * page
    n_groups = lambda_re.shape[1]

    cos_t, sin_a, sin_b = _rope_tables(max(seq, past_len + s_dec))
    ab_re, ab_im, bb_re, bb_im = _ssm_prep(lambda_re[0], lambda_im[0], log_step[0], b_re[0], b_im[0])
    to_in = lambda bb: _block_diag_tiles(
        bb.reshape(n_groups, SSM_STATE, SSM_GROUP).transpose(0, 2, 1))
    wb_re, wb_im = to_in(bb_re), to_in(bb_im)
    to_out = lambda cc: _block_diag_tiles(cc.transpose(0, 2, 1))
    wc_re, wc_im = to_out(c_re[0]), to_out(c_im[0])
    n_t = d // SSM_CH_TILE
    ab_re_t = ab_re.reshape(n_t, SSM_STATE_ROWS, LANES)
    ab_im_t = ab_im.reshape(n_t, SSM_STATE_ROWS, LANES)

    g_mix, g_ffn, g_fin = norm_mix.reshape(1, d), norm_ffn.reshape(1, d), norm_final.reshape(1, d)
    d_row = d_skip.reshape(1, d)
    drop_depth = lambda w: w.reshape(w.shape[1:])
    in_widths = (width, width, width, d, 2 * d)

    xs = x_sample.reshape(n_dec, d)
    rope_s = tuple(jnp.broadcast_to(t[past_len:past_len + 1], (n_dec, LANES))
                   for t in (cos_t, sin_a, sin_b))
    qs, ks, vs, us, gs, w_in16 = _in_proj(xs, g_mix, drop_depth(w_in), rope_s, 1, in_widths,
                                          tm=n_dec, tn=512)
    heads = (n_dec, N_HEADS, HEAD_DIM)
    cache_shape = (n_pool, page, N_HEADS, HEAD_DIM)
    attn_s = _moba_decode(qs.reshape(heads), ks.reshape(heads), vs.reshape(heads),
                          cache_k.reshape(cache_shape), cache_v.reshape(cache_shape), page_table,
                          pages_per_step=8)
    n_state = n_groups * SSM_STATE
    zs, hsr, hsi = _s5_step(us, state_ssm_re.reshape(n_dec, n_state),
                            state_ssm_im.reshape(n_dec, n_state),
                            wb_re, wb_im, wc_re, wc_im,
                            ab_re.reshape(1, n_state), ab_im.reshape(1, n_state), d_row)
    mix_s, w_v16, w_g16 = _glu_mix(zs, attn_s.reshape(n_dec, width), gs, drop_depth(w_glu_v),
                                   drop_depth(w_glu_g), tm=n_dec, tn=512)
    x1_s, w_o16 = _out_proj(mix_s, xs, drop_depth(w_out), tm=n_dec, tn=512)
    y_sample, w_up16, w_dn16 = _ffn(x1_s, g_ffn, g_fin, drop_depth(w_up), drop_depth(w_down),
                                    tm=n_dec, tf=512)

    tm_p = 1024
    xp = x_prompt.reshape(n_b * seq, d)
    qp, kp, vp, up, gp = _in_proj(xp, g_mix, w_in16, (cos_t, sin_a, sin_b), seq // tm_p, in_widths,
                                  tm=tm_p, tn=512)
    attn_p = _moba_prompt(qp.reshape(n_b, seq, width), kp.reshape(n_b, seq, width),
                          vp.reshape(n_b, seq, width))
    zp, hpr, hpi = _s5_prompt(up.reshape(n_b, seq, d), wb_re, wb_im, wc_re, wc_im,
                              ab_re_t, ab_im_t, d_row, t_c=256, tiles_per_step=2)
    (mix_p,) = _glu_mix(zp.reshape(n_b * seq, d), attn_p.reshape(n_b * seq, width), gp, w_v16, w_g16,
                        tm=tm_p, tn=512)
    (x1_p,) = _out_proj(mix_p, xp, w_o16, tm=tm_p, tn=1024)
    (y_prompt,) = _ffn(x1_p, g_ffn, g_fin, w_up16, w_dn16, tm=512, tf=1024)

    kv_p = (1, n_b, seq, N_HEADS, HEAD_DIM)
    kv_s = (1, n_dec, s_dec, N_HEADS, HEAD_DIM)
    st_p = (1, n_b, n_groups, SSM_STATE)
    st_s = (1, n_dec, n_groups, SSM_STATE)
    return (y_prompt.reshape(n_b, seq, d), y_sample.reshape(n_dec, s_dec, d),
            kp.reshape(kv_p), vp.reshape(kv_p), hpr.reshape(st_p), hpi.reshape(st_p),
            ks.reshape(kv_s), vs.reshape(kv_s), hsr.reshape(st_s), hsi.reshape(st_s))
```

```python
import functools
import math

import jax
import jax.numpy as jnp
from jax import lax
from jax.experimental import pallas as pl
from jax.experimental.pallas import tpu as pltpu

N_HEADS = 16
HEAD_DIM = 128
ROT_DIM = HEAD_DIM // 4
ROT_HALF = ROT_DIM // 2
ROPE_THETA = 500000.0
MOBA_BLOCK = 256
MOBA_TOPK = 3
SSM_GROUP = 16
SSM_STATE = 64
RMS_EPS = 1e-6
NEG = -1e30
MASK_BIAS = -(2.0 ** 100)
MOBA_SCORES_AHEAD = 1

LANES = 128
SUBLANES = 8
MXU_DIM = 256
VMEM_LIMIT_BYTES = 60 * 1024 * 1024

SSM_CH_TILE = MXU_DIM
SSM_GROUPS_PER_TILE = SSM_CH_TILE // SSM_GROUP
SSM_STATE_TILE = SSM_GROUPS_PER_TILE * SSM_STATE
SSM_STATE_ROWS = SSM_STATE_TILE // LANES

ROPE_ROW_CHUNK = 8 * SUBLANES

F32 = jnp.float32
BF16 = jnp.bfloat16


def _cparams(*sem):
    return pltpu.CompilerParams(dimension_semantics=sem, vmem_limit_bytes=VMEM_LIMIT_BYTES)


def _sigmoid(x):
    return 0.5 * jnp.tanh(0.5 * x) + 0.5


def _gelu_tanh(x):
    c = math.sqrt(2.0 / math.pi)
    return 0.5 * x * (1.0 + jnp.tanh(c * (x + 0.044715 * (x * x * x))))


def _rms_scale(x):
    return x * lax.rsqrt(jnp.mean(x * x, axis=-1, keepdims=True) + RMS_EPS)


def _rope_table_kernel(inv_ref, cos_ref, sin_a_ref, sin_b_ref):
    rows = cos_ref.shape[0]
    pos = lax.broadcasted_iota(jnp.int32, (rows, LANES), 0).astype(F32)
    lane = lax.broadcasted_iota(jnp.int32, (rows, LANES), 1)
    ang = pos * inv_ref[...]
    c = jnp.cos(ang)
    s = jnp.sin(ang)
    cos_ref[...] = c
    sin_a_ref[...] = jnp.where((lane >= ROT_HALF) & (lane < ROT_DIM), s, 0.0)
    sin_b_ref[...] = jnp.where(lane < ROT_HALF, -s, 0.0)


def _rope_tables(n_pos):
    rows = -(-n_pos // SUBLANES) * SUBLANES
    inv = ROPE_THETA ** (-jnp.arange(ROT_HALF, dtype=F32) / ROT_HALF)
    inv_row = jnp.concatenate([inv, inv, jnp.zeros((LANES - ROT_DIM,), F32)])[None, :]
    out = jax.ShapeDtypeStruct((rows, LANES), F32)
    return pl.pallas_call(_rope_table_kernel, out_shape=(out, out, out), name="rope_tables")(inv_row)


def _ssm_prep_kernel(ls_ref, lr_ref, li_ref, lrr_ref, lir_ref, br_ref, bi_ref,
                     abr_ref, abi_ref, bbr_ref, bbi_ref):
    step = jnp.exp(ls_ref[...])

    def disc(lr, li):
        mag = jnp.exp(lr * step)
        ang = li * step
        ab_re, ab_im = mag * jnp.cos(ang), mag * jnp.sin(ang)
        den = lr * lr + li * li
        nr, ni = ab_re - 1.0, ab_im
        f_re = (nr * lr + ni * li) / den
        f_im = (ni * lr - nr * li) / den
        return ab_re, ab_im, f_re, f_im

    ab_re, ab_im, _, _ = disc(lr_ref[...], li_ref[...])
    abr_ref[...] = ab_re
    abi_ref[...] = ab_im
    _, _, f_re, f_im = disc(lrr_ref[...], lir_ref[...])
    br, bi = br_ref[...], bi_ref[...]
    bbr_ref[...] = f_re * br - f_im * bi
    bbi_ref[...] = f_re * bi + f_im * br


def _ssm_prep(lambda_re, lambda_im, log_step, b_re, b_im):
    g, n = lambda_re.shape
    flat = g, n * SSM_GROUP
    outs = (jax.ShapeDtypeStruct((g, n), F32),) * 2 + (jax.ShapeDtypeStruct(flat, F32),) * 2
    return pl.pallas_call(_ssm_prep_kernel, out_shape=outs, name="ssm_prep")(
        log_step.reshape(g, 1), lambda_re, lambda_im,
        jnp.repeat(lambda_re, SSM_GROUP, axis=1), jnp.repeat(lambda_im, SSM_GROUP, axis=1),
        b_re.reshape(flat), b_im.reshape(flat))


def _block_diag_tiles(w):
    g, r, c = w.shape
    t, n = g // SSM_GROUPS_PER_TILE, SSM_GROUPS_PER_TILE
    tiled = jnp.tile(w.reshape(t, n * r, c), (1, 1, n))
    row_grp = jnp.arange(n * r, dtype=jnp.int32)[:, None] // r
    col_grp = jnp.arange(n * c, dtype=jnp.int32)[None, :] // c
    return jnp.where(row_grp == col_grp, tiled, 0.0).astype(BF16)


def _weight_tile(w_ref, w16_ref):
    if w16_ref is None:
        return w_ref[...]
    w16_ref[...] = w_ref[...].astype(BF16)
    return w16_ref[...]


def _cast_mode(w, m, tm):
    cast = w.dtype == F32
    assert not cast or m == tm
    return cast


def _in_proj_kernel(x_ref, g_ref, w_ref, cos_ref, sa_ref, sb_ref,
                    q_ref, k_ref, v_ref, u_ref, gt_ref, *rest, ends):
    w16_ref, hn_ref = (rest[0], rest[1]) if len(rest) == 2 else (None, rest[0])
    j = pl.program_id(1)
    q_end, k_end, v_end, u_end = ends

    @pl.when(j == 0)
    def _():
        hn_ref[...] = (_rms_scale(x_ref[...]) * g_ref[...]).astype(BF16)

    def proj():
        return jnp.dot(hn_ref[...], _weight_tile(w_ref, w16_ref), preferred_element_type=F32)

    def rope_into(o_ref):
        acc = proj()
        chunk = min(ROPE_ROW_CHUNK, acc.shape[0])
        lane = lax.broadcasted_iota(jnp.int32, (chunk, HEAD_DIM), 1)
        partner = jnp.where(lane < ROT_HALF, lane + ROT_HALF,
                            jnp.where(lane < ROT_DIM, lane - ROT_HALF, lane))
        for r0 in range(0, acc.shape[0], chunk):
            rows = slice(r0, r0 + chunk)
            cos, sin = cos_ref[rows, :], sa_ref[rows, :] + sb_ref[rows, :]
            for h in range(acc.shape[1] // HEAD_DIM):
                cols = slice(h * HEAD_DIM, (h + 1) * HEAD_DIM)
                xh = acc[rows, cols]
                o_ref[rows, cols] = xh * cos + jnp.take_along_axis(xh, partner, axis=1) * sin

    @pl.when(j < q_end)
    def _():
        rope_into(q_ref)

    @pl.when((j >= q_end) & (j < k_end))
    def _():
        rope_into(k_ref)

    @pl.when((j >= k_end) & (j < v_end))
    def _():
        v_ref[...] = proj()

    @pl.when((j >= v_end) & (j < u_end))
    def _():
        u_ref[...] = proj()

    @pl.when(j >= u_end)
    def _():
        gt_ref[...] = _sigmoid(proj())


def _in_proj(x, gain, w, rope, rope_blocks, widths, tm, tn):
    m, d = x.shape
    assert all(wd % tn == 0 for wd in widths)
    tiles = [wd // tn for wd in widths]
    starts = [sum(tiles[:n]) for n in range(len(tiles))]

    def out_spec(n):
        return pl.BlockSpec((tm, tn), lambda i, j: (i, jnp.clip(j - starts[n], 0, tiles[n] - 1)))

    table = pl.BlockSpec((tm, LANES), lambda i, j: (i % rope_blocks, 0))
    w_spec = pl.BlockSpec((d, tn), lambda i, j: (0, j))
    out_shape = [jax.ShapeDtypeStruct((m, wd), F32) for wd in widths]
    out_specs = [out_spec(n) for n in range(len(widths))]
    if _cast_mode(w, m, tm):
        out_shape.append(jax.ShapeDtypeStruct(w.shape, BF16))
        out_specs.append(w_spec)
    return pl.pallas_call(
        functools.partial(_in_proj_kernel, ends=tuple(starts[1:])),
        out_shape=tuple(out_shape),
        grid=(m // tm, sum(tiles)),
        in_specs=[pl.BlockSpec((tm, d), lambda i, j: (i, 0)),
                  pl.BlockSpec((1, d), lambda i, j: (0, 0)),
                  w_spec, table, table, table],
        out_specs=tuple(out_specs),
        scratch_shapes=[pltpu.VMEM((tm, d), BF16)],
        compiler_params=_cparams("parallel", "arbitrary"),
        name="in_proj",
    )(x, gain, w, *rope)


def _topk_selected(gates, i):
    return _topk_rank(gates, i) < float(MOBA_TOPK)


def _topk_rank(gates, i):
    gi = gates[i]
    rank = jnp.zeros(gi.shape, F32)
    for i2, g2 in enumerate(gates):
        if i2 == i:
            continue
        ahead = (g2 >= gi) if i2 < i else (g2 > gi)
        rank = rank + jnp.where(ahead, 1.0, 0.0)
    return rank


def _moba_prompt_kernel(q_ref, k_ref, v_ref, kbias_ref, o_ref):
    seq = q_ref.shape[1]
    nb = seq // MOBA_BLOCK
    heads = q_ref.shape[2] // HEAD_DIM
    row = lax.broadcasted_iota(jnp.int32, (MOBA_BLOCK, MOBA_BLOCK), 0)
    col = lax.broadcasted_iota(jnp.int32, (MOBA_BLOCK, MOBA_BLOCK), 1)
    causal = col <= row
    operands = [_moba_head_operands(q_ref, k_ref, v_ref, kbias_ref, h) for h in range(heads)]
    items = [(h, j) for j in range(nb) for h in range(heads)]

    def scores(item):
        h, j = item
        q_aug, k_aug, _ = operands[h]
        return lax.dot_general(q_aug[j * MOBA_BLOCK:(j + 1) * MOBA_BLOCK], k_aug[:(j + 1) * MOBA_BLOCK],
                               (((1,), (1,)), ((), ())), preferred_element_type=F32)

    def finish(item, s):
        h, j = item
        v_aug = operands[h][2]
        s_own = jnp.where(causal, s[:, j * MOBA_BLOCK:], NEG)
        s = jnp.concatenate([s[:, :j * MOBA_BLOCK], s_own], axis=1) if j else s_own
        p = jnp.exp2(s - jnp.max(s, axis=-1, keepdims=True))
        o = jnp.dot(p.astype(BF16), v_aug[:(j + 1) * MOBA_BLOCK], preferred_element_type=F32)
        o_ref[0, j * MOBA_BLOCK:(j + 1) * MOBA_BLOCK, h * HEAD_DIM:(h + 1) * HEAD_DIM] = (
            o[:, :HEAD_DIM] / o[:, HEAD_DIM:])

    ahead = [scores(item) for item in items[:MOBA_SCORES_AHEAD]]
    for n, item in enumerate(items):
        if n + MOBA_SCORES_AHEAD < len(items):
            ahead.append(scores(items[n + MOBA_SCORES_AHEAD]))
        finish(item, ahead.pop(0))


def _moba_head_operands(q_ref, k_ref, v_ref, kbias_ref, h):
    seq = q_ref.shape[1]
    nb = seq // MOBA_BLOCK
    cols = slice(h * HEAD_DIM, (h + 1) * HEAD_DIM)
    q, k, v = q_ref[0, :, cols], k_ref[0, :, cols], v_ref[0, :, cols]
    k_mean = jnp.concatenate(
        [jnp.mean(k[i * MOBA_BLOCK:(i + 1) * MOBA_BLOCK], axis=0, keepdims=True) for i in range(nb)],
        axis=0)
    gate_t = lax.dot_general(k_mean, q, (((1,), (1,)), ((), ())),
                             precision=lax.Precision.HIGHEST, preferred_element_type=F32)
    blk = lax.broadcasted_iota(jnp.int32, (nb, seq), 0)
    own_blk = lax.broadcasted_iota(jnp.int32, (nb, seq), 1) // MOBA_BLOCK
    past = blk < own_blk
    dropped = jnp.zeros((nb, seq), F32)
    for i in range(nb - 1):
        gi = gate_t[i:i + 1, :]
        ahead = (gate_t > gi) | ((gate_t == gi) & (blk < i))
        rank = jnp.sum(jnp.where(ahead & past, 1.0, 0.0), axis=0, keepdims=True)
        dropped = jnp.where((blk == i) & (rank >= float(MOBA_TOPK)), 1.0, dropped)
    dropped = jnp.where(past, dropped, 0.0)
    drop_cols = jnp.concatenate([dropped, jnp.zeros((HEAD_DIM - nb, seq), F32)], axis=0).T

    c = HEAD_DIM ** -0.5 * math.log2(math.e)
    q_aug = jnp.concatenate([(q * c).astype(BF16), drop_cols.astype(BF16)], axis=1)
    k_aug = jnp.concatenate([k.astype(BF16), kbias_ref[...]], axis=1)
    v_aug = jnp.concatenate([v.astype(BF16), jnp.ones((seq, HEAD_DIM), BF16)], axis=1)
    return q_aug, k_aug, v_aug


def _moba_prompt(q, k, v, heads_per_step):
    b, s, _ = q.shape
    nb = s // MOBA_BLOCK
    assert nb <= HEAD_DIM and N_HEADS % heads_per_step == 0
    spec = pl.BlockSpec((1, s, heads_per_step * HEAD_DIM), lambda bi, h: (bi, 0, h))
    key_blk = jnp.arange(s, dtype=jnp.int32)[:, None] // MOBA_BLOCK
    kbias = jnp.where(key_blk == jnp.arange(HEAD_DIM, dtype=jnp.int32)[None, :], MASK_BIAS, 0.0)
    return pl.pallas_call(
        _moba_prompt_kernel,
        out_shape=jax.ShapeDtypeStruct(q.shape, F32),
        grid=(b, N_HEADS // heads_per_step),
        in_specs=[spec, spec, spec, pl.BlockSpec((s, HEAD_DIM), lambda bi, h: (0, 0))],
        out_specs=spec,
        compiler_params=_cparams("parallel", "parallel"),
        name="moba_prompt",
    )(q, k, v, kbias.astype(BF16))


def _value_slab_copy(pt_ref, cv_ref, vbuf, sem, seq, slot, rank, head, blk, p):
    pages_per_blk = vbuf.shape[2]
    pg = pt_ref[seq, blk * pages_per_blk + p]
    return pltpu.make_async_copy(cv_ref.at[pg, :, head, :], vbuf.at[slot, rank, p, :, head, :],
                                 sem.at[slot])


def _moba_decode_kernel(pt_ref, q_ref, kn_ref, vn_ref, *refs, pages_per_step, n_seq):
    k_refs = refs[:pages_per_step]
    cv_ref, o_ref, g_s, m_s, l_s, p_s, rank_s, own_s, vbuf, sem = refs[pages_per_step:]
    b, step = pl.program_id(0), pl.program_id(1)
    last_step = step == pl.num_programs(1) - 1
    slot = lax.rem(b, 2)
    page = k_refs[0].shape[1]
    pages_per_blk = MOBA_BLOCK // page
    blks_per_step = pages_per_step // pages_per_blk
    rows_per_page = page * N_HEADS
    n_cols = pages_per_blk * rows_per_page
    nb = g_s.shape[0]
    scale = HEAD_DIM ** -0.5
    lanes = (N_HEADS, HEAD_DIM)

    @pl.when(b < n_seq)
    def _key_pass():
        q = q_ref[0]
        qb = q.astype(BF16)
        col_head = lax.broadcasted_iota(jnp.int32, (N_HEADS, n_cols), 1) % N_HEADS
        own = col_head == lax.broadcasted_iota(jnp.int32, (N_HEADS, n_cols), 0)
        for bi in range(blks_per_step):
            kp = [k_refs[bi * pages_per_blk + p][0] for p in range(pages_per_blk)]
            k_rows = jnp.concatenate([k.reshape(rows_per_page, HEAD_DIM) for k in kp], axis=0)
            k_sum = kp[0].sum(axis=0)
            for k in kp[1:]:
                k_sum = k_sum + k.sum(axis=0)
            gate = jnp.sum(q * (k_sum * (1.0 / MOBA_BLOCK)), axis=-1, keepdims=True)
            s = lax.dot_general(qb, k_rows.astype(BF16), (((1,), (1,)), ((), ())),
                                preferred_element_type=F32) * scale
            s = jnp.where(own, s, NEG)
            m = jnp.max(s, axis=-1, keepdims=True)
            p = jnp.exp(s - m)
            blk = step * blks_per_step + bi
            g_s[blk] = jnp.broadcast_to(gate, lanes)
            m_s[slot, blk] = jnp.broadcast_to(m, lanes)
            l_s[slot, blk] = jnp.broadcast_to(jnp.sum(p, axis=-1, keepdims=True), lanes)
            p_s[slot, blk] = p

    @pl.when(last_step & (b < n_seq))
    def _select_and_fetch():
        qb = q_ref[0].astype(BF16).astype(F32)
        kn = kn_ref[0].astype(BF16).astype(F32)
        s_own = jnp.sum(qb * kn, axis=-1, keepdims=True) * scale
        own_s[slot, 0] = jnp.broadcast_to(s_own, lanes)
        own_s[slot, 1] = vn_ref[0].astype(BF16).astype(F32)
        gates = [g_s[i] for i in range(nb)]
        ranks = [_topk_rank(gates, i) for i in range(nb)]
        for i in range(nb):
            rank_s[slot, i] = ranks[i]
        for r in range(MOBA_TOPK):
            blk_of_head = jnp.zeros(lanes, F32)
            for i in range(nb):
                blk_of_head = jnp.where(ranks[i] == float(r), float(i), blk_of_head)
            blk_of_head = blk_of_head.astype(jnp.int32)
            for h in range(N_HEADS):
                blk = blk_of_head[h, 0]
                for p in range(pages_per_blk):
                    _value_slab_copy(pt_ref, cv_ref, vbuf, sem, b, slot, r, h, blk, p).start()

    @pl.when(last_step & (b >= 1))
    def _merge_previous():
        prev = 1 - slot
        for r in range(MOBA_TOPK):
            for h in range(N_HEADS):
                for p in range(pages_per_blk):
                    _value_slab_copy(pt_ref, cv_ref, vbuf, sem, 0, prev, r, h, 0, p).wait()
        s_own, v_own = own_s[prev, 0], own_s[prev, 1]
        ranks = [rank_s[prev, i] for i in range(nb)]
        m_blk = [m_s[prev, i] for i in range(nb)]
        m_all = s_own
        for i in range(nb):
            m_all = jnp.maximum(m_all, jnp.where(ranks[i] < float(MOBA_TOPK), m_blk[i], NEG))
        w_own = jnp.exp(s_own - m_all)
        den = w_own
        num = w_own.astype(BF16).astype(F32) * v_own
        for r in range(MOBA_TOPK):
            w_r = jnp.zeros(lanes, F32)
            p_r = jnp.zeros((N_HEADS, n_cols), F32)
            for i in range(nb):
                mine = jnp.where(ranks[i] == float(r), 1.0, 0.0)
                w_i = jnp.where(ranks[i] == float(r), jnp.exp(m_blk[i] - m_all), 0.0)
                w_r = w_r + w_i
                den = den + w_i * l_s[prev, i]
                p_r = p_r + jnp.broadcast_to(mine[:, :1], p_r.shape) * p_s[prev, i]
            v_rows = vbuf[prev, r].reshape(n_cols, HEAD_DIM)
            o_r = jnp.dot(p_r.astype(BF16), v_rows.astype(BF16), preferred_element_type=F32)
            num = num + w_r * o_r
        o_ref[0] = num / den


def _moba_decode(q, k_new, v_new, cache_k, cache_v, page_table, pages_per_step):
    n_dec = q.shape[0]
    page = cache_k.shape[1]
    n_pages = page_table.shape[1]
    pages_per_blk = MOBA_BLOCK // page
    assert MOBA_BLOCK % page == 0 and (n_pages * page) % MOBA_BLOCK == 0
    assert pages_per_step % pages_per_blk == 0 and n_pages % pages_per_step == 0
    nb = n_pages // pages_per_blk
    assert nb >= MOBA_TOPK
    last = n_dec - 1
    row = pl.BlockSpec((1, N_HEADS, HEAD_DIM), lambda b, i, pt: (jnp.minimum(b, last), 0, 0))

    def page_spec(p):
        return pl.BlockSpec((1, page, N_HEADS, HEAD_DIM),
                            lambda b, i, pt: (pt[jnp.minimum(b, last), i * pages_per_step + p], 0, 0, 0))

    n_cols = MOBA_BLOCK * N_HEADS
    stat = (N_HEADS, HEAD_DIM)
    return pl.pallas_call(
        functools.partial(_moba_decode_kernel, pages_per_step=pages_per_step, n_seq=n_dec),
        out_shape=jax.ShapeDtypeStruct((n_dec, N_HEADS, HEAD_DIM), F32),
        grid_spec=pltpu.PrefetchScalarGridSpec(
            num_scalar_prefetch=1,
            grid=(n_dec + 1, n_pages // pages_per_step),
            in_specs=[row, row, row] + [page_spec(p) for p in range(pages_per_step)]
            + [pl.BlockSpec(memory_space=pl.ANY)],
            out_specs=pl.BlockSpec((1, N_HEADS, HEAD_DIM), lambda b, i, pt: (jnp.maximum(b - 1, 0), 0, 0)),
            scratch_shapes=[
                pltpu.VMEM((nb,) + stat, F32),
                pltpu.VMEM((2, nb) + stat, F32),
                pltpu.VMEM((2, nb) + stat, F32),
                pltpu.VMEM((2, nb, N_HEADS, n_cols), F32),
                pltpu.VMEM((2, nb) + stat, F32),
                pltpu.VMEM((2, 2) + stat, F32),
                pltpu.VMEM((2, MOBA_TOPK, pages_per_blk, page) + stat, F32),
                pltpu.SemaphoreType.DMA((2,))]),
        compiler_params=_cparams("arbitrary", "arbitrary"),
        name="moba_decode",
    )(page_table, q, k_new, v_new, *([cache_k] * pages_per_step), cache_v)


def _s5_prompt_kernel(u_ref, wbr_ref, wbi_ref, wcr_ref, wci_ref, ar_ref, ai_ref, d_ref,
                      z_ref, hr_out, hi_out, xr_s, xi_s, hr_s, hi_s):
    c = pl.program_id(1)
    n_b, t_c = u_ref.shape[0], u_ref.shape[1]
    n_k = wbr_ref.shape[0]

    @pl.when(c == 0)
    def _():
        hr_s[...] = jnp.zeros_like(hr_s)
        hi_s[...] = jnp.zeros_like(hi_s)

    u_all = u_ref[...].reshape(n_b * t_c, u_ref.shape[2])

    def u_tile(k):
        return u_all[:, k * SSM_CH_TILE:(k + 1) * SSM_CH_TILE]

    def project_in(k):
        ub = u_tile(k).astype(BF16)
        for w_ref, x_s in ((wbr_ref, xr_s), (wbi_ref, xi_s)):
            x = jnp.dot(ub, w_ref[k], preferred_element_type=F32)
            for b in range(n_b):
                for j in range(SSM_STATE_ROWS):
                    x_s[k, b, pl.ds(j, t_c, stride=SSM_STATE_ROWS), :] = (
                        x[b * t_c:(b + 1) * t_c, j * LANES:(j + 1) * LANES])

    def scan(k):
        ar, ai = ar_ref[k], ai_ref[k]
        h = [(hr_s[k, b], hi_s[k, b]) for b in range(n_b)]
        for t in range(t_c):
            rows = slice(t * SSM_STATE_ROWS, (t + 1) * SSM_STATE_ROWS)
            for b in range(n_b):
                hr, hi = h[b]
                nhr = ar * hr - ai * hi + xr_s[k, b, rows, :]
                nhi = ar * hi + ai * hr + xi_s[k, b, rows, :]
                xr_s[k, b, rows, :] = nhr
                xi_s[k, b, rows, :] = nhi
                h[b] = (nhr, nhi)
        for b in range(n_b):
            hr_s[k, b], hi_s[k, b] = h[b]

    def project_out(k):
        def gather_states(s_ref):
            return jnp.concatenate(
                [jnp.concatenate([s_ref[k, b, pl.ds(j, t_c, stride=SSM_STATE_ROWS), :].astype(BF16)
                                  for j in range(SSM_STATE_ROWS)], axis=1) for b in range(n_b)], axis=0)

        y = (jnp.dot(gather_states(xr_s), wcr_ref[k], preferred_element_type=F32)
             - jnp.dot(gather_states(xi_s), wci_ref[k], preferred_element_type=F32)
             + d_ref[:, k * SSM_CH_TILE:(k + 1) * SSM_CH_TILE] * u_tile(k))
        z = _gelu_tanh(y).astype(z_ref.dtype).reshape(n_b, t_c, SSM_CH_TILE)
        z_ref[:, :, k * SSM_CH_TILE:(k + 1) * SSM_CH_TILE] = z

    project_in(0)
    for k in range(n_k):
        if k + 1 < n_k:
            project_in(k + 1)
        scan(k)
        if k > 0:
            project_out(k - 1)
    project_out(n_k - 1)

    @pl.when(c == pl.num_programs(1) - 1)
    def _():
        for k in range(n_k):
            for b in range(n_b):
                hr_out[b, k] = hr_s[k, b]
                hi_out[b, k] = hi_s[k, b]


def _s5_prompt(u, wb_re, wb_im, wc_re, wc_im, ab_re, ab_im, d_skip, t_c, tiles_per_step):
    n_b, seq, width = u.shape
    n_t = width // SSM_CH_TILE
    n_k = tiles_per_step
    assert n_t % n_k == 0
    state = jax.ShapeDtypeStruct((n_b, n_t, SSM_STATE_ROWS, LANES), F32)
    wb_spec = pl.BlockSpec((n_k, SSM_CH_TILE, SSM_STATE_TILE), lambda kt, c: (kt, 0, 0))
    wc_spec = pl.BlockSpec((n_k, SSM_STATE_TILE, SSM_CH_TILE), lambda kt, c: (kt, 0, 0))
    a_spec = pl.BlockSpec((n_k, SSM_STATE_ROWS, LANES), lambda kt, c: (kt, 0, 0))
    u_spec = pl.BlockSpec((n_b, t_c, n_k * SSM_CH_TILE), lambda kt, c: (0, c, kt))
    h_spec = pl.BlockSpec((n_b, n_k, SSM_STATE_ROWS, LANES), lambda kt, c: (0, kt, 0, 0))
    x_scr = pltpu.VMEM((n_k, n_b, t_c * SSM_STATE_ROWS, LANES), F32)
    h_scr = pltpu.VMEM((n_k, n_b, SSM_STATE_ROWS, LANES), F32)
    return pl.pallas_call(
        _s5_prompt_kernel,
        out_shape=(jax.ShapeDtypeStruct(u.shape, BF16), state, state),
        grid=(n_t // n_k, seq // t_c),
        in_specs=[u_spec, wb_spec, wb_spec, wc_spec, wc_spec, a_spec, a_spec,
                  pl.BlockSpec((1, n_k * SSM_CH_TILE), lambda kt, c: (0, kt))],
        out_specs=(u_spec, h_spec, h_spec),
        scratch_shapes=[x_scr, x_scr, h_scr, h_scr],
        compiler_params=_cparams("parallel", "arbitrary"),
        name="s5_prompt",
    )(u, wb_re, wb_im, wc_re, wc_im, ab_re, ab_im, d_skip)


def _s5_step_kernel(u_ref, h0r_ref, h0i_ref, wbr_ref, wbi_ref, wcr_ref, wci_ref, ar_ref, ai_ref,
                    d_ref, z_ref, hr_out, hi_out):
    u = u_ref[...]
    ub = u.astype(BF16)
    ar, ai = ar_ref[...], ai_ref[...]
    h0r, h0i = h0r_ref[...], h0i_ref[...]
    hr = jnp.dot(ub, wbr_ref[0], preferred_element_type=F32) + (ar * h0r - ai * h0i)
    hi = jnp.dot(ub, wbi_ref[0], preferred_element_type=F32) + (ar * h0i + ai * h0r)
    hr_out[...] = hr
    hi_out[...] = hi
    y = (jnp.dot(hr.astype(BF16), wcr_ref[0], preferred_element_type=F32)
         - jnp.dot(hi.astype(BF16), wci_ref[0], preferred_element_type=F32)
         + d_ref[...] * u)
    z_ref[...] = _gelu_tanh(y).astype(z_ref.dtype)


def _s5_step(u, h0_re, h0_im, wb_re, wb_im, wc_re, wc_im, ab_re, ab_im, d_skip):
    n_seq, width = u.shape
    n_t = width // SSM_CH_TILE
    n_state = h0_re.shape[1]
    state = jax.ShapeDtypeStruct((n_seq, n_state), F32)
    u_spec = pl.BlockSpec((n_seq, SSM_CH_TILE), lambda kt: (0, kt))
    h_spec = pl.BlockSpec((n_seq, SSM_STATE_TILE), lambda kt: (0, kt))
    wb_spec = pl.BlockSpec((1, SSM_CH_TILE, SSM_STATE_TILE), lambda kt: (kt, 0, 0))
    wc_spec = pl.BlockSpec((1, SSM_STATE_TILE, SSM_CH_TILE), lambda kt: (kt, 0, 0))
    a_spec = pl.BlockSpec((1, SSM_STATE_TILE), lambda kt: (0, kt))
    return pl.pallas_call(
        _s5_step_kernel,
        out_shape=(jax.ShapeDtypeStruct(u.shape, BF16), state, state),
        grid=(n_t,),
        in_specs=[u_spec, h_spec, h_spec, wb_spec, wb_spec, wc_spec, wc_spec, a_spec, a_spec,
                  pl.BlockSpec((1, SSM_CH_TILE), lambda kt: (0, kt))],
        out_specs=(u_spec, h_spec, h_spec),
        compiler_params=_cparams("parallel"),
        name="s5_step",
    )(u, h0_re, h0_im, wb_re, wb_im, wc_re, wc_im, ab_re, ab_im, d_skip)


def _glu_mix_kernel(z_ref, attn_ref, ga_ref, gs_ref, wv_ref, wg_ref, o_ref, wv16_ref=None, wg16_ref=None):
    z = z_ref[...]
    val = jnp.dot(z, _weight_tile(wv_ref, wv16_ref), preferred_element_type=F32)
    gat = jnp.dot(z, _weight_tile(wg_ref, wg16_ref), preferred_element_type=F32)
    mix = ga_ref[...] * attn_ref[...] + gs_ref[...] * (val * _sigmoid(gat))
    o_ref[...] = mix.astype(o_ref.dtype)


def _glu_mix(z, attn, gates, w_glu_v, w_glu_g, tm, tn):
    m, d = z.shape
    gs_off = d // tn
    col = pl.BlockSpec((tm, tn), lambda i, j: (i, j))
    w_spec = pl.BlockSpec((d, tn), lambda i, j: (0, j))
    out_shape, out_specs = [jax.ShapeDtypeStruct((m, d), BF16)], [col]
    if _cast_mode(w_glu_v, m, tm):
        out_shape += [jax.ShapeDtypeStruct(w_glu_v.shape, BF16)] * 2
        out_specs += [w_spec, w_spec]
    return pl.pallas_call(
        _glu_mix_kernel,
        out_shape=tuple(out_shape),
        grid=(m // tm, d // tn),
        in_specs=[pl.BlockSpec((tm, d), lambda i, j: (i, 0)), col, col,
                  pl.BlockSpec((tm, tn), lambda i, j: (i, gs_off + j)), w_spec, w_spec],
        out_specs=tuple(out_specs),
        compiler_params=_cparams("parallel", "arbitrary"),
        name="glu_mix",
    )(z, attn, gates, gates, w_glu_v, w_glu_g)


def _out_proj_kernel(mix_ref, x_ref, w_ref, o_ref, w16_ref=None):
    o_ref[...] = x_ref[...] + jnp.dot(mix_ref[...], _weight_tile(w_ref, w16_ref),
                                      preferred_element_type=F32)


def _out_proj(mix, x, w_out, tm, tn):
    m, d = x.shape
    col = pl.BlockSpec((tm, tn), lambda i, j: (i, j))
    w_spec = pl.BlockSpec((d, tn), lambda i, j: (0, j))
    out_shape, out_specs = [jax.ShapeDtypeStruct((m, d), F32)], [col]
    if _cast_mode(w_out, m, tm):
        out_shape.append(jax.ShapeDtypeStruct(w_out.shape, BF16))
        out_specs.append(w_spec)
    return pl.pallas_call(
        _out_proj_kernel,
        out_shape=tuple(out_shape),
        grid=(m // tm, d // tn),
        in_specs=[pl.BlockSpec((tm, d), lambda i, j: (i, 0)), col, w_spec],
        out_specs=tuple(out_specs),
        compiler_params=_cparams("parallel", "arbitrary"),
        name="out_proj",
    )(mix, x, w_out)


def _ffn_kernel(x_ref, g_ref, gf_ref, wu_ref, wd_ref, o_ref, *rest):
    wu16_ref, wd16_ref, hn_ref = rest if len(rest) == 3 else (None, None, rest[0])
    f = pl.program_id(1)

    @pl.when(f == 0)
    def _():
        x = x_ref[...]
        hn_ref[...] = (_rms_scale(x) * g_ref[...]).astype(BF16)
        o_ref[...] = x

    up = jnp.dot(hn_ref[...], _weight_tile(wu_ref, wu16_ref), preferred_element_type=F32)
    act = jnp.square(jnp.maximum(up, 0.0))
    o_ref[...] += jnp.dot(act.astype(BF16), _weight_tile(wd_ref, wd16_ref),
                          preferred_element_type=F32)

    @pl.when(f == pl.num_programs(1) - 1)
    def _():
        o_ref[...] = _rms_scale(o_ref[...]) * gf_ref[...]


def _ffn(x, norm_ffn, norm_final, w_up, w_down, tm, tf):
    m, d = x.shape
    d_ff = w_up.shape[1]
    row = pl.BlockSpec((tm, d), lambda i, f: (i, 0))
    vec = pl.BlockSpec((1, d), lambda i, f: (0, 0))
    up_spec = pl.BlockSpec((d, tf), lambda i, f: (0, f))
    down_spec = pl.BlockSpec((tf, d), lambda i, f: (f, 0))
    out_shape, out_specs = [jax.ShapeDtypeStruct((m, d), F32)], [row]
    if _cast_mode(w_up, m, tm):
        out_shape += [jax.ShapeDtypeStruct(w_up.shape, BF16), jax.ShapeDtypeStruct(w_down.shape, BF16)]
        out_specs += [up_spec, down_spec]
    return pl.pallas_call(
        _ffn_kernel,
        out_shape=tuple(out_shape),
        grid=(m // tm, d_ff // tf),
        in_specs=[row, vec, vec, up_spec, down_spec],
        out_specs=tuple(out_specs),
        scratch_shapes=[pltpu.VMEM((tm, d), BF16)],
        compiler_params=_cparams("parallel", "arbitrary"),
        name="ffn",
    )(x, norm_ffn, norm_final, w_up, w_down)


def kernel(x_prompt, x_sample, cache_k, cache_v, state_ssm_re, state_ssm_im, page_table, norm_mix, w_in, lambda_re, lambda_im, log_step, b_re, b_im, c_re, c_im, d_skip, w_glu_v, w_glu_g, w_out, norm_ffn, w_up, w_down, norm_final):
    depth = w_in.shape[0]
    assert depth == 1, "single trunk layer"
    n_b, seq, d = x_prompt.shape
    n_dec, s_dec, _ = x_sample.shape
    assert s_dec == 1
    width = N_HEADS * HEAD_DIM
    n_pool, page = cache_k.shape[1], cache_k.shape[2]
    past_len = page_table.shape[1] * page
    n_groups = lambda_re.shape[1]

    cos_t, sin_a, sin_b = _rope_tables(max(seq, past_len + s_dec))
    ab_re, ab_im, bb_re, bb_im = _ssm_prep(lambda_re[0], lambda_im[0], log_step[0], b_re[0], b_im[0])
    to_in = lambda bb: _block_diag_tiles(
        bb.reshape(n_groups, SSM_STATE, SSM_GROUP).transpose(0, 2, 1))
    wb_re, wb_im = to_in(bb_re), to_in(bb_im)
    to_out = lambda cc: _block_diag_tiles(cc.transpose(0, 2, 1))
    wc_re, wc_im = to_out(c_re[0]), to_out(c_im[0])
    n_t = d // SSM_CH_TILE
    ab_re_t = ab_re.reshape(n_t, SSM_STATE_ROWS, LANES)
    ab_im_t = ab_im.reshape(n_t, SSM_STATE_ROWS, LANES)

    g_mix, g_ffn, g_fin = norm_mix.reshape(1, d), norm_ffn.reshape(1, d), norm_final.reshape(1, d)
    d_row = d_skip.reshape(1, d)
    drop_depth = lambda w: w.reshape(w.shape[1:])
    in_widths = (width, width, width, d, 2 * d)

    xs = x_sample.reshape(n_dec, d)
    rope_s = tuple(jnp.broadcast_to(t[past_len:past_len + 1], (n_dec, LANES))
                   for t in (cos_t, sin_a, sin_b))
    qs, ks, vs, us, gs, w_in16 = _in_proj(xs, g_mix, drop_depth(w_in), rope_s, 1, in_widths,
                                          tm=n_dec, tn=512)
    heads = (n_dec, N_HEADS, HEAD_DIM)
    cache_shape = (n_pool, page, N_HEADS, HEAD_DIM)
    attn_s = _moba_decode(qs.reshape(heads), ks.reshape(heads), vs.reshape(heads),
                          cache_k.reshape(cache_shape), cache_v.reshape(cache_shape), page_table,
                          pages_per_step=16)
    n_state = n_groups * SSM_STATE
    zs, hsr, hsi = _s5_step(us, state_ssm_re.reshape(n_dec, n_state),
                            state_ssm_im.reshape(n_dec, n_state),
                            wb_re, wb_im, wc_re, wc_im,
                            ab_re.reshape(1, n_state), ab_im.reshape(1, n_state), d_row)
    mix_s, w_v16, w_g16 = _glu_mix(zs, attn_s.reshape(n_dec, width), gs, drop_depth(w_glu_v),
                                   drop_depth(w_glu_g), tm=n_dec, tn=512)
    x1_s, w_o16 = _out_proj(mix_s, xs, drop_depth(w_out), tm=n_dec, tn=512)
    y_sample, w_up16, w_dn16 = _ffn(x1_s, g_ffn, g_fin, drop_depth(w_up), drop_depth(w_down),
                                    tm=n_dec, tf=512)

    tm_p = 1024
    xp = x_prompt.reshape(n_b * seq, d)
    qp, kp, vp, up, gp = _in_proj(xp, g_mix, w_in16, (cos_t, sin_a, sin_b), seq // tm_p, in_widths,
                                  tm=tm_p, tn=512)
    attn_p = _moba_prompt(qp.reshape(n_b, seq, width), kp.reshape(n_b, seq, width),
                          vp.reshape(n_b, seq, width), heads_per_step=4)
    zp, hpr, hpi = _s5_prompt(up.reshape(n_b, seq, d), wb_re, wb_im, wc_re, wc_im,
                              ab_re_t, ab_im_t, d_row, t_c=256, tiles_per_step=2)
    (mix_p,) = _glu_mix(zp.reshape(n_b * seq, d), attn_p.reshape(n_b * seq, width), gp, w_v16, w_g16,
                        tm=tm_p, tn=512)
    (x1_p,) = _out_proj(mix_p, xp, w_o16, tm=tm_p, tn=1024)
    (y_prompt,) = _ffn(x1_p, g_ffn, g_fin, w_up16, w_dn16, tm=512, tf=1024)

    kv_p = (1, n_b, seq, N_HEADS, HEAD_DIM)
    kv_s = (1, n_dec, s_dec, N_HEADS, HEAD_DIM)
    st_p = (1, n_b, n_groups, SSM_STATE)
    st_s = (1, n_dec, n_groups, SSM_STATE)
    return (y_prompt.reshape(n_b, seq, d), y_sample.reshape(n_dec, s_dec, d),
            kp.reshape(kv_p), vp.reshape(kv_p), hpr.reshape(st_p), hpi.reshape(st_p),
            ks.reshape(kv_s), vs.reshape(kv_s), hsr.reshape(st_s), hsi.reshape(st_s))
```

```python
import functools
import math

import jax
import jax.numpy as jnp
from jax import lax
from jax.experimental import pallas as pl
from jax.experimental.pallas import tpu as pltpu

N_HEADS = 16
HEAD_DIM = 128
ROT_DIM = HEAD_DIM // 4
ROT_HALF = ROT_DIM // 2
ROPE_THETA = 500000.0
MOBA_BLOCK = 256
MOBA_TOPK = 3
SSM_GROUP = 16
SSM_STATE = 64
RMS_EPS = 1e-6
NEG = -1e30
MASK_BIAS = -(2.0 ** 100)
MOBA_SCORES_AHEAD = 1

LANES = 128
SUBLANES = 8
MXU_DIM = 256
VMEM_LIMIT_BYTES = 60 * 1024 * 1024

SSM_CH_TILE = MXU_DIM
SSM_GROUPS_PER_TILE = SSM_CH_TILE // SSM_GROUP
SSM_STATE_TILE = SSM_GROUPS_PER_TILE * SSM_STATE
SSM_STATE_ROWS = SSM_STATE_TILE // LANES

ROPE_ROW_CHUNK = 8 * SUBLANES

F32 = jnp.float32
BF16 = jnp.bfloat16


def _cparams(*sem):
    return pltpu.CompilerParams(dimension_semantics=sem, vmem_limit_bytes=VMEM_LIMIT_BYTES)


def _sigmoid(x):
    return 0.5 * jnp.tanh(0.5 * x) + 0.5


def _gelu_tanh(x):
    c = math.sqrt(2.0 / math.pi)
    return 0.5 * x * (1.0 + jnp.tanh(c * (x + 0.044715 * (x * x * x))))


def _rms_scale(x):
    return x * lax.rsqrt(jnp.mean(x * x, axis=-1, keepdims=True) + RMS_EPS)


def _rope_table_kernel(inv_ref, cos_ref, sin_a_ref, sin_b_ref):
    rows = cos_ref.shape[0]
    pos = lax.broadcasted_iota(jnp.int32, (rows, LANES), 0).astype(F32)
    lane = lax.broadcasted_iota(jnp.int32, (rows, LANES), 1)
    ang = pos * inv_ref[...]
    c = jnp.cos(ang)
    s = jnp.sin(ang)
    cos_ref[...] = c
    sin_a_ref[...] = jnp.where((lane >= ROT_HALF) & (lane < ROT_DIM), s, 0.0)
    sin_b_ref[...] = jnp.where(lane < ROT_HALF, -s, 0.0)


def _rope_tables(n_pos):
    rows = -(-n_pos // SUBLANES) * SUBLANES
    inv = ROPE_THETA ** (-jnp.arange(ROT_HALF, dtype=F32) / ROT_HALF)
    inv_row = jnp.concatenate([inv, inv, jnp.zeros((LANES - ROT_DIM,), F32)])[None, :]
    out = jax.ShapeDtypeStruct((rows, LANES), F32)
    return pl.pallas_call(_rope_table_kernel, out_shape=(out, out, out), name="rope_tables")(inv_row)


def _ssm_prep_kernel(ls_ref, lr_ref, li_ref, lrr_ref, lir_ref, br_ref, bi_ref,
                     abr_ref, abi_ref, bbr_ref, bbi_ref):
    step = jnp.exp(ls_ref[...])

    def disc(lr, li):
        mag = jnp.exp(lr * step)
        ang = li * step
        ab_re, ab_im = mag * jnp.cos(ang), mag * jnp.sin(ang)
        den = lr * lr + li * li
        nr, ni = ab_re - 1.0, ab_im
        f_re = (nr * lr + ni * li) / den
        f_im = (ni * lr - nr * li) / den
        return ab_re, ab_im, f_re, f_im

    ab_re, ab_im, _, _ = disc(lr_ref[...], li_ref[...])
    abr_ref[...] = ab_re
    abi_ref[...] = ab_im
    _, _, f_re, f_im = disc(lrr_ref[...], lir_ref[...])
    br, bi = br_ref[...], bi_ref[...]
    bbr_ref[...] = f_re * br - f_im * bi
    bbi_ref[...] = f_re * bi + f_im * br


def _ssm_prep(lambda_re, lambda_im, log_step, b_re, b_im):
    g, n = lambda_re.shape
    flat = g, n * SSM_GROUP
    outs = (jax.ShapeDtypeStruct((g, n), F32),) * 2 + (jax.ShapeDtypeStruct(flat, F32),) * 2
    return pl.pallas_call(_ssm_prep_kernel, out_shape=outs, name="ssm_prep")(
        log_step.reshape(g, 1), lambda_re, lambda_im,
        jnp.repeat(lambda_re, SSM_GROUP, axis=1), jnp.repeat(lambda_im, SSM_GROUP, axis=1),
        b_re.reshape(flat), b_im.reshape(flat))


def _block_diag_tiles(w):
    g, r, c = w.shape
    t, n = g // SSM_GROUPS_PER_TILE, SSM_GROUPS_PER_TILE
    tiled = jnp.tile(w.reshape(t, n * r, c), (1, 1, n))
    row_grp = jnp.arange(n * r, dtype=jnp.int32)[:, None] // r
    col_grp = jnp.arange(n * c, dtype=jnp.int32)[None, :] // c
    return jnp.where(row_grp == col_grp, tiled, 0.0).astype(BF16)


def _weight_tile(w_ref, w16_ref):
    if w16_ref is None:
        return w_ref[...]
    w16_ref[...] = w_ref[...].astype(BF16)
    return w16_ref[...]


def _cast_mode(w, m, tm):
    cast = w.dtype == F32
    assert not cast or m == tm
    return cast


def _in_proj_kernel(x_ref, g_ref, w_ref, cos_ref, sa_ref, sb_ref,
                    qug_ref, k_ref, v_ref, *rest, ends):
    w16_ref, hn_ref = (rest[0], rest[1]) if len(rest) == 2 else (None, rest[0])
    j = pl.program_id(1)
    q_end, k_end, v_end, u_end = ends

    @pl.when(j == 0)
    def _():
        hn_ref[...] = (_rms_scale(x_ref[...]) * g_ref[...]).astype(BF16)

    def proj():
        return jnp.dot(hn_ref[...], _weight_tile(w_ref, w16_ref), preferred_element_type=F32)

    def rope_into(o_ref):
        acc = proj()
        chunk = min(ROPE_ROW_CHUNK, acc.shape[0])
        lane = lax.broadcasted_iota(jnp.int32, (chunk, HEAD_DIM), 1)
        partner = jnp.where(lane < ROT_HALF, lane + ROT_HALF,
                            jnp.where(lane < ROT_DIM, lane - ROT_HALF, lane))
        for r0 in range(0, acc.shape[0], chunk):
            rows = slice(r0, r0 + chunk)
            cos, sin = cos_ref[rows, :], sa_ref[rows, :] + sb_ref[rows, :]
            for h in range(acc.shape[1] // HEAD_DIM):
                cols = slice(h * HEAD_DIM, (h + 1) * HEAD_DIM)
                xh = acc[rows, cols]
                o_ref[rows, cols] = xh * cos + jnp.take_along_axis(xh, partner, axis=1) * sin

    @pl.when(j < q_end)
    def _():
        rope_into(qug_ref)

    @pl.when((j >= q_end) & (j < k_end))
    def _():
        rope_into(k_ref)

    @pl.when((j >= k_end) & (j < v_end))
    def _():
        v_ref[...] = proj()

    @pl.when((j >= v_end) & (j < u_end))
    def _():
        qug_ref[...] = proj()

    @pl.when(j >= u_end)
    def _():
        qug_ref[...] = _sigmoid(proj())


def _in_proj(x, gain, w, rope, rope_blocks, widths, tm, tn):
    m, d = x.shape
    assert all(wd % tn == 0 for wd in widths)
    tiles = [wd // tn for wd in widths]
    starts = [sum(tiles[:n]) for n in range(len(tiles))]

    def own_tile(n):
        return lambda i, j: (i, jnp.clip(j - starts[n], 0, tiles[n] - 1))

    def qug_tile(i, j):
        return i, jnp.where(j < starts[1], j, jnp.maximum(j - tiles[1] - tiles[2], tiles[0] - 1))

    table = pl.BlockSpec((tm, LANES), lambda i, j: (i % rope_blocks, 0))
    w_spec = pl.BlockSpec((d, tn), lambda i, j: (0, j))
    out_shape = [jax.ShapeDtypeStruct((m, widths[0] + widths[3] + widths[4]), F32),
                 jax.ShapeDtypeStruct((m, widths[1]), F32), jax.ShapeDtypeStruct((m, widths[2]), F32)]
    out_specs = [pl.BlockSpec((tm, tn), qug_tile), pl.BlockSpec((tm, tn), own_tile(1)),
                 pl.BlockSpec((tm, tn), own_tile(2))]
    if _cast_mode(w, m, tm):
        out_shape.append(jax.ShapeDtypeStruct(w.shape, BF16))
        out_specs.append(w_spec)
    x_spec = pl.BlockSpec((tm, d), lambda i, j: (i, 0), pipeline_mode=pl.Buffered(1))
    return pl.pallas_call(
        functools.partial(_in_proj_kernel, ends=tuple(starts[1:])),
        out_shape=tuple(out_shape),
        grid=(m // tm, sum(tiles)),
        in_specs=[x_spec, pl.BlockSpec((1, d), lambda i, j: (0, 0)),
                  w_spec, table, table, table],
        out_specs=tuple(out_specs),
        scratch_shapes=[pltpu.VMEM((tm, d), BF16)],
        compiler_params=_cparams("parallel", "arbitrary"),
        name="in_proj",
    )(x, gain, w, *rope)


def _topk_selected(gates, i):
    return _topk_rank(gates, i) < float(MOBA_TOPK)


def _topk_rank(gates, i):
    gi = gates[i]
    rank = jnp.zeros(gi.shape, F32)
    for i2, g2 in enumerate(gates):
        if i2 == i:
            continue
        ahead = (g2 >= gi) if i2 < i else (g2 > gi)
        rank = rank + jnp.where(ahead, 1.0, 0.0)
    return rank


def _moba_prompt_kernel(q_ref, k_ref, v_ref, kbias_ref, o_ref):
    seq = q_ref.shape[1]
    nb = seq // MOBA_BLOCK
    heads = q_ref.shape[2] // HEAD_DIM
    row = lax.broadcasted_iota(jnp.int32, (MOBA_BLOCK, MOBA_BLOCK), 0)
    col = lax.broadcasted_iota(jnp.int32, (MOBA_BLOCK, MOBA_BLOCK), 1)
    causal = col <= row
    operands = [_moba_head_operands(q_ref, k_ref, v_ref, kbias_ref, h) for h in range(heads)]
    items = [(h, j) for j in range(nb) for h in range(heads)]

    def scores(item):
        h, j = item
        q_aug, k_aug, _ = operands[h]
        return lax.dot_general(q_aug[j * MOBA_BLOCK:(j + 1) * MOBA_BLOCK], k_aug[:(j + 1) * MOBA_BLOCK],
                               (((1,), (1,)), ((), ())), preferred_element_type=F32)

    def finish(item, s):
        h, j = item
        v_aug = operands[h][2]
        s_own = jnp.where(causal, s[:, j * MOBA_BLOCK:], NEG)
        s = jnp.concatenate([s[:, :j * MOBA_BLOCK], s_own], axis=1) if j else s_own
        p = jnp.exp2(s - jnp.max(s, axis=-1, keepdims=True))
        o = jnp.dot(p.astype(BF16), v_aug[:(j + 1) * MOBA_BLOCK], preferred_element_type=F32)
        o_ref[0, j * MOBA_BLOCK:(j + 1) * MOBA_BLOCK, h * HEAD_DIM:(h + 1) * HEAD_DIM] = (
            o[:, :HEAD_DIM] / o[:, HEAD_DIM:])

    ahead = [scores(item) for item in items[:MOBA_SCORES_AHEAD]]
    for n, item in enumerate(items):
        if n + MOBA_SCORES_AHEAD < len(items):
            ahead.append(scores(items[n + MOBA_SCORES_AHEAD]))
        finish(item, ahead.pop(0))


def _moba_head_operands(q_ref, k_ref, v_ref, kbias_ref, h):
    seq = q_ref.shape[1]
    nb = seq // MOBA_BLOCK
    cols = slice(h * HEAD_DIM, (h + 1) * HEAD_DIM)
    q, k, v = q_ref[0, :, cols], k_ref[0, :, cols], v_ref[0, :, cols]
    k_mean = jnp.concatenate(
        [jnp.mean(k[i * MOBA_BLOCK:(i + 1) * MOBA_BLOCK], axis=0, keepdims=True) for i in range(nb)],
        axis=0)
    gate_t = lax.dot_general(k_mean, q, (((1,), (1,)), ((), ())),
                             precision=lax.Precision.HIGHEST, preferred_element_type=F32)
    blk = lax.broadcasted_iota(jnp.int32, (nb, seq), 0)
    own_blk = lax.broadcasted_iota(jnp.int32, (nb, seq), 1) // MOBA_BLOCK
    past = blk < own_blk
    dropped = jnp.zeros((nb, seq), F32)
    for i in range(nb - 1):
        gi = gate_t[i:i + 1, :]
        ahead = (gate_t > gi) | ((gate_t == gi) & (blk < i))
        rank = jnp.sum(jnp.where(ahead & past, 1.0, 0.0), axis=0, keepdims=True)
        dropped = jnp.where((blk == i) & (rank >= float(MOBA_TOPK)), 1.0, dropped)
    dropped = jnp.where(past, dropped, 0.0)
    drop_cols = jnp.concatenate([dropped, jnp.zeros((HEAD_DIM - nb, seq), F32)], axis=0).T

    c = HEAD_DIM ** -0.5 * math.log2(math.e)
    q_aug = jnp.concatenate([(q * c).astype(BF16), drop_cols.astype(BF16)], axis=1)
    k_aug = jnp.concatenate([k.astype(BF16), kbias_ref[...]], axis=1)
    v_aug = jnp.concatenate([v.astype(BF16), jnp.ones((seq, HEAD_DIM), BF16)], axis=1)
    return q_aug, k_aug, v_aug


def _moba_prompt(q, k, v, heads_per_step):
    b, s, _ = q.shape
    nb = s // MOBA_BLOCK
    assert nb <= HEAD_DIM and N_HEADS % heads_per_step == 0
    spec = pl.BlockSpec((1, s, heads_per_step * HEAD_DIM), lambda bi, h: (bi, 0, h))
    key_blk = jnp.arange(s, dtype=jnp.int32)[:, None] // MOBA_BLOCK
    kbias = jnp.where(key_blk == jnp.arange(HEAD_DIM, dtype=jnp.int32)[None, :], MASK_BIAS, 0.0)
    return pl.pallas_call(
        _moba_prompt_kernel,
        out_shape=jax.ShapeDtypeStruct(k.shape, F32),
        grid=(b, N_HEADS // heads_per_step),
        in_specs=[spec, spec, spec, pl.BlockSpec((s, HEAD_DIM), lambda bi, h: (0, 0))],
        out_specs=spec,
        compiler_params=_cparams("parallel", "parallel"),
        name="moba_prompt",
    )(q, k, v, kbias.astype(BF16))


def _value_slab_copy(pt_ref, cv_ref, vbuf, sem, seq, slot, rank, head, blk, p):
    pages_per_blk = vbuf.shape[2]
    pg = pt_ref[seq, blk * pages_per_blk + p]
    return pltpu.make_async_copy(cv_ref.at[pg, :, head, :], vbuf.at[slot, rank, p, :, head, :],
                                 sem.at[slot])


def _moba_decode_kernel(pt_ref, q_ref, kn_ref, vn_ref, *refs, pages_per_step, n_seq):
    k_refs = refs[:pages_per_step]
    cv_ref, o_ref, g_s, m_s, l_s, p_s, rank_s, own_s, vbuf, sem = refs[pages_per_step:]
    b, step = pl.program_id(0), pl.program_id(1)
    last_step = step == pl.num_programs(1) - 1
    slot = lax.rem(b, 2)
    page = k_refs[0].shape[1]
    pages_per_blk = MOBA_BLOCK // page
    blks_per_step = pages_per_step // pages_per_blk
    rows_per_page = page * N_HEADS
    n_cols = pages_per_blk * rows_per_page
    nb = g_s.shape[0]
    scale = HEAD_DIM ** -0.5
    lanes = (N_HEADS, HEAD_DIM)

    @pl.when(b < n_seq)
    def _key_pass():
        q = q_ref[0]
        qb = q.astype(BF16)
        col_head = lax.broadcasted_iota(jnp.int32, (N_HEADS, n_cols), 1) % N_HEADS
        own = col_head == lax.broadcasted_iota(jnp.int32, (N_HEADS, n_cols), 0)
        for bi in range(blks_per_step):
            kp = [k_refs[bi * pages_per_blk + p][0] for p in range(pages_per_blk)]
            k_rows = jnp.concatenate([k.reshape(rows_per_page, HEAD_DIM) for k in kp], axis=0)
            k_sum = kp[0].sum(axis=0)
            for k in kp[1:]:
                k_sum = k_sum + k.sum(axis=0)
            gate = jnp.sum(q * (k_sum * (1.0 / MOBA_BLOCK)), axis=-1, keepdims=True)
            s = lax.dot_general(qb, k_rows.astype(BF16), (((1,), (1,)), ((), ())),
                                preferred_element_type=F32) * scale
            s = jnp.where(own, s, NEG)
            m = jnp.max(s, axis=-1, keepdims=True)
            p = jnp.exp(s - m)
            blk = step * blks_per_step + bi
            g_s[blk] = jnp.broadcast_to(gate, lanes)
            m_s[slot, blk] = jnp.broadcast_to(m, lanes)
            l_s[slot, blk] = jnp.broadcast_to(jnp.sum(p, axis=-1, keepdims=True), lanes)
            p_s[slot, blk] = p

    @pl.when(last_step & (b < n_seq))
    def _select_and_fetch():
        qb = q_ref[0].astype(BF16).astype(F32)
        kn = kn_ref[0].astype(BF16).astype(F32)
        s_own = jnp.sum(qb * kn, axis=-1, keepdims=True) * scale
        own_s[slot, 0] = jnp.broadcast_to(s_own, lanes)
        own_s[slot, 1] = vn_ref[0].astype(BF16).astype(F32)
        gates = [g_s[i] for i in range(nb)]
        ranks = [_topk_rank(gates, i) for i in range(nb)]
        for i in range(nb):
            rank_s[slot, i] = ranks[i]
        for r in range(MOBA_TOPK):
            blk_of_head = jnp.zeros(lanes, F32)
            for i in range(nb):
                blk_of_head = jnp.where(ranks[i] == float(r), float(i), blk_of_head)
            blk_of_head = blk_of_head.astype(jnp.int32)
            for h in range(N_HEADS):
                blk = blk_of_head[h, 0]
                for p in range(pages_per_blk):
                    _value_slab_copy(pt_ref, cv_ref, vbuf, sem, b, slot, r, h, blk, p).start()

    @pl.when(last_step & (b >= 1))
    def _merge_previous():
        prev = 1 - slot
        for r in range(MOBA_TOPK):
            for h in range(N_HEADS):
                for p in range(pages_per_blk):
                    _value_slab_copy(pt_ref, cv_ref, vbuf, sem, 0, prev, r, h, 0, p).wait()
        s_own, v_own = own_s[prev, 0], own_s[prev, 1]
        ranks = [rank_s[prev, i] for i in range(nb)]
        m_blk = [m_s[prev, i] for i in range(nb)]
        m_all = s_own
        for i in range(nb):
            m_all = jnp.maximum(m_all, jnp.where(ranks[i] < float(MOBA_TOPK), m_blk[i], NEG))
        w_own = jnp.exp(s_own - m_all)
        den = w_own
        num = w_own.astype(BF16).astype(F32) * v_own
        for r in range(MOBA_TOPK):
            w_r = jnp.zeros(lanes, F32)
            p_r = jnp.zeros((N_HEADS, n_cols), F32)
            for i in range(nb):
                mine = jnp.where(ranks[i] == float(r), 1.0, 0.0)
                w_i = jnp.where(ranks[i] == float(r), jnp.exp(m_blk[i] - m_all), 0.0)
                w_r = w_r + w_i
                den = den + w_i * l_s[prev, i]
                p_r = p_r + jnp.broadcast_to(mine[:, :1], p_r.shape) * p_s[prev, i]
            v_rows = vbuf[prev, r].reshape(n_cols, HEAD_DIM)
            o_r = jnp.dot(p_r.astype(BF16), v_rows.astype(BF16), preferred_element_type=F32)
            num = num + w_r * o_r
        o_ref[0] = num / den


def _moba_decode(q, k_new, v_new, cache_k, cache_v, page_table, pages_per_step):
    n_dec = q.shape[0]
    page = cache_k.shape[1]
    n_pages = page_table.shape[1]
    pages_per_blk = MOBA_BLOCK // page
    assert MOBA_BLOCK % page == 0 and (n_pages * page) % MOBA_BLOCK == 0
    assert pages_per_step % pages_per_blk == 0 and n_pages % pages_per_step == 0
    nb = n_pages // pages_per_blk
    assert nb >= MOBA_TOPK
    last = n_dec - 1
    row = pl.BlockSpec((1, N_HEADS, HEAD_DIM), lambda b, i, pt: (jnp.minimum(b, last), 0, 0))

    def page_spec(p):
        return pl.BlockSpec((1, page, N_HEADS, HEAD_DIM),
                            lambda b, i, pt: (pt[jnp.minimum(b, last), i * pages_per_step + p], 0, 0, 0))

    n_cols = MOBA_BLOCK * N_HEADS
    stat = (N_HEADS, HEAD_DIM)
    return pl.pallas_call(
        functools.partial(_moba_decode_kernel, pages_per_step=pages_per_step, n_seq=n_dec),
        out_shape=jax.ShapeDtypeStruct((n_dec, N_HEADS, HEAD_DIM), F32),
        grid_spec=pltpu.PrefetchScalarGridSpec(
            num_scalar_prefetch=1,
            grid=(n_dec + 1, n_pages // pages_per_step),
            in_specs=[row, row, row] + [page_spec(p) for p in range(pages_per_step)]
            + [pl.BlockSpec(memory_space=pl.ANY)],
            out_specs=pl.BlockSpec((1, N_HEADS, HEAD_DIM), lambda b, i, pt: (jnp.maximum(b - 1, 0), 0, 0)),
            scratch_shapes=[
                pltpu.VMEM((nb,) + stat, F32),
                pltpu.VMEM((2, nb) + stat, F32),
                pltpu.VMEM((2, nb) + stat, F32),
                pltpu.VMEM((2, nb, N_HEADS, n_cols), F32),
                pltpu.VMEM((2, nb) + stat, F32),
                pltpu.VMEM((2, 2) + stat, F32),
                pltpu.VMEM((2, MOBA_TOPK, pages_per_blk, page) + stat, F32),
                pltpu.SemaphoreType.DMA((2,))]),
        compiler_params=_cparams("arbitrary", "arbitrary"),
        name="moba_decode",
    )(page_table, q, k_new, v_new, *([cache_k] * pages_per_step), cache_v)


def _s5_prompt_kernel(u_ref, wbr_ref, wbi_ref, wcr_ref, wci_ref, ar_ref, ai_ref, d_ref,
                      z_ref, hr_out, hi_out, xr_s, xi_s, hr_s, hi_s):
    c = pl.program_id(1)
    n_b, t_c = u_ref.shape[0], u_ref.shape[1]
    n_k = wbr_ref.shape[0]

    @pl.when(c == 0)
    def _():
        hr_s[...] = jnp.zeros_like(hr_s)
        hi_s[...] = jnp.zeros_like(hi_s)

    u_all = u_ref[...].reshape(n_b * t_c, u_ref.shape[2])

    def u_tile(k):
        return u_all[:, k * SSM_CH_TILE:(k + 1) * SSM_CH_TILE]

    def project_in(k):
        ub = u_tile(k).astype(BF16)
        for w_ref, x_s in ((wbr_ref, xr_s), (wbi_ref, xi_s)):
            x = jnp.dot(ub, w_ref[k], preferred_element_type=F32)
            for b in range(n_b):
                for j in range(SSM_STATE_ROWS):
                    x_s[k, b, pl.ds(j, t_c, stride=SSM_STATE_ROWS), :] = (
                        x[b * t_c:(b + 1) * t_c, j * LANES:(j + 1) * LANES])

    def scan(k):
        ar, ai = ar_ref[k], ai_ref[k]
        h = [(hr_s[k, b], hi_s[k, b]) for b in range(n_b)]
        for t in range(t_c):
            rows = slice(t * SSM_STATE_ROWS, (t + 1) * SSM_STATE_ROWS)
            for b in range(n_b):
                hr, hi = h[b]
                nhr = ar * hr - ai * hi + xr_s[k, b, rows, :]
                nhi = ar * hi + ai * hr + xi_s[k, b, rows, :]
                xr_s[k, b, rows, :] = nhr
                xi_s[k, b, rows, :] = nhi
                h[b] = (nhr, nhi)
        for b in range(n_b):
            hr_s[k, b], hi_s[k, b] = h[b]

    def project_out(k):
        def gather_states(s_ref):
            return jnp.concatenate(
                [jnp.concatenate([s_ref[k, b, pl.ds(j, t_c, stride=SSM_STATE_ROWS), :].astype(BF16)
                                  for j in range(SSM_STATE_ROWS)], axis=1) for b in range(n_b)], axis=0)

        y = (jnp.dot(gather_states(xr_s), wcr_ref[k], preferred_element_type=F32)
             - jnp.dot(gather_states(xi_s), wci_ref[k], preferred_element_type=F32)
             + d_ref[:, k * SSM_CH_TILE:(k + 1) * SSM_CH_TILE] * u_tile(k))
        z = _gelu_tanh(y).astype(z_ref.dtype).reshape(n_b, t_c, SSM_CH_TILE)
        z_ref[:, :, k * SSM_CH_TILE:(k + 1) * SSM_CH_TILE] = z

    project_in(0)
    for k in range(n_k):
        if k + 1 < n_k:
            project_in(k + 1)
        scan(k)
        if k > 0:
            project_out(k - 1)
    project_out(n_k - 1)

    @pl.when(c == pl.num_programs(1) - 1)
    def _():
        for k in range(n_k):
            for b in range(n_b):
                hr_out[b, k] = hr_s[k, b]
                hi_out[b, k] = hi_s[k, b]


def _s5_prompt(u, u_col0, wb_re, wb_im, wc_re, wc_im, ab_re, ab_im, d_skip, t_c, tiles_per_step):
    n_b, seq, _ = u.shape
    width = d_skip.shape[1]
    n_t = width // SSM_CH_TILE
    n_k = tiles_per_step
    assert n_t % n_k == 0 and u_col0 % (n_k * SSM_CH_TILE) == 0
    u_blk0 = u_col0 // (n_k * SSM_CH_TILE)
    state = jax.ShapeDtypeStruct((n_b, n_t, SSM_STATE_ROWS, LANES), F32)
    wb_spec = pl.BlockSpec((n_k, SSM_CH_TILE, SSM_STATE_TILE), lambda kt, c: (kt, 0, 0))
    wc_spec = pl.BlockSpec((n_k, SSM_STATE_TILE, SSM_CH_TILE), lambda kt, c: (kt, 0, 0))
    a_spec = pl.BlockSpec((n_k, SSM_STATE_ROWS, LANES), lambda kt, c: (kt, 0, 0))
    u_spec = pl.BlockSpec((n_b, t_c, n_k * SSM_CH_TILE), lambda kt, c: (0, c, u_blk0 + kt))
    z_spec = pl.BlockSpec((n_b, t_c, n_k * SSM_CH_TILE), lambda kt, c: (0, c, kt))
    h_spec = pl.BlockSpec((n_b, n_k, SSM_STATE_ROWS, LANES), lambda kt, c: (0, kt, 0, 0))
    x_scr = pltpu.VMEM((n_k, n_b, t_c * SSM_STATE_ROWS, LANES), F32)
    h_scr = pltpu.VMEM((n_k, n_b, SSM_STATE_ROWS, LANES), F32)
    return pl.pallas_call(
        _s5_prompt_kernel,
        out_shape=(jax.ShapeDtypeStruct((n_b, seq, width), BF16), state, state),
        grid=(n_t // n_k, seq // t_c),
        in_specs=[u_spec, wb_spec, wb_spec, wc_spec, wc_spec, a_spec, a_spec,
                  pl.BlockSpec((1, n_k * SSM_CH_TILE), lambda kt, c: (0, kt))],
        out_specs=(z_spec, h_spec, h_spec),
        scratch_shapes=[x_scr, x_scr, h_scr, h_scr],
        compiler_params=_cparams("parallel", "arbitrary"),
        name="s5_prompt",
    )(u, wb_re, wb_im, wc_re, wc_im, ab_re, ab_im, d_skip)


def _s5_step_kernel(u_ref, h0r_ref, h0i_ref, wbr_ref, wbi_ref, wcr_ref, wci_ref, ar_ref, ai_ref,
                    d_ref, z_ref, hr_out, hi_out):
    u = u_ref[...]
    ub = u.astype(BF16)
    ar, ai = ar_ref[...], ai_ref[...]
    h0r, h0i = h0r_ref[...], h0i_ref[...]
    hr = jnp.dot(ub, wbr_ref[0], preferred_element_type=F32) + (ar * h0r - ai * h0i)
    hi = jnp.dot(ub, wbi_ref[0], preferred_element_type=F32) + (ar * h0i + ai * h0r)
    hr_out[...] = hr
    hi_out[...] = hi
    y = (jnp.dot(hr.astype(BF16), wcr_ref[0], preferred_element_type=F32)
         - jnp.dot(hi.astype(BF16), wci_ref[0], preferred_element_type=F32)
         + d_ref[...] * u)
    z_ref[...] = _gelu_tanh(y).astype(z_ref.dtype)


def _s5_step(u, u_col0, h0_re, h0_im, wb_re, wb_im, wc_re, wc_im, ab_re, ab_im, d_skip):
    n_seq = u.shape[0]
    width = d_skip.shape[1]
    n_t = width // SSM_CH_TILE
    assert u_col0 % SSM_CH_TILE == 0
    u_blk0 = u_col0 // SSM_CH_TILE
    n_state = h0_re.shape[1]
    state = jax.ShapeDtypeStruct((n_seq, n_state), F32)
    u_spec = pl.BlockSpec((n_seq, SSM_CH_TILE), lambda kt: (0, u_blk0 + kt))
    z_spec = pl.BlockSpec((n_seq, SSM_CH_TILE), lambda kt: (0, kt))
    h_spec = pl.BlockSpec((n_seq, SSM_STATE_TILE), lambda kt: (0, kt))
    wb_spec = pl.BlockSpec((1, SSM_CH_TILE, SSM_STATE_TILE), lambda kt: (kt, 0, 0))
    wc_spec = pl.BlockSpec((1, SSM_STATE_TILE, SSM_CH_TILE), lambda kt: (kt, 0, 0))
    a_spec = pl.BlockSpec((1, SSM_STATE_TILE), lambda kt: (0, kt))
    return pl.pallas_call(
        _s5_step_kernel,
        out_shape=(jax.ShapeDtypeStruct((n_seq, width), BF16), state, state),
        grid=(n_t,),
        in_specs=[u_spec, h_spec, h_spec, wb_spec, wb_spec, wc_spec, wc_spec, a_spec, a_spec,
                  pl.BlockSpec((1, SSM_CH_TILE), lambda kt: (0, kt))],
        out_specs=(z_spec, h_spec, h_spec),
        compiler_params=_cparams("parallel"),
        name="s5_step",
    )(u, h0_re, h0_im, wb_re, wb_im, wc_re, wc_im, ab_re, ab_im, d_skip)


def _glu_mix_kernel(z_ref, attn_ref, ga_ref, gs_ref, wv_ref, wg_ref, o_ref, wv16_ref=None, wg16_ref=None):
    z = z_ref[...]
    val = jnp.dot(z, _weight_tile(wv_ref, wv16_ref), preferred_element_type=F32)
    gat = jnp.dot(z, _weight_tile(wg_ref, wg16_ref), preferred_element_type=F32)
    mix = ga_ref[...] * attn_ref[...] + gs_ref[...] * (val * _sigmoid(gat))
    o_ref[...] = mix.astype(o_ref.dtype)


def _glu_mix(z, attn, gates, gate_col0, w_glu_v, w_glu_g, tm, tn):
    m, d = z.shape
    assert gate_col0 % tn == 0
    ga_off = gate_col0 // tn
    gs_off = ga_off + d // tn
    col = pl.BlockSpec((tm, tn), lambda i, j: (i, j))
    w_spec = pl.BlockSpec((d, tn), lambda i, j: (0, j))
    out_shape, out_specs = [jax.ShapeDtypeStruct((m, d), BF16)], [col]
    if _cast_mode(w_glu_v, m, tm):
        out_shape += [jax.ShapeDtypeStruct(w_glu_v.shape, BF16)] * 2
        out_specs += [w_spec, w_spec]
    return pl.pallas_call(
        _glu_mix_kernel,
        out_shape=tuple(out_shape),
        grid=(m // tm, d // tn),
        in_specs=[pl.BlockSpec((tm, d), lambda i, j: (i, 0)), col,
                  pl.BlockSpec((tm, tn), lambda i, j: (i, ga_off + j)),
                  pl.BlockSpec((tm, tn), lambda i, j: (i, gs_off + j)), w_spec, w_spec],
        out_specs=tuple(out_specs),
        compiler_params=_cparams("parallel", "arbitrary"),
        name="glu_mix",
    )(z, attn, gates, gates, w_glu_v, w_glu_g)


def _out_proj_kernel(mix_ref, x_ref, w_ref, o_ref, w16_ref=None):
    o_ref[...] = x_ref[...] + jnp.dot(mix_ref[...], _weight_tile(w_ref, w16_ref),
                                      preferred_element_type=F32)


def _out_proj(mix, x, w_out, tm, tn):
    m, d = x.shape
    col = pl.BlockSpec((tm, tn), lambda i, j: (i, j))
    w_spec = pl.BlockSpec((d, tn), lambda i, j: (0, j))
    out_shape, out_specs = [jax.ShapeDtypeStruct((m, d), F32)], [col]
    if _cast_mode(w_out, m, tm):
        out_shape.append(jax.ShapeDtypeStruct(w_out.shape, BF16))
        out_specs.append(w_spec)
    return pl.pallas_call(
        _out_proj_kernel,
        out_shape=tuple(out_shape),
        grid=(m // tm, d // tn),
        in_specs=[pl.BlockSpec((tm, d), lambda i, j: (i, 0)), col, w_spec],
        out_specs=tuple(out_specs),
        compiler_params=_cparams("parallel", "arbitrary"),
        name="out_proj",
    )(mix, x, w_out)


def _ffn_kernel(x_ref, g_ref, gf_ref, wu_ref, wd_ref, o_ref, *rest):
    wu16_ref, wd16_ref, hn_ref = rest if len(rest) == 3 else (None, None, rest[0])
    f = pl.program_id(1)

    @pl.when(f == 0)
    def _():
        x = x_ref[...]
        hn_ref[...] = (_rms_scale(x) * g_ref[...]).astype(BF16)
        o_ref[...] = x

    up = jnp.dot(hn_ref[...], _weight_tile(wu_ref, wu16_ref), preferred_element_type=F32)
    act = jnp.square(jnp.maximum(up, 0.0))
    o_ref[...] += jnp.dot(act.astype(BF16), _weight_tile(wd_ref, wd16_ref),
                          preferred_element_type=F32)

    @pl.when(f == pl.num_programs(1) - 1)
    def _():
        o_ref[...] = _rms_scale(o_ref[...]) * gf_ref[...]


def _ffn(x, norm_ffn, norm_final, w_up, w_down, tm, tf):
    m, d = x.shape
    d_ff = w_up.shape[1]
    row = pl.BlockSpec((tm, d), lambda i, f: (i, 0))
    vec = pl.BlockSpec((1, d), lambda i, f: (0, 0))
    up_spec = pl.BlockSpec((d, tf), lambda i, f: (0, f))
    down_spec = pl.BlockSpec((tf, d), lambda i, f: (f, 0))
    out_shape, out_specs = [jax.ShapeDtypeStruct((m, d), F32)], [row]
    if _cast_mode(w_up, m, tm):
        out_shape += [jax.ShapeDtypeStruct(w_up.shape, BF16), jax.ShapeDtypeStruct(w_down.shape, BF16)]
        out_specs += [up_spec, down_spec]
    return pl.pallas_call(
        _ffn_kernel,
        out_shape=tuple(out_shape),
        grid=(m // tm, d_ff // tf),
        in_specs=[row, vec, vec, up_spec, down_spec],
        out_specs=tuple(out_specs),
        scratch_shapes=[pltpu.VMEM((tm, d), BF16)],
        compiler_params=_cparams("parallel", "arbitrary"),
        name="ffn",
    )(x, norm_ffn, norm_final, w_up, w_down)


def kernel(x_prompt, x_sample, cache_k, cache_v, state_ssm_re, state_ssm_im, page_table, norm_mix, w_in, lambda_re, lambda_im, log_step, b_re, b_im, c_re, c_im, d_skip, w_glu_v, w_glu_g, w_out, norm_ffn, w_up, w_down, norm_final):
    depth = w_in.shape[0]
    assert depth == 1, "single trunk layer"
    n_b, seq, d = x_prompt.shape
    n_dec, s_dec, _ = x_sample.shape
    assert s_dec == 1
    width = N_HEADS * HEAD_DIM
    n_pool, page = cache_k.shape[1], cache_k.shape[2]
    past_len = page_table.shape[1] * page
    n_groups = lambda_re.shape[1]

    cos_t, sin_a, sin_b = _rope_tables(max(seq, past_len + s_dec))
    ab_re, ab_im, bb_re, bb_im = _ssm_prep(lambda_re[0], lambda_im[0], log_step[0], b_re[0], b_im[0])
    to_in = lambda bb: _block_diag_tiles(
        bb.reshape(n_groups, SSM_STATE, SSM_GROUP).transpose(0, 2, 1))
    wb_re, wb_im = to_in(bb_re), to_in(bb_im)
    to_out = lambda cc: _block_diag_tiles(cc.transpose(0, 2, 1))
    wc_re, wc_im = to_out(c_re[0]), to_out(c_im[0])
    n_t = d // SSM_CH_TILE
    ab_re_t = ab_re.reshape(n_t, SSM_STATE_ROWS, LANES)
    ab_im_t = ab_im.reshape(n_t, SSM_STATE_ROWS, LANES)

    g_mix, g_ffn, g_fin = norm_mix.reshape(1, d), norm_ffn.reshape(1, d), norm_final.reshape(1, d)
    d_row = d_skip.reshape(1, d)
    drop_depth = lambda w: w.reshape(w.shape[1:])
    in_widths = (width, width, width, d, 2 * d)

    xs = x_sample.reshape(n_dec, d)
    rope_s = tuple(jnp.broadcast_to(t[past_len:past_len + 1], (n_dec, LANES))
                   for t in (cos_t, sin_a, sin_b))
    u_col0, gate_col0 = width, width + d
    qug_s, ks, vs, w_in16 = _in_proj(xs, g_mix, drop_depth(w_in), rope_s, 1, in_widths,
                                     tm=n_dec, tn=1024)
    heads = (n_dec, N_HEADS, HEAD_DIM)
    cache_shape = (n_pool, page, N_HEADS, HEAD_DIM)
    attn_s = _moba_decode(qug_s[:, :width].reshape(heads), ks.reshape(heads), vs.reshape(heads),
                          cache_k.reshape(cache_shape), cache_v.reshape(cache_shape), page_table,
                          pages_per_step=16)
    n_state = n_groups * SSM_STATE
    zs, hsr, hsi = _s5_step(qug_s, u_col0, state_ssm_re.reshape(n_dec, n_state),
                            state_ssm_im.reshape(n_dec, n_state),
                            wb_re, wb_im, wc_re, wc_im,
                            ab_re.reshape(1, n_state), ab_im.reshape(1, n_state), d_row)
    mix_s, w_v16, w_g16 = _glu_mix(zs, attn_s.reshape(n_dec, width), qug_s, gate_col0,
                                   drop_depth(w_glu_v), drop_depth(w_glu_g), tm=n_dec, tn=512)
    x1_s, w_o16 = _out_proj(mix_s, xs, drop_depth(w_out), tm=n_dec, tn=512)
    y_sample, w_up16, w_dn16 = _ffn(x1_s, g_ffn, g_fin, drop_depth(w_up), drop_depth(w_down),
                                    tm=n_dec, tf=512)

    tm_p = 1024
    xp = x_prompt.reshape(n_b * seq, d)
    qug_p, kp, vp = _in_proj(xp, g_mix, w_in16, (cos_t, sin_a, sin_b), seq // tm_p, in_widths,
                             tm=tm_p, tn=1024)
    qug_p3 = qug_p.reshape(n_b, seq, qug_p.shape[1])
    attn_p = _moba_prompt(qug_p3, kp.reshape(n_b, seq, width), vp.reshape(n_b, seq, width),
                          heads_per_step=4)
    zp, hpr, hpi = _s5_prompt(qug_p3, u_col0, wb_re, wb_im, wc_re, wc_im,
                              ab_re_t, ab_im_t, d_row, t_c=256, tiles_per_step=2)
    (mix_p,) = _glu_mix(zp.reshape(n_b * seq, d), attn_p.reshape(n_b * seq, width), qug_p, gate_col0,
                        w_v16, w_g16, tm=tm_p, tn=512)
    (x1_p,) = _out_proj(mix_p, xp, w_o16, tm=tm_p, tn=1024)
    (y_prompt,) = _ffn(x1_p, g_ffn, g_fin, w_up16, w_dn16, tm=512, tf=1024)

    kv_p = (1, n_b, seq, N_HEADS, HEAD_DIM)
    kv_s = (1, n_dec, s_dec, N_HEADS, HEAD_DIM)
    st_p = (1, n_b, n_groups, SSM_STATE)
    st_s = (1, n_dec, n_groups, SSM_STATE)
    return (y_prompt.reshape(n_b, seq, d), y_sample.reshape(n_dec, s_dec, d),
            kp.reshape(kv_p), vp.reshape(kv_p), hpr.reshape(st_p), hpi.reshape(st_p),
            ks.reshape(kv_s), vs.reshape(kv_s), hsr.reshape(st_s), hsi.reshape(st_s))
```

```python
import functools
import math

import jax
import jax.numpy as jnp
from jax import lax
from jax.experimental import pallas as pl
from jax.experimental.pallas import tpu as pltpu

N_HEADS = 16
HEAD_DIM = 128
ROT_DIM = HEAD_DIM // 4
ROT_HALF = ROT_DIM // 2
ROPE_THETA = 500000.0
MOBA_BLOCK = 256
MOBA_TOPK = 3
SSM_GROUP = 16
SSM_STATE = 64
RMS_EPS = 1e-6
NEG = -1e30
MASK_BIAS = -(2.0 ** 100)
MOBA_SCORES_AHEAD = 1

LANES = 128
SUBLANES = 8
MXU_DIM = 256
VMEM_LIMIT_BYTES = 60 * 1024 * 1024

SSM_CH_TILE = MXU_DIM
SSM_GROUPS_PER_TILE = SSM_CH_TILE // SSM_GROUP
SSM_STATE_TILE = SSM_GROUPS_PER_TILE * SSM_STATE
SSM_STATE_ROWS = SSM_STATE_TILE // LANES

ROPE_ROW_CHUNK = 8 * SUBLANES

PROMPT_ROW_TILE = 1024
IN_PROJ_COL_TILE = 1024
GLU_COL_TILE = 512
OUT_PROJ_COL_TILE = 1024
FFN_ROW_TILE = 512
FFN_HIDDEN_TILE = 1024
CAST_COL_TILE = 512
S5_TOKEN_CHUNK = 256
S5_TILES_PER_STEP = 2
MOBA_HEADS_PER_STEP = 4
DECODE_PAGES_PER_STEP = 16

F32 = jnp.float32
BF16 = jnp.bfloat16


def _cparams(*sem):
    return pltpu.CompilerParams(dimension_semantics=sem, vmem_limit_bytes=VMEM_LIMIT_BYTES)


def _sigmoid(x):
    return 0.5 * jnp.tanh(0.5 * x) + 0.5


def _gelu_tanh(x):
    c = math.sqrt(2.0 / math.pi)
    return 0.5 * x * (1.0 + jnp.tanh(c * (x + 0.044715 * (x * x * x))))


def _rms_scale(x):
    return x * lax.rsqrt(jnp.mean(x * x, axis=-1, keepdims=True) + RMS_EPS)


def _rope_table_kernel(inv_ref, cos_ref, sin_ref):
    rows = cos_ref.shape[0]
    pos = lax.broadcasted_iota(jnp.int32, (rows, LANES), 0).astype(F32)
    lane = lax.broadcasted_iota(jnp.int32, (rows, LANES), 1)
    ang = pos * inv_ref[...]
    cos_ref[...] = jnp.cos(ang)
    s = jnp.sin(ang)
    sin_ref[...] = jnp.where(lane < ROT_HALF, -s, s)


def _rope_tables(n_pos):
    rows = -(-n_pos // SUBLANES) * SUBLANES
    inv = ROPE_THETA ** (-jnp.arange(ROT_HALF, dtype=F32) / ROT_HALF)
    inv_row = jnp.concatenate([inv, inv, jnp.zeros((LANES - ROT_DIM,), F32)])[None, :]
    out = jax.ShapeDtypeStruct((rows, LANES), F32)
    return pl.pallas_call(_rope_table_kernel, out_shape=(out, out), name="rope_tables")(inv_row)


def _ssm_prep_kernel(ls_ref, lr_ref, li_ref, lrr_ref, lir_ref, br_ref, bi_ref,
                     abr_ref, abi_ref, bbr_ref, bbi_ref):
    step = jnp.exp(ls_ref[...])

    def disc(lr, li):
        mag = jnp.exp(lr * step)
        ang = li * step
        ab_re, ab_im = mag * jnp.cos(ang), mag * jnp.sin(ang)
        den = lr * lr + li * li
        nr, ni = ab_re - 1.0, ab_im
        f_re = (nr * lr + ni * li) / den
        f_im = (ni * lr - nr * li) / den
        return ab_re, ab_im, f_re, f_im

    ab_re, ab_im, _, _ = disc(lr_ref[...], li_ref[...])
    abr_ref[...] = ab_re
    abi_ref[...] = ab_im
    _, _, f_re, f_im = disc(lrr_ref[...], lir_ref[...])
    br, bi = br_ref[...], bi_ref[...]
    bbr_ref[...] = f_re * br - f_im * bi
    bbi_ref[...] = f_re * bi + f_im * br


def _ssm_prep(lambda_re, lambda_im, log_step, b_re, b_im):
    g, n = lambda_re.shape
    flat = g, n * SSM_GROUP
    outs = (jax.ShapeDtypeStruct((g, n), F32),) * 2 + (jax.ShapeDtypeStruct(flat, F32),) * 2
    return pl.pallas_call(_ssm_prep_kernel, out_shape=outs, name="ssm_prep")(
        log_step.reshape(g, 1), lambda_re, lambda_im,
        jnp.repeat(lambda_re, SSM_GROUP, axis=1), jnp.repeat(lambda_im, SSM_GROUP, axis=1),
        b_re.reshape(flat), b_im.reshape(flat))


def _block_diag_tiles(w):
    g, r, c = w.shape
    t, n = g // SSM_GROUPS_PER_TILE, SSM_GROUPS_PER_TILE
    tiled = jnp.tile(w.reshape(t, n * r, c), (1, 1, n))
    row_grp = jnp.arange(n * r, dtype=jnp.int32)[:, None] // r
    col_grp = jnp.arange(n * c, dtype=jnp.int32)[None, :] // c
    return jnp.where(row_grp == col_grp, tiled, 0.0).astype(BF16)


def _weight_tile(w_ref, w16_ref):
    if w16_ref is None:
        return w_ref[...]
    w16_ref[...] = w_ref[...].astype(BF16)
    return w16_ref[...]


def _cast_mode(w, m, tm):
    cast = w.dtype == F32
    assert not cast or m == tm
    return cast


def _in_proj_kernel(x_ref, g_ref, w_ref, cos_ref, sin_ref, qug_ref, k_ref, v_ref, *rest, ends):
    w16_ref, hn_ref = (rest[0], rest[1]) if len(rest) == 2 else (None, rest[0])
    j = pl.program_id(1)
    q_end, k_end, v_end, u_end = ends

    @pl.when(j == 0)
    def _():
        hn_ref[...] = (_rms_scale(x_ref[...]) * g_ref[...]).astype(BF16)

    def proj():
        return jnp.dot(hn_ref[...], _weight_tile(w_ref, w16_ref), preferred_element_type=F32)

    def rope_into(o_ref):
        acc = proj()
        chunk = min(ROPE_ROW_CHUNK, acc.shape[0])
        lane = lax.broadcasted_iota(jnp.int32, (chunk, HEAD_DIM), 1)
        partner = jnp.where(lane < ROT_HALF, lane + ROT_HALF,
                            jnp.where(lane < ROT_DIM, lane - ROT_HALF, lane))
        for r0 in range(0, acc.shape[0], chunk):
            rows = slice(r0, r0 + chunk)
            cos, sin = cos_ref[rows, :], sin_ref[rows, :]
            for h in range(acc.shape[1] // HEAD_DIM):
                cols = slice(h * HEAD_DIM, (h + 1) * HEAD_DIM)
                xh = acc[rows, cols]
                o_ref[rows, cols] = xh * cos + jnp.take_along_axis(xh, partner, axis=1) * sin

    @pl.when(j < q_end)
    def _():
        rope_into(qug_ref)

    @pl.when((j >= q_end) & (j < k_end))
    def _():
        rope_into(k_ref)

    @pl.when((j >= k_end) & (j < v_end))
    def _():
        v_ref[...] = proj()

    @pl.when((j >= v_end) & (j < u_end))
    def _():
        qug_ref[...] = proj()

    @pl.when(j >= u_end)
    def _():
        qug_ref[...] = _sigmoid(proj())


def _in_proj(x, gain, w, rope, rope_blocks, widths, tm, tn):
    m, d = x.shape
    assert all(wd % tn == 0 for wd in widths)
    tiles = [wd // tn for wd in widths]
    starts = [sum(tiles[:n]) for n in range(len(tiles))]

    def own_tile(n):
        return lambda i, j: (i, jnp.clip(j - starts[n], 0, tiles[n] - 1))

    def qug_tile(i, j):
        return i, jnp.where(j < starts[1], j, jnp.maximum(j - tiles[1] - tiles[2], tiles[0] - 1))

    table = pl.BlockSpec((tm, LANES), lambda i, j: (i % rope_blocks, 0))
    w_spec = pl.BlockSpec((d, tn), lambda i, j: (0, j))
    out_shape = [jax.ShapeDtypeStruct((m, widths[0] + widths[3] + widths[4]), F32),
                 jax.ShapeDtypeStruct((m, widths[1]), F32), jax.ShapeDtypeStruct((m, widths[2]), F32)]
    out_specs = [pl.BlockSpec((tm, tn), qug_tile), pl.BlockSpec((tm, tn), own_tile(1)),
                 pl.BlockSpec((tm, tn), own_tile(2))]
    if _cast_mode(w, m, tm):
        out_shape.append(jax.ShapeDtypeStruct(w.shape, BF16))
        out_specs.append(w_spec)
    x_spec = pl.BlockSpec((tm, d), lambda i, j: (i, 0), pipeline_mode=pl.Buffered(1))
    return pl.pallas_call(
        functools.partial(_in_proj_kernel, ends=tuple(starts[1:])),
        out_shape=tuple(out_shape),
        grid=(m // tm, sum(tiles)),
        in_specs=[x_spec, pl.BlockSpec((1, d), lambda i, j: (0, 0)), w_spec, table, table],
        out_specs=tuple(out_specs),
        scratch_shapes=[pltpu.VMEM((tm, d), BF16)],
        compiler_params=_cparams("parallel", "arbitrary"),
        name="in_proj",
    )(x, gain, w, *rope)


def _topk_rank(gates, i):
    gi = gates[i]
    rank = jnp.zeros(gi.shape, F32)
    for i2, g2 in enumerate(gates):
        if i2 == i:
            continue
        ahead = (g2 >= gi) if i2 < i else (g2 > gi)
        rank = rank + jnp.where(ahead, 1.0, 0.0)
    return rank


def _moba_prompt_kernel(q_ref, k_ref, v_ref, kbias_ref, o_ref):
    seq = q_ref.shape[1]
    nb = seq // MOBA_BLOCK
    heads = q_ref.shape[2] // HEAD_DIM
    row = lax.broadcasted_iota(jnp.int32, (MOBA_BLOCK, MOBA_BLOCK), 0)
    col = lax.broadcasted_iota(jnp.int32, (MOBA_BLOCK, MOBA_BLOCK), 1)
    causal = col <= row
    operands = [_moba_head_operands(q_ref, k_ref, v_ref, kbias_ref, h) for h in range(heads)]
    items = [(h, j) for j in range(nb) for h in range(heads)]

    def scores(item):
        h, j = item
        q_aug, k_aug, _ = operands[h]
        return lax.dot_general(q_aug[j * MOBA_BLOCK:(j + 1) * MOBA_BLOCK], k_aug[:(j + 1) * MOBA_BLOCK],
                               (((1,), (1,)), ((), ())), preferred_element_type=F32)

    def finish(item, s):
        h, j = item
        v_aug = operands[h][2]
        s_own = jnp.where(causal, s[:, j * MOBA_BLOCK:], NEG)
        s = jnp.concatenate([s[:, :j * MOBA_BLOCK], s_own], axis=1) if j else s_own
        p = jnp.exp2(s - jnp.max(s, axis=-1, keepdims=True))
        o = jnp.dot(p.astype(BF16), v_aug[:(j + 1) * MOBA_BLOCK], preferred_element_type=F32)
        o_ref[0, j * MOBA_BLOCK:(j + 1) * MOBA_BLOCK, h * HEAD_DIM:(h + 1) * HEAD_DIM] = (
            o[:, :HEAD_DIM] / o[:, HEAD_DIM:])

    ahead = [scores(item) for item in items[:MOBA_SCORES_AHEAD]]
    for n, item in enumerate(items):
        if n + MOBA_SCORES_AHEAD < len(items):
            ahead.append(scores(items[n + MOBA_SCORES_AHEAD]))
        finish(item, ahead.pop(0))


def _moba_head_operands(q_ref, k_ref, v_ref, kbias_ref, h):
    seq = q_ref.shape[1]
    nb = seq // MOBA_BLOCK
    cols = slice(h * HEAD_DIM, (h + 1) * HEAD_DIM)
    q, k, v = q_ref[0, :, cols], k_ref[0, :, cols], v_ref[0, :, cols]
    k_mean = jnp.concatenate(
        [jnp.mean(k[i * MOBA_BLOCK:(i + 1) * MOBA_BLOCK], axis=0, keepdims=True) for i in range(nb)],
        axis=0)
    gate_t = lax.dot_general(k_mean, q, (((1,), (1,)), ((), ())),
                             precision=lax.Precision.HIGHEST, preferred_element_type=F32)
    blk = lax.broadcasted_iota(jnp.int32, (nb, seq), 0)
    own_blk = lax.broadcasted_iota(jnp.int32, (nb, seq), 1) // MOBA_BLOCK
    past = blk < own_blk
    dropped = jnp.zeros((nb, seq), F32)
    for i in range(nb - 1):
        gi = gate_t[i:i + 1, :]
        ahead = (gate_t > gi) | ((gate_t == gi) & (blk < i))
        rank = jnp.sum(jnp.where(ahead & past, 1.0, 0.0), axis=0, keepdims=True)
        dropped = jnp.where((blk == i) & (rank >= float(MOBA_TOPK)), 1.0, dropped)
    dropped = jnp.where(past, dropped, 0.0)
    drop_cols = jnp.concatenate([dropped, jnp.zeros((HEAD_DIM - nb, seq), F32)], axis=0).T

    c = HEAD_DIM ** -0.5 * math.log2(math.e)
    q_aug = jnp.concatenate([(q * c).astype(BF16), drop_cols.astype(BF16)], axis=1)
    k_aug = jnp.concatenate([k.astype(BF16), kbias_ref[...]], axis=1)
    v_aug = jnp.concatenate([v.astype(BF16), jnp.ones((seq, HEAD_DIM), BF16)], axis=1)
    return q_aug, k_aug, v_aug


def _moba_prompt(q, k, v, heads_per_step):
    b, s, _ = q.shape
    nb = s // MOBA_BLOCK
    assert nb <= HEAD_DIM and N_HEADS % heads_per_step == 0
    spec = pl.BlockSpec((1, s, heads_per_step * HEAD_DIM), lambda bi, h: (bi, 0, h))
    key_blk = jnp.arange(s, dtype=jnp.int32)[:, None] // MOBA_BLOCK
    kbias = jnp.where(key_blk == jnp.arange(HEAD_DIM, dtype=jnp.int32)[None, :], MASK_BIAS, 0.0)
    return pl.pallas_call(
        _moba_prompt_kernel,
        out_shape=jax.ShapeDtypeStruct(k.shape, F32),
        grid=(b, N_HEADS // heads_per_step),
        in_specs=[spec, spec, spec, pl.BlockSpec((s, HEAD_DIM), lambda bi, h: (0, 0))],
        out_specs=spec,
        compiler_params=_cparams("parallel", "parallel"),
        name="moba_prompt",
    )(q, k, v, kbias.astype(BF16))


def _value_slab_copy(pt_ref, cv_ref, vbuf, sem, seq, slot, rank, head, blk, p):
    pages_per_blk = vbuf.shape[2]
    pg = pt_ref[seq, blk * pages_per_blk + p]
    return pltpu.make_async_copy(cv_ref.at[pg, :, head, :], vbuf.at[slot, rank, p, :, head, :],
                                 sem.at[slot])


def _moba_decode_kernel(pt_ref, q_ref, kn_ref, vn_ref, *refs, pages_per_step, n_seq):
    k_refs = refs[:pages_per_step]
    cv_ref, o_ref, g_s, m_s, l_s, p_s, rank_s, own_s, vbuf, sem = refs[pages_per_step:]
    b, step = pl.program_id(0), pl.program_id(1)
    last_step = step == pl.num_programs(1) - 1
    slot = lax.rem(b, 2)
    page = k_refs[0].shape[1]
    pages_per_blk = MOBA_BLOCK // page
    blks_per_step = pages_per_step // pages_per_blk
    rows_per_page = page * N_HEADS
    n_cols = pages_per_blk * rows_per_page
    nb = g_s.shape[0]
    scale = HEAD_DIM ** -0.5
    lanes = (N_HEADS, HEAD_DIM)

    @pl.when(b < n_seq)
    def _key_pass():
        q = q_ref[0]
        qb = q.astype(BF16)
        col_head = lax.broadcasted_iota(jnp.int32, (N_HEADS, n_cols), 1) % N_HEADS
        own = col_head == lax.broadcasted_iota(jnp.int32, (N_HEADS, n_cols), 0)
        for bi in range(blks_per_step):
            kp = [k_refs[bi * pages_per_blk + p][0] for p in range(pages_per_blk)]
            k_rows = jnp.concatenate([k.reshape(rows_per_page, HEAD_DIM) for k in kp], axis=0)
            k_sum = kp[0].sum(axis=0)
            for k in kp[1:]:
                k_sum = k_sum + k.sum(axis=0)
            gate = jnp.sum(q * (k_sum * (1.0 / MOBA_BLOCK)), axis=-1, keepdims=True)
            s = lax.dot_general(qb, k_rows.astype(BF16), (((1,), (1,)), ((), ())),
                                preferred_element_type=F32) * scale
            s = jnp.where(own, s, NEG)
            m = jnp.max(s, axis=-1, keepdims=True)
            p = jnp.exp(s - m)
            blk = step * blks_per_step + bi
            g_s[blk] = jnp.broadcast_to(gate, lanes)
            m_s[slot, blk] = jnp.broadcast_to(m, lanes)
            l_s[slot, blk] = jnp.broadcast_to(jnp.sum(p, axis=-1, keepdims=True), lanes)
            p_s[slot, blk] = p

    @pl.when(last_step & (b < n_seq))
    def _select_and_fetch():
        qb = q_ref[0].astype(BF16).astype(F32)
        kn = kn_ref[0].astype(BF16).astype(F32)
        s_own = jnp.sum(qb * kn, axis=-1, keepdims=True) * scale
        own_s[slot, 0] = jnp.broadcast_to(s_own, lanes)
        own_s[slot, 1] = vn_ref[0].astype(BF16).astype(F32)
        gates = [g_s[i] for i in range(nb)]
        ranks = [_topk_rank(gates, i) for i in range(nb)]
        for i in range(nb):
            rank_s[slot, i] = ranks[i]
        for r in range(MOBA_TOPK):
            blk_of_head = jnp.zeros(lanes, F32)
            for i in range(nb):
                blk_of_head = jnp.where(ranks[i] == float(r), float(i), blk_of_head)
            blk_of_head = blk_of_head.astype(jnp.int32)
            for h in range(N_HEADS):
                blk = blk_of_head[h, 0]
                for p in range(pages_per_blk):
                    _value_slab_copy(pt_ref, cv_ref, vbuf, sem, b, slot, r, h, blk, p).start()

    @pl.when(last_step & (b >= 1))
    def _merge_previous():
        prev = 1 - slot
        for r in range(MOBA_TOPK):
            for h in range(N_HEADS):
                for p in range(pages_per_blk):
                    _value_slab_copy(pt_ref, cv_ref, vbuf, sem, 0, prev, r, h, 0, p).wait()
        s_own, v_own = own_s[prev, 0], own_s[prev, 1]
        ranks = [rank_s[prev, i] for i in range(nb)]
        m_blk = [m_s[prev, i] for i in range(nb)]
        m_all = s_own
        for i in range(nb):
            m_all = jnp.maximum(m_all, jnp.where(ranks[i] < float(MOBA_TOPK), m_blk[i], NEG))
        w_own = jnp.exp(s_own - m_all)
        den = w_own
        num = w_own.astype(BF16).astype(F32) * v_own
        for r in range(MOBA_TOPK):
            w_r = jnp.zeros(lanes, F32)
            p_r = jnp.zeros((N_HEADS, n_cols), F32)
            for i in range(nb):
                mine = jnp.where(ranks[i] == float(r), 1.0, 0.0)
                w_i = jnp.where(ranks[i] == float(r), jnp.exp(m_blk[i] - m_all), 0.0)
                w_r = w_r + w_i
                den = den + w_i * l_s[prev, i]
                p_r = p_r + jnp.broadcast_to(mine[:, :1], p_r.shape) * p_s[prev, i]
            v_rows = vbuf[prev, r].reshape(n_cols, HEAD_DIM)
            o_r = jnp.dot(p_r.astype(BF16), v_rows.astype(BF16), preferred_element_type=F32)
            num = num + w_r * o_r
        o_ref[0] = num / den


def _moba_decode(q, k_new, v_new, cache_k, cache_v, page_table, pages_per_step):
    n_dec = q.shape[0]
    page = cache_k.shape[1]
    n_pages = page_table.shape[1]
    pages_per_blk = MOBA_BLOCK // page
    assert MOBA_BLOCK % page == 0 and (n_pages * page) % MOBA_BLOCK == 0
    assert pages_per_step % pages_per_blk == 0 and n_pages % pages_per_step == 0
    nb = n_pages // pages_per_blk
    assert nb >= MOBA_TOPK
    last = n_dec - 1
    row = pl.BlockSpec((1, N_HEADS, HEAD_DIM), lambda b, i, pt: (jnp.minimum(b, last), 0, 0))

    def page_spec(p):
        return pl.BlockSpec((1, page, N_HEADS, HEAD_DIM),
                            lambda b, i, pt: (pt[jnp.minimum(b, last), i * pages_per_step + p], 0, 0, 0))

    n_cols = MOBA_BLOCK * N_HEADS
    stat = (N_HEADS, HEAD_DIM)
    return pl.pallas_call(
        functools.partial(_moba_decode_kernel, pages_per_step=pages_per_step, n_seq=n_dec),
        out_shape=jax.ShapeDtypeStruct((n_dec, N_HEADS, HEAD_DIM), F32),
        grid_spec=pltpu.PrefetchScalarGridSpec(
            num_scalar_prefetch=1,
            grid=(n_dec + 1, n_pages // pages_per_step),
            in_specs=[row, row, row] + [page_spec(p) for p in range(pages_per_step)]
            + [pl.BlockSpec(memory_space=pl.ANY)],
            out_specs=pl.BlockSpec((1, N_HEADS, HEAD_DIM), lambda b, i, pt: (jnp.maximum(b - 1, 0), 0, 0)),
            scratch_shapes=[
                pltpu.VMEM((nb,) + stat, F32),
                pltpu.VMEM((2, nb) + stat, F32),
                pltpu.VMEM((2, nb) + stat, F32),
                pltpu.VMEM((2, nb, N_HEADS, n_cols), F32),
                pltpu.VMEM((2, nb) + stat, F32),
                pltpu.VMEM((2, 2) + stat, F32),
                pltpu.VMEM((2, MOBA_TOPK, pages_per_blk, page) + stat, F32),
                pltpu.SemaphoreType.DMA((2,))]),
        compiler_params=_cparams("arbitrary", "arbitrary"),
        name="moba_decode",
    )(page_table, q, k_new, v_new, *([cache_k] * pages_per_step), cache_v)


def _s5_prompt_kernel(u_ref, wbr_ref, wbi_ref, wcr_ref, wci_ref, ar_ref, ai_ref, d_ref,
                      z_ref, hr_out, hi_out, xr_s, xi_s, hr_s, hi_s):
    c = pl.program_id(1)
    n_b, t_c = u_ref.shape[0], u_ref.shape[1]
    n_k = wbr_ref.shape[0]

    @pl.when(c == 0)
    def _():
        hr_s[...] = jnp.zeros_like(hr_s)
        hi_s[...] = jnp.zeros_like(hi_s)

    u_all = u_ref[...].reshape(n_b * t_c, u_ref.shape[2])

    def u_tile(k):
        return u_all[:, k * SSM_CH_TILE:(k + 1) * SSM_CH_TILE]

    def project_in(k):
        ub = u_tile(k).astype(BF16)
        for w_ref, x_s in ((wbr_ref, xr_s), (wbi_ref, xi_s)):
            x = jnp.dot(ub, w_ref[k], preferred_element_type=F32)
            for b in range(n_b):
                for j in range(SSM_STATE_ROWS):
                    x_s[k, b, pl.ds(j, t_c, stride=SSM_STATE_ROWS), :] = (
                        x[b * t_c:(b + 1) * t_c, j * LANES:(j + 1) * LANES])

    def scan(k):
        ar, ai = ar_ref[k], ai_ref[k]
        h = [(hr_s[k, b], hi_s[k, b]) for b in range(n_b)]
        for t in range(t_c):
            rows = slice(t * SSM_STATE_ROWS, (t + 1) * SSM_STATE_ROWS)
            for b in range(n_b):
                hr, hi = h[b]
                nhr = ar * hr - ai * hi + xr_s[k, b, rows, :]
                nhi = ar * hi + ai * hr + xi_s[k, b, rows, :]
                xr_s[k, b, rows, :] = nhr
                xi_s[k, b, rows, :] = nhi
                h[b] = (nhr, nhi)
        for b in range(n_b):
            hr_s[k, b], hi_s[k, b] = h[b]

    def project_out(k):
        def gather_states(s_ref):
            return jnp.concatenate(
                [jnp.concatenate([s_ref[k, b, pl.ds(j, t_c, stride=SSM_STATE_ROWS), :].astype(BF16)
                                  for j in range(SSM_STATE_ROWS)], axis=1) for b in range(n_b)], axis=0)

        y = (jnp.dot(gather_states(xr_s), wcr_ref[k], preferred_element_type=F32)
             - jnp.dot(gather_states(xi_s), wci_ref[k], preferred_element_type=F32)
             + d_ref[:, k * SSM_CH_TILE:(k + 1) * SSM_CH_TILE] * u_tile(k))
        z = _gelu_tanh(y).astype(z_ref.dtype).reshape(n_b, t_c, SSM_CH_TILE)
        z_ref[:, :, k * SSM_CH_TILE:(k + 1) * SSM_CH_TILE] = z

    project_in(0)
    for k in range(n_k):
        if k + 1 < n_k:
            project_in(k + 1)
        scan(k)
        if k > 0:
            project_out(k - 1)
    project_out(n_k - 1)

    @pl.when(c == pl.num_programs(1) - 1)
    def _():
        for k in range(n_k):
            for b in range(n_b):
                hr_out[b, k] = hr_s[k, b]
                hi_out[b, k] = hi_s[k, b]


def _s5_prompt(u, u_col0, wb_re, wb_im, wc_re, wc_im, ab_re, ab_im, d_skip, t_c, tiles_per_step):
    n_b, seq, _ = u.shape
    width = d_skip.shape[1]
    n_t = width // SSM_CH_TILE
    n_k = tiles_per_step
    assert n_t % n_k == 0 and u_col0 % (n_k * SSM_CH_TILE) == 0
    u_blk0 = u_col0 // (n_k * SSM_CH_TILE)
    state = jax.ShapeDtypeStruct((n_b, n_t, SSM_STATE_ROWS, LANES), F32)
    wb_spec = pl.BlockSpec((n_k, SSM_CH_TILE, SSM_STATE_TILE), lambda kt, c: (kt, 0, 0))
    wc_spec = pl.BlockSpec((n_k, SSM_STATE_TILE, SSM_CH_TILE), lambda kt, c: (kt, 0, 0))
    a_spec = pl.BlockSpec((n_k, SSM_STATE_ROWS, LANES), lambda kt, c: (kt, 0, 0))
    u_spec = pl.BlockSpec((n_b, t_c, n_k * SSM_CH_TILE), lambda kt, c: (0, c, u_blk0 + kt))
    z_spec = pl.BlockSpec((n_b, t_c, n_k * SSM_CH_TILE), lambda kt, c: (0, c, kt))
    h_spec = pl.BlockSpec((n_b, n_k, SSM_STATE_ROWS, LANES), lambda kt, c: (0, kt, 0, 0))
    x_scr = pltpu.VMEM((n_k, n_b, t_c * SSM_STATE_ROWS, LANES), F32)
    h_scr = pltpu.VMEM((n_k, n_b, SSM_STATE_ROWS, LANES), F32)
    return pl.pallas_call(
        _s5_prompt_kernel,
        out_shape=(jax.ShapeDtypeStruct((n_b, seq, width), BF16), state, state),
        grid=(n_t // n_k, seq // t_c),
        in_specs=[u_spec, wb_spec, wb_spec, wc_spec, wc_spec, a_spec, a_spec,
                  pl.BlockSpec((1, n_k * SSM_CH_TILE), lambda kt, c: (0, kt))],
        out_specs=(z_spec, h_spec, h_spec),
        scratch_shapes=[x_scr, x_scr, h_scr, h_scr],
        compiler_params=_cparams("parallel", "arbitrary"),
        name="s5_prompt",
    )(u, wb_re, wb_im, wc_re, wc_im, ab_re, ab_im, d_skip)


def _s5_step_kernel(u_ref, h0r_ref, h0i_ref, wbr_ref, wbi_ref, wcr_ref, wci_ref, ar_ref, ai_ref,
                    d_ref, z_ref, hr_out, hi_out):
    u = u_ref[...]
    ub = u.astype(BF16)
    ar, ai = ar_ref[...], ai_ref[...]
    h0r, h0i = h0r_ref[...], h0i_ref[...]
    hr = jnp.dot(ub, wbr_ref[0], preferred_element_type=F32) + (ar * h0r - ai * h0i)
    hi = jnp.dot(ub, wbi_ref[0], preferred_element_type=F32) + (ar * h0i + ai * h0r)
    hr_out[...] = hr
    hi_out[...] = hi
    y = (jnp.dot(hr.astype(BF16), wcr_ref[0], preferred_element_type=F32)
         - jnp.dot(hi.astype(BF16), wci_ref[0], preferred_element_type=F32)
         + d_ref[...] * u)
    z_ref[...] = _gelu_tanh(y).astype(z_ref.dtype)


def _s5_step(u, u_col0, h0_re, h0_im, wb_re, wb_im, wc_re, wc_im, ab_re, ab_im, d_skip):
    n_seq = u.shape[0]
    width = d_skip.shape[1]
    n_t = width // SSM_CH_TILE
    assert u_col0 % SSM_CH_TILE == 0
    u_blk0 = u_col0 // SSM_CH_TILE
    n_state = h0_re.shape[1]
    state = jax.ShapeDtypeStruct((n_seq, n_state), F32)
    u_spec = pl.BlockSpec((n_seq, SSM_CH_TILE), lambda kt: (0, u_blk0 + kt))
    z_spec = pl.BlockSpec((n_seq, SSM_CH_TILE), lambda kt: (0, kt))
    h_spec = pl.BlockSpec((n_seq, SSM_STATE_TILE), lambda kt: (0, kt))
    wb_spec = pl.BlockSpec((1, SSM_CH_TILE, SSM_STATE_TILE), lambda kt: (kt, 0, 0))
    wc_spec = pl.BlockSpec((1, SSM_STATE_TILE, SSM_CH_TILE), lambda kt: (kt, 0, 0))
    a_spec = pl.BlockSpec((1, SSM_STATE_TILE), lambda kt: (0, kt))
    return pl.pallas_call(
        _s5_step_kernel,
        out_shape=(jax.ShapeDtypeStruct((n_seq, width), BF16), state, state),
        grid=(n_t,),
        in_specs=[u_spec, h_spec, h_spec, wb_spec, wb_spec, wc_spec, wc_spec, a_spec, a_spec,
                  pl.BlockSpec((1, SSM_CH_TILE), lambda kt: (0, kt))],
        out_specs=(z_spec, h_spec, h_spec),
        compiler_params=_cparams("parallel"),
        name="s5_step",
    )(u, h0_re, h0_im, wb_re, wb_im, wc_re, wc_im, ab_re, ab_im, d_skip)


def _glu_mix_kernel(z_ref, attn_ref, ga_ref, gs_ref, wv_ref, wg_ref, o_ref, wv16_ref=None, wg16_ref=None):
    z = z_ref[...]
    val = jnp.dot(z, _weight_tile(wv_ref, wv16_ref), preferred_element_type=F32)
    gat = jnp.dot(z, _weight_tile(wg_ref, wg16_ref), preferred_element_type=F32)
    mix = ga_ref[...] * attn_ref[...] + gs_ref[...] * (val * _sigmoid(gat))
    o_ref[...] = mix.astype(o_ref.dtype)


def _glu_mix(z, attn, gates, gate_col0, w_glu_v, w_glu_g, tm, tn):
    m, d = z.shape
    assert gate_col0 % tn == 0
    ga_off = gate_col0 // tn
    gs_off = ga_off + d // tn
    col = pl.BlockSpec((tm, tn), lambda i, j: (i, j))
    w_spec = pl.BlockSpec((d, tn), lambda i, j: (0, j))
    out_shape, out_specs = [jax.ShapeDtypeStruct((m, d), BF16)], [col]
    if _cast_mode(w_glu_v, m, tm):
        out_shape += [jax.ShapeDtypeStruct(w_glu_v.shape, BF16)] * 2
        out_specs += [w_spec, w_spec]
    return pl.pallas_call(
        _glu_mix_kernel,
        out_shape=tuple(out_shape),
        grid=(m // tm, d // tn),
        in_specs=[pl.BlockSpec((tm, d), lambda i, j: (i, 0)), col,
                  pl.BlockSpec((tm, tn), lambda i, j: (i, ga_off + j)),
                  pl.BlockSpec((tm, tn), lambda i, j: (i, gs_off + j)), w_spec, w_spec],
        out_specs=tuple(out_specs),
        compiler_params=_cparams("parallel", "arbitrary"),
        name="glu_mix",
    )(z, attn, gates, gates, w_glu_v, w_glu_g)


def _out_proj_kernel(mix_ref, x_ref, w_ref, o_ref, w16_ref=None):
    o_ref[...] = x_ref[...] + jnp.dot(mix_ref[...], _weight_tile(w_ref, w16_ref),
                                      preferred_element_type=F32)


def _out_proj(mix, x, w_out, tm, tn):
    m, d = x.shape
    col = pl.BlockSpec((tm, tn), lambda i, j: (i, j))
    w_spec = pl.BlockSpec((d, tn), lambda i, j: (0, j))
    out_shape, out_specs = [jax.ShapeDtypeStruct((m, d), F32)], [col]
    if _cast_mode(w_out, m, tm):
        out_shape.append(jax.ShapeDtypeStruct(w_out.shape, BF16))
        out_specs.append(w_spec)
    return pl.pallas_call(
        _out_proj_kernel,
        out_shape=tuple(out_shape),
        grid=(m // tm, d // tn),
        in_specs=[pl.BlockSpec((tm, d), lambda i, j: (i, 0)), col, w_spec],
        out_specs=tuple(out_specs),
        compiler_params=_cparams("parallel", "arbitrary"),
        name="out_proj",
    )(mix, x, w_out)


def _ffn_kernel(x_ref, g_ref, gf_ref, wu_ref, wd_ref, o_ref, *rest):
    wu16_ref, wd16_ref, hn_ref = rest if len(rest) == 3 else (None, None, rest[0])
    f = pl.program_id(1)

    @pl.when(f == 0)
    def _():
        x = x_ref[...]
        hn_ref[...] = (_rms_scale(x) * g_ref[...]).astype(BF16)
        o_ref[...] = x

    up = jnp.dot(hn_ref[...], _weight_tile(wu_ref, wu16_ref), preferred_element_type=F32)
    act = jnp.square(jnp.maximum(up, 0.0))
    o_ref[...] += jnp.dot(act.astype(BF16), _weight_tile(wd_ref, wd16_ref),
                          preferred_element_type=F32)

    @pl.when(f == pl.num_programs(1) - 1)
    def _():
        o_ref[...] = _rms_scale(o_ref[...]) * gf_ref[...]


def _ffn(x, norm_ffn, norm_final, w_up, w_down, tm, tf):
    m, d = x.shape
    d_ff = w_up.shape[1]
    row = pl.BlockSpec((tm, d), lambda i, f: (i, 0))
    vec = pl.BlockSpec((1, d), lambda i, f: (0, 0))
    up_spec = pl.BlockSpec((d, tf), lambda i, f: (0, f))
    down_spec = pl.BlockSpec((tf, d), lambda i, f: (f, 0))
    out_shape, out_specs = [jax.ShapeDtypeStruct((m, d), F32)], [row]
    if _cast_mode(w_up, m, tm):
        out_shape += [jax.ShapeDtypeStruct(w_up.shape, BF16), jax.ShapeDtypeStruct(w_down.shape, BF16)]
        out_specs += [up_spec, down_spec]
    return pl.pallas_call(
        _ffn_kernel,
        out_shape=tuple(out_shape),
        grid=(m // tm, d_ff // tf),
        in_specs=[row, vec, vec, up_spec, down_spec],
        out_specs=tuple(out_specs),
        scratch_shapes=[pltpu.VMEM((tm, d), BF16)],
        compiler_params=_cparams("parallel", "arbitrary"),
        name="ffn",
    )(x, norm_ffn, norm_final, w_up, w_down)


def kernel(x_prompt, x_sample, cache_k, cache_v, state_ssm_re, state_ssm_im, page_table, norm_mix, w_in, lambda_re, lambda_im, log_step, b_re, b_im, c_re, c_im, d_skip, w_glu_v, w_glu_g, w_out, norm_ffn, w_up, w_down, norm_final):
    depth = w_in.shape[0]
    assert depth == 1, "single trunk layer"
    n_b, seq, d = x_prompt.shape
    n_dec, s_dec, _ = x_sample.shape
    assert s_dec == 1
    width = N_HEADS * HEAD_DIM
    n_pool, page = cache_k.shape[1], cache_k.shape[2]
    past_len = page_table.shape[1] * page
    n_groups = lambda_re.shape[1]

    rope_p = _rope_tables(max(seq, past_len + s_dec))
    ab_re, ab_im, bb_re, bb_im = _ssm_prep(lambda_re[0], lambda_im[0], log_step[0], b_re[0], b_im[0])
    to_in = lambda bb: _block_diag_tiles(
        bb.reshape(n_groups, SSM_STATE, SSM_GROUP).transpose(0, 2, 1))
    wb_re, wb_im = to_in(bb_re), to_in(bb_im)
    to_out = lambda cc: _block_diag_tiles(cc.transpose(0, 2, 1))
    wc_re, wc_im = to_out(c_re[0]), to_out(c_im[0])
    n_t = d // SSM_CH_TILE
    ab_re_t = ab_re.reshape(n_t, SSM_STATE_ROWS, LANES)
    ab_im_t = ab_im.reshape(n_t, SSM_STATE_ROWS, LANES)

    g_mix, g_ffn, g_fin = norm_mix.reshape(1, d), norm_ffn.reshape(1, d), norm_final.reshape(1, d)
    d_row = d_skip.reshape(1, d)
    drop_depth = lambda w: w.reshape(w.shape[1:])
    in_widths = (width, width, width, d, 2 * d)

    xs = x_sample.reshape(n_dec, d)
    rope_s = tuple(jnp.broadcast_to(t[past_len:past_len + 1], (n_dec, LANES)) for t in rope_p)
    u_col0, gate_col0 = width, width + d
    qug_s, ks, vs, w_in16 = _in_proj(xs, g_mix, drop_depth(w_in), rope_s, 1, in_widths,
                                     tm=n_dec, tn=IN_PROJ_COL_TILE)
    heads = (n_dec, N_HEADS, HEAD_DIM)
    cache_shape = (n_pool, page, N_HEADS, HEAD_DIM)
    attn_s = _moba_decode(qug_s[:, :width].reshape(heads), ks.reshape(heads), vs.reshape(heads),
                          cache_k.reshape(cache_shape), cache_v.reshape(cache_shape), page_table,
                          pages_per_step=min(DECODE_PAGES_PER_STEP, page_table.shape[1]))
    n_state = n_groups * SSM_STATE
    zs, hsr, hsi = _s5_step(qug_s, u_col0, state_ssm_re.reshape(n_dec, n_state),
                            state_ssm_im.reshape(n_dec, n_state),
                            wb_re, wb_im, wc_re, wc_im,
                            ab_re.reshape(1, n_state), ab_im.reshape(1, n_state), d_row)
    mix_s, w_v16, w_g16 = _glu_mix(zs, attn_s.reshape(n_dec, width), qug_s, gate_col0,
                                   drop_depth(w_glu_v), drop_depth(w_glu_g),
                                   tm=n_dec, tn=CAST_COL_TILE)
    x1_s, w_o16 = _out_proj(mix_s, xs, drop_depth(w_out), tm=n_dec, tn=CAST_COL_TILE)
    y_sample, w_up16, w_dn16 = _ffn(x1_s, g_ffn, g_fin, drop_depth(w_up), drop_depth(w_down),
                                    tm=n_dec, tf=CAST_COL_TILE)

    tm_p = PROMPT_ROW_TILE
    xp = x_prompt.reshape(n_b * seq, d)
    qug_p, kp, vp = _in_proj(xp, g_mix, w_in16, rope_p, seq // tm_p, in_widths,
                             tm=tm_p, tn=IN_PROJ_COL_TILE)
    qug_p3 = qug_p.reshape(n_b, seq, qug_p.shape[1])
    attn_p = _moba_prompt(qug_p3, kp.reshape(n_b, seq, width), vp.reshape(n_b, seq, width),
                          heads_per_step=MOBA_HEADS_PER_STEP)
    zp, hpr, hpi = _s5_prompt(qug_p3, u_col0, wb_re, wb_im, wc_re, wc_im, ab_re_t, ab_im_t, d_row,
                              t_c=S5_TOKEN_CHUNK, tiles_per_step=S5_TILES_PER_STEP)
    (mix_p,) = _glu_mix(zp.reshape(n_b * seq, d), attn_p.reshape(n_b * seq, width), qug_p, gate_col0,
                        w_v16, w_g16, tm=tm_p, tn=GLU_COL_TILE)
    (x1_p,) = _out_proj(mix_p, xp, w_o16, tm=tm_p, tn=OUT_PROJ_COL_TILE)
    (y_prompt,) = _ffn(x1_p, g_ffn, g_fin, w_up16, w_dn16, tm=FFN_ROW_TILE, tf=FFN_HIDDEN_TILE)

    kv_p = (1, n_b, seq, N_HEADS, HEAD_DIM)
    kv_s = (1, n_dec, s_dec, N_HEADS, HEAD_DIM)
    st_p = (1, n_b, n_groups, SSM_STATE)
    st_s = (1, n_dec, n_groups, SSM_STATE)
    return (y_prompt.reshape(n_b, seq, d), y_sample.reshape(n_dec, s_dec, d),
            kp.reshape(kv_p), vp.reshape(kv_p), hpr.reshape(st_p), hpi.reshape(st_p),
            ks.reshape(kv_s), vs.reshape(kv_s), hsr.reshape(st_s), hsi.reshape(st_s))
```

```python
import functools
import math

import jax
import jax.numpy as jnp
from jax import lax
from jax.experimental import pallas as pl
from jax.experimental.pallas import tpu as pltpu

N_HEADS = 16
HEAD_DIM = 128
ROT_DIM = HEAD_DIM // 4
ROT_HALF = ROT_DIM // 2
ROPE_THETA = 500000.0
MOBA_BLOCK = 256
MOBA_TOPK = 3
SSM_GROUP = 16
SSM_STATE = 64
RMS_EPS = 1e-6
NEG = -1e30
MASK_BIAS = -(2.0 ** 100)
MOBA_SCORES_AHEAD = 1

LANES = 128
SUBLANES = 8
MXU_DIM = 256
VMEM_LIMIT_BYTES = 60 * 1024 * 1024

SSM_CH_TILE = MXU_DIM
SSM_GROUPS_PER_TILE = SSM_CH_TILE // SSM_GROUP
SSM_STATE_TILE = SSM_GROUPS_PER_TILE * SSM_STATE
SSM_STATE_ROWS = SSM_STATE_TILE // LANES
SSM_TOKEN_PITCH = 12

ROPE_ROW_CHUNK = 8 * SUBLANES

PROMPT_ROW_TILE = 1024
IN_PROJ_COL_TILE = 1024
GLU_COL_TILE = 512
OUT_PROJ_COL_TILE = 1024
FFN_ROW_TILE = 512
FFN_HIDDEN_TILE = 1024
CAST_COL_TILE = 512
S5_TOKEN_CHUNK = 256
S5_TILES_PER_STEP = 2
MOBA_HEADS_PER_STEP = 4
DECODE_PAGES_PER_STEP = 16

F32 = jnp.float32
BF16 = jnp.bfloat16


def _cparams(*sem):
    return pltpu.CompilerParams(dimension_semantics=sem, vmem_limit_bytes=VMEM_LIMIT_BYTES)


def _sigmoid(x):
    return 0.5 * jnp.tanh(0.5 * x) + 0.5


def _gelu_tanh(x):
    c = math.sqrt(2.0 / math.pi)
    return 0.5 * x * (1.0 + jnp.tanh(c * (x + 0.044715 * (x * x * x))))


def _rms_scale(x):
    return x * lax.rsqrt(jnp.mean(x * x, axis=-1, keepdims=True) + RMS_EPS)


def _rope_table_kernel(inv_ref, cos_ref, sin_ref):
    rows = cos_ref.shape[0]
    pos = lax.broadcasted_iota(jnp.int32, (rows, LANES), 0).astype(F32)
    lane = lax.broadcasted_iota(jnp.int32, (rows, LANES), 1)
    ang = pos * inv_ref[...]
    cos_ref[...] = jnp.cos(ang)
    s = jnp.sin(ang)
    sin_ref[...] = jnp.where(lane < ROT_HALF, -s, s)


def _rope_tables(n_pos):
    rows = -(-n_pos // SUBLANES) * SUBLANES
    inv = ROPE_THETA ** (-jnp.arange(ROT_HALF, dtype=F32) / ROT_HALF)
    inv_row = jnp.concatenate([inv, inv, jnp.zeros((LANES - ROT_DIM,), F32)])[None, :]
    out = jax.ShapeDtypeStruct((rows, LANES), F32)
    return pl.pallas_call(_rope_table_kernel, out_shape=(out, out), name="rope_tables")(inv_row)


def _ssm_prep_kernel(ls_ref, lr_ref, li_ref, lrr_ref, lir_ref, br_ref, bi_ref,
                     abr_ref, abi_ref, bbr_ref, bbi_ref):
    step = jnp.exp(ls_ref[...])

    def disc(lr, li):
        mag = jnp.exp(lr * step)
        ang = li * step
        ab_re, ab_im = mag * jnp.cos(ang), mag * jnp.sin(ang)
        den = lr * lr + li * li
        nr, ni = ab_re - 1.0, ab_im
        f_re = (nr * lr + ni * li) / den
        f_im = (ni * lr - nr * li) / den
        return ab_re, ab_im, f_re, f_im

    ab_re, ab_im, _, _ = disc(lr_ref[...], li_ref[...])
    abr_ref[...] = ab_re
    abi_ref[...] = ab_im
    _, _, f_re, f_im = disc(lrr_ref[...], lir_ref[...])
    br, bi = br_ref[...], bi_ref[...]
    bbr_ref[...] = f_re * br - f_im * bi
    bbi_ref[...] = f_re * bi + f_im * br


def _ssm_prep(lambda_re, lambda_im, log_step, b_re, b_im):
    g, n = lambda_re.shape
    flat = g, n * SSM_GROUP
    outs = (jax.ShapeDtypeStruct((g, n), F32),) * 2 + (jax.ShapeDtypeStruct(flat, F32),) * 2
    return pl.pallas_call(_ssm_prep_kernel, out_shape=outs, name="ssm_prep")(
        log_step.reshape(g, 1), lambda_re, lambda_im,
        jnp.repeat(lambda_re, SSM_GROUP, axis=1), jnp.repeat(lambda_im, SSM_GROUP, axis=1),
        b_re.reshape(flat), b_im.reshape(flat))


def _block_diag_tiles(w):
    g, r, c = w.shape
    t, n = g // SSM_GROUPS_PER_TILE, SSM_GROUPS_PER_TILE
    tiled = jnp.tile(w.reshape(t, n * r, c), (1, 1, n))
    row_grp = jnp.arange(n * r, dtype=jnp.int32)[:, None] // r
    col_grp = jnp.arange(n * c, dtype=jnp.int32)[None, :] // c
    return jnp.where(row_grp == col_grp, tiled, 0.0).astype(BF16)


def _weight_tile(w_ref, w16_ref):
    if w16_ref is None:
        return w_ref[...]
    w16_ref[...] = w_ref[...].astype(BF16)
    return w16_ref[...]


def _cast_mode(w, m, tm):
    cast = w.dtype == F32
    assert not cast or m == tm
    return cast


def _in_proj_kernel(x_ref, g_ref, w_ref, cos_ref, sin_ref, qug_ref, k_ref, v_ref, *rest, ends):
    w16_ref, hn_ref = (rest[0], rest[1]) if len(rest) == 2 else (None, rest[0])
    j = pl.program_id(1)
    q_end, k_end, v_end, u_end = ends

    @pl.when(j == 0)
    def _():
        hn_ref[...] = (_rms_scale(x_ref[...]) * g_ref[...]).astype(BF16)

    def proj():
        return jnp.dot(hn_ref[...], _weight_tile(w_ref, w16_ref), preferred_element_type=F32)

    def rope_into(o_ref):
        acc = proj()
        chunk = min(ROPE_ROW_CHUNK, acc.shape[0])
        lane = lax.broadcasted_iota(jnp.int32, (chunk, HEAD_DIM), 1)
        partner = jnp.where(lane < ROT_HALF, lane + ROT_HALF,
                            jnp.where(lane < ROT_DIM, lane - ROT_HALF, lane))
        for r0 in range(0, acc.shape[0], chunk):
            rows = slice(r0, r0 + chunk)
            cos, sin = cos_ref[rows, :], sin_ref[rows, :]
            for h in range(acc.shape[1] // HEAD_DIM):
                cols = slice(h * HEAD_DIM, (h + 1) * HEAD_DIM)
                xh = acc[rows, cols]
                o_ref[rows, cols] = xh * cos + jnp.take_along_axis(xh, partner, axis=1) * sin

    @pl.when(j < q_end)
    def _():
        rope_into(qug_ref)

    @pl.when((j >= q_end) & (j < k_end))
    def _():
        rope_into(k_ref)

    @pl.when((j >= k_end) & (j < v_end))
    def _():
        v_ref[...] = proj()

    @pl.when((j >= v_end) & (j < u_end))
    def _():
        qug_ref[...] = proj()

    @pl.when(j >= u_end)
    def _():
        qug_ref[...] = _sigmoid(proj())


def _in_proj(x, gain, w, rope, rope_blocks, widths, tm, tn):
    m, d = x.shape
    assert all(wd % tn == 0 for wd in widths)
    tiles = [wd // tn for wd in widths]
    starts = [sum(tiles[:n]) for n in range(len(tiles))]

    def own_tile(n):
        return lambda i, j: (i, jnp.clip(j - starts[n], 0, tiles[n] - 1))

    def qug_tile(i, j):
        return i, jnp.where(j < starts[1], j, jnp.maximum(j - tiles[1] - tiles[2], tiles[0] - 1))

    table = pl.BlockSpec((tm, LANES), lambda i, j: (i % rope_blocks, 0))
    w_spec = pl.BlockSpec((d, tn), lambda i, j: (0, j))
    out_shape = [jax.ShapeDtypeStruct((m, widths[0] + widths[3] + widths[4]), F32),
                 jax.ShapeDtypeStruct((m, widths[1]), F32), jax.ShapeDtypeStruct((m, widths[2]), F32)]
    out_specs = [pl.BlockSpec((tm, tn), qug_tile), pl.BlockSpec((tm, tn), own_tile(1)),
                 pl.BlockSpec((tm, tn), own_tile(2))]
    if _cast_mode(w, m, tm):
        out_shape.append(jax.ShapeDtypeStruct(w.shape, BF16))
        out_specs.append(w_spec)
    x_spec = pl.BlockSpec((tm, d), lambda i, j: (i, 0), pipeline_mode=pl.Buffered(1))
    return pl.pallas_call(
        functools.partial(_in_proj_kernel, ends=tuple(starts[1:])),
        out_shape=tuple(out_shape),
        grid=(m // tm, sum(tiles)),
        in_specs=[x_spec, pl.BlockSpec((1, d), lambda i, j: (0, 0)), w_spec, table, table],
        out_specs=tuple(out_specs),
        scratch_shapes=[pltpu.VMEM((tm, d), BF16)],
        compiler_params=_cparams("parallel", "arbitrary"),
        name="in_proj",
    )(x, gain, w, *rope)


def _topk_rank(gates, i):
    gi = gates[i]
    rank = jnp.zeros(gi.shape, F32)
    for i2, g2 in enumerate(gates):
        if i2 == i:
            continue
        ahead = (g2 >= gi) if i2 < i else (g2 > gi)
        rank = rank + jnp.where(ahead, 1.0, 0.0)
    return rank


def _moba_prompt_kernel(q_ref, k_ref, v_ref, kbias_ref, o_ref):
    seq = q_ref.shape[1]
    nb = seq // MOBA_BLOCK
    heads = q_ref.shape[2] // HEAD_DIM
    row = lax.broadcasted_iota(jnp.int32, (MOBA_BLOCK, MOBA_BLOCK), 0)
    col = lax.broadcasted_iota(jnp.int32, (MOBA_BLOCK, MOBA_BLOCK), 1)
    causal = col <= row
    operands = [_moba_head_operands(q_ref, k_ref, v_ref, kbias_ref, h) for h in range(heads)]
    items = [(h, j) for j in range(nb) for h in range(heads)]

    def scores(item):
        h, j = item
        q_aug, k_aug, _ = operands[h]
        return lax.dot_general(q_aug[j * MOBA_BLOCK:(j + 1) * MOBA_BLOCK], k_aug[:(j + 1) * MOBA_BLOCK],
                               (((1,), (1,)), ((), ())), preferred_element_type=F32)

    def finish(item, s):
        h, j = item
        v_aug = operands[h][2]
        s_own = jnp.where(causal, s[:, j * MOBA_BLOCK:], NEG)
        s = jnp.concatenate([s[:, :j * MOBA_BLOCK], s_own], axis=1) if j else s_own
        p = jnp.exp2(s - jnp.max(s, axis=-1, keepdims=True))
        o = jnp.dot(p.astype(BF16), v_aug[:(j + 1) * MOBA_BLOCK], preferred_element_type=F32)
        o_ref[0, j * MOBA_BLOCK:(j + 1) * MOBA_BLOCK, h * HEAD_DIM:(h + 1) * HEAD_DIM] = (
            o[:, :HEAD_DIM] / o[:, HEAD_DIM:])

    ahead = [scores(item) for item in items[:MOBA_SCORES_AHEAD]]
    for n, item in enumerate(items):
        if n + MOBA_SCORES_AHEAD < len(items):
            ahead.append(scores(items[n + MOBA_SCORES_AHEAD]))
        finish(item, ahead.pop(0))


def _moba_head_operands(q_ref, k_ref, v_ref, kbias_ref, h):
    seq = q_ref.shape[1]
    nb = seq // MOBA_BLOCK
    cols = slice(h * HEAD_DIM, (h + 1) * HEAD_DIM)
    q, k, v = q_ref[0, :, cols], k_ref[0, :, cols], v_ref[0, :, cols]
    k_mean = jnp.concatenate(
        [jnp.mean(k[i * MOBA_BLOCK:(i + 1) * MOBA_BLOCK], axis=0, keepdims=True) for i in range(nb)],
        axis=0)
    gate_t = lax.dot_general(k_mean, q, (((1,), (1,)), ((), ())),
                             precision=lax.Precision.HIGHEST, preferred_element_type=F32)
    blk = lax.broadcasted_iota(jnp.int32, (nb, seq), 0)
    own_blk = lax.broadcasted_iota(jnp.int32, (nb, seq), 1) // MOBA_BLOCK
    past = blk < own_blk
    dropped = jnp.zeros((nb, seq), F32)
    for i in range(nb - 1):
        gi = gate_t[i:i + 1, :]
        ahead = (gate_t > gi) | ((gate_t == gi) & (blk < i))
        rank = jnp.sum(jnp.where(ahead & past, 1.0, 0.0), axis=0, keepdims=True)
        dropped = jnp.where((blk == i) & (rank >= float(MOBA_TOPK)), 1.0, dropped)
    dropped = jnp.where(past, dropped, 0.0)
    drop_cols = jnp.concatenate([dropped, jnp.zeros((HEAD_DIM - nb, seq), F32)], axis=0).T

    c = HEAD_DIM ** -0.5 * math.log2(math.e)
    q_aug = jnp.concatenate([(q * c).astype(BF16), drop_cols.astype(BF16)], axis=1)
    k_aug = jnp.concatenate([k.astype(BF16), kbias_ref[...]], axis=1)
    v_aug = jnp.concatenate([v.astype(BF16), jnp.ones((seq, HEAD_DIM), BF16)], axis=1)
    return q_aug, k_aug, v_aug


def _moba_prompt(q, k, v, heads_per_step):
    b, s, _ = q.shape
    nb = s // MOBA_BLOCK
    assert nb <= HEAD_DIM and N_HEADS % heads_per_step == 0
    spec = pl.BlockSpec((1, s, heads_per_step * HEAD_DIM), lambda bi, h: (bi, 0, h))
    key_blk = jnp.arange(s, dtype=jnp.int32)[:, None] // MOBA_BLOCK
    kbias = jnp.where(key_blk == jnp.arange(HEAD_DIM, dtype=jnp.int32)[None, :], MASK_BIAS, 0.0)
    return pl.pallas_call(
        _moba_prompt_kernel,
        out_shape=jax.ShapeDtypeStruct(k.shape, F32),
        grid=(b, N_HEADS // heads_per_step),
        in_specs=[spec, spec, spec, pl.BlockSpec((s, HEAD_DIM), lambda bi, h: (0, 0))],
        out_specs=spec,
        compiler_params=_cparams("parallel", "parallel"),
        name="moba_prompt",
    )(q, k, v, kbias.astype(BF16))


def _value_slab_copy(pt_ref, cv_ref, vbuf, sem, seq, slot, rank, head, blk, p):
    pages_per_blk = vbuf.shape[2]
    pg = pt_ref[seq, blk * pages_per_blk + p]
    return pltpu.make_async_copy(cv_ref.at[pg, :, head, :], vbuf.at[slot, rank, p, :, head, :],
                                 sem.at[slot])


def _moba_decode_kernel(pt_ref, q_ref, kn_ref, vn_ref, *refs, pages_per_step, n_seq):
    k_refs = refs[:pages_per_step]
    cv_ref, o_ref, g_s, m_s, l_s, p_s, rank_s, own_s, vbuf, sem = refs[pages_per_step:]
    b, step = pl.program_id(0), pl.program_id(1)
    last_step = step == pl.num_programs(1) - 1
    slot = lax.rem(b, 2)
    page = k_refs[0].shape[1]
    pages_per_blk = MOBA_BLOCK // page
    blks_per_step = pages_per_step // pages_per_blk
    rows_per_page = page * N_HEADS
    n_cols = pages_per_blk * rows_per_page
    nb = g_s.shape[0]
    scale = HEAD_DIM ** -0.5
    lanes = (N_HEADS, HEAD_DIM)

    @pl.when(b < n_seq)
    def _key_pass():
        q = q_ref[0]
        qb = q.astype(BF16)
        col_head = lax.broadcasted_iota(jnp.int32, (N_HEADS, n_cols), 1) % N_HEADS
        own = col_head == lax.broadcasted_iota(jnp.int32, (N_HEADS, n_cols), 0)
        for bi in range(blks_per_step):
            kp = [k_refs[bi * pages_per_blk + p][0] for p in range(pages_per_blk)]
            k_rows = jnp.concatenate([k.reshape(rows_per_page, HEAD_DIM) for k in kp], axis=0)
            k_sum = kp[0].sum(axis=0)
            for k in kp[1:]:
                k_sum = k_sum + k.sum(axis=0)
            gate = jnp.sum(q * (k_sum * (1.0 / MOBA_BLOCK)), axis=-1, keepdims=True)
            s = lax.dot_general(qb, k_rows.astype(BF16), (((1,), (1,)), ((), ())),
                                preferred_element_type=F32) * scale
            s = jnp.where(own, s, NEG)
            m = jnp.max(s, axis=-1, keepdims=True)
            p = jnp.exp(s - m)
            blk = step * blks_per_step + bi
            g_s[blk] = jnp.broadcast_to(gate, lanes)
            m_s[slot, blk] = jnp.broadcast_to(m, lanes)
            l_s[slot, blk] = jnp.broadcast_to(jnp.sum(p, axis=-1, keepdims=True), lanes)
            p_s[slot, blk] = p

    @pl.when(last_step & (b < n_seq))
    def _select_and_fetch():
        qb = q_ref[0].astype(BF16).astype(F32)
        kn = kn_ref[0].astype(BF16).astype(F32)
        s_own = jnp.sum(qb * kn, axis=-1, keepdims=True) * scale
        own_s[slot, 0] = jnp.broadcast_to(s_own, lanes)
        own_s[slot, 1] = vn_ref[0].astype(BF16).astype(F32)
        gates = [g_s[i] for i in range(nb)]
        ranks = [_topk_rank(gates, i) for i in range(nb)]
        for i in range(nb):
            rank_s[slot, i] = ranks[i]
        for r in range(MOBA_TOPK):
            blk_of_head = jnp.zeros(lanes, F32)
            for i in range(nb):
                blk_of_head = jnp.where(ranks[i] == float(r), float(i), blk_of_head)
            blk_of_head = blk_of_head.astype(jnp.int32)
            for h in range(N_HEADS):
                blk = blk_of_head[h, 0]
                for p in range(pages_per_blk):
                    _value_slab_copy(pt_ref, cv_ref, vbuf, sem, b, slot, r, h, blk, p).start()

    @pl.when(last_step & (b >= 1))
    def _merge_previous():
        prev = 1 - slot
        for r in range(MOBA_TOPK):
            for h in range(N_HEADS):
                for p in range(pages_per_blk):
                    _value_slab_copy(pt_ref, cv_ref, vbuf, sem, 0, prev, r, h, 0, p).wait()
        s_own, v_own = own_s[prev, 0], own_s[prev, 1]
        ranks = [rank_s[prev, i] for i in range(nb)]
        m_blk = [m_s[prev, i] for i in range(nb)]
        m_all = s_own
        for i in range(nb):
            m_all = jnp.maximum(m_all, jnp.where(ranks[i] < float(MOBA_TOPK), m_blk[i], NEG))
        w_own = jnp.exp(s_own - m_all)
        den = w_own
        num = w_own.astype(BF16).astype(F32) * v_own
        for r in range(MOBA_TOPK):
            w_r = jnp.zeros(lanes, F32)
            p_r = jnp.zeros((N_HEADS, n_cols), F32)
            for i in range(nb):
                mine = jnp.where(ranks[i] == float(r), 1.0, 0.0)
                w_i = jnp.where(ranks[i] == float(r), jnp.exp(m_blk[i] - m_all), 0.0)
                w_r = w_r + w_i
                den = den + w_i * l_s[prev, i]
                p_r = p_r + jnp.broadcast_to(mine[:, :1], p_r.shape) * p_s[prev, i]
            v_rows = vbuf[prev, r].reshape(n_cols, HEAD_DIM)
            o_r = jnp.dot(p_r.astype(BF16), v_rows.astype(BF16), preferred_element_type=F32)
            num = num + w_r * o_r
        o_ref[0] = num / den


def _moba_decode(q, k_new, v_new, cache_k, cache_v, page_table, pages_per_step):
    n_dec = q.shape[0]
    page = cache_k.shape[1]
    n_pages = page_table.shape[1]
    pages_per_blk = MOBA_BLOCK // page
    assert MOBA_BLOCK % page == 0 and (n_pages * page) % MOBA_BLOCK == 0
    assert pages_per_step % pages_per_blk == 0 and n_pages % pages_per_step == 0
    nb = n_pages // pages_per_blk
    assert nb >= MOBA_TOPK
    last = n_dec - 1
    row = pl.BlockSpec((1, N_HEADS, HEAD_DIM), lambda b, i, pt: (jnp.minimum(b, last), 0, 0))

    def page_spec(p):
        return pl.BlockSpec((1, page, N_HEADS, HEAD_DIM),
                            lambda b, i, pt: (pt[jnp.minimum(b, last), i * pages_per_step + p], 0, 0, 0))

    n_cols = MOBA_BLOCK * N_HEADS
    stat = (N_HEADS, HEAD_DIM)
    return pl.pallas_call(
        functools.partial(_moba_decode_kernel, pages_per_step=pages_per_step, n_seq=n_dec),
        out_shape=jax.ShapeDtypeStruct((n_dec, N_HEADS, HEAD_DIM), F32),
        grid_spec=pltpu.PrefetchScalarGridSpec(
            num_scalar_prefetch=1,
            grid=(n_dec + 1, n_pages // pages_per_step),
            in_specs=[row, row, row] + [page_spec(p) for p in range(pages_per_step)]
            + [pl.BlockSpec(memory_space=pl.ANY)],
            out_specs=pl.BlockSpec((1, N_HEADS, HEAD_DIM), lambda b, i, pt: (jnp.maximum(b - 1, 0), 0, 0)),
            scratch_shapes=[
                pltpu.VMEM((nb,) + stat, F32),
                pltpu.VMEM((2, nb) + stat, F32),
                pltpu.VMEM((2, nb) + stat, F32),
                pltpu.VMEM((2, nb, N_HEADS, n_cols), F32),
                pltpu.VMEM((2, nb) + stat, F32),
                pltpu.VMEM((2, 2) + stat, F32),
                pltpu.VMEM((2, MOBA_TOPK, pages_per_blk, page) + stat, F32),
                pltpu.SemaphoreType.DMA((2,))]),
        compiler_params=_cparams("arbitrary", "arbitrary"),
        name="moba_decode",
    )(page_table, q, k_new, v_new, *([cache_k] * pages_per_step), cache_v)


def _s5_prompt_kernel(u_ref, wbr_ref, wbi_ref, wcr_ref, wci_ref, ar_ref, ai_ref, d_ref,
                      z_ref, hr_out, hi_out, xr_s, xi_s, hr_s, hi_s):
    c = pl.program_id(1)
    n_b, t_c = u_ref.shape[0], u_ref.shape[1]
    n_k = wbr_ref.shape[0]

    @pl.when(c == 0)
    def _():
        hr_s[...] = jnp.zeros_like(hr_s)
        hi_s[...] = jnp.zeros_like(hi_s)

    u_all = u_ref[...].reshape(n_b * t_c, u_ref.shape[2])

    def u_tile(k):
        return u_all[:, k * SSM_CH_TILE:(k + 1) * SSM_CH_TILE]

    def project_in(k):
        ub = u_tile(k).astype(BF16)
        for w_ref, x_s in ((wbr_ref, xr_s), (wbi_ref, xi_s)):
            x = jnp.dot(ub, w_ref[k], preferred_element_type=F32)
            for b in range(n_b):
                for j in range(SSM_STATE_ROWS):
                    x_s[k, b, pl.ds(j, t_c, stride=SSM_TOKEN_PITCH), :] = (
                        x[b * t_c:(b + 1) * t_c, j * LANES:(j + 1) * LANES])

    def scan(k):
        ar, ai = ar_ref[k], ai_ref[k]
        h = [(hr_s[k, b], hi_s[k, b]) for b in range(n_b)]
        for t in range(t_c):
            rows = slice(t * SSM_TOKEN_PITCH, t * SSM_TOKEN_PITCH + SSM_STATE_ROWS)
            for b in range(n_b):
                hr, hi = h[b]
                nhr = ar * hr - ai * hi + xr_s[k, b, rows, :]
                nhi = ar * hi + ai * hr + xi_s[k, b, rows, :]
                xr_s[k, b, rows, :] = nhr
                xi_s[k, b, rows, :] = nhi
                h[b] = (nhr, nhi)
        for b in range(n_b):
            hr_s[k, b], hi_s[k, b] = h[b]

    def project_out(k):
        def gather_states(s_ref):
            return jnp.concatenate(
                [jnp.concatenate([s_ref[k, b, pl.ds(j, t_c, stride=SSM_TOKEN_PITCH), :].astype(BF16)
                                  for j in range(SSM_STATE_ROWS)], axis=1) for b in range(n_b)], axis=0)

        y = (jnp.dot(gather_states(xr_s), wcr_ref[k], preferred_element_type=F32)
             - jnp.dot(gather_states(xi_s), wci_ref[k], preferred_element_type=F32)
             + d_ref[:, k * SSM_CH_TILE:(k + 1) * SSM_CH_TILE] * u_tile(k))
        z = _gelu_tanh(y).astype(z_ref.dtype).reshape(n_b, t_c, SSM_CH_TILE)
        z_ref[:, :, k * SSM_CH_TILE:(k + 1) * SSM_CH_TILE] = z

    project_in(0)
    for k in range(n_k):
        if k + 1 < n_k:
            project_in(k + 1)
        scan(k)
        if k > 0:
            project_out(k - 1)
    project_out(n_k - 1)

    @pl.when(c == pl.num_programs(1) - 1)
    def _():
        for k in range(n_k):
            for b in range(n_b):
                hr_out[b, k] = hr_s[k, b]
                hi_out[b, k] = hi_s[k, b]


def _s5_prompt(u, u_col0, wb_re, wb_im, wc_re, wc_im, ab_re, ab_im, d_skip, t_c, tiles_per_step):
    n_b, seq, _ = u.shape
    width = d_skip.shape[1]
    n_t = width // SSM_CH_TILE
    n_k = tiles_per_step
    assert n_t % n_k == 0 and u_col0 % (n_k * SSM_CH_TILE) == 0
    u_blk0 = u_col0 // (n_k * SSM_CH_TILE)
    state = jax.ShapeDtypeStruct((n_b, n_t, SSM_STATE_ROWS, LANES), F32)
    wb_spec = pl.BlockSpec((n_k, SSM_CH_TILE, SSM_STATE_TILE), lambda kt, c: (kt, 0, 0))
    wc_spec = pl.BlockSpec((n_k, SSM_STATE_TILE, SSM_CH_TILE), lambda kt, c: (kt, 0, 0))
    a_spec = pl.BlockSpec((n_k, SSM_STATE_ROWS, LANES), lambda kt, c: (kt, 0, 0))
    u_spec = pl.BlockSpec((n_b, t_c, n_k * SSM_CH_TILE), lambda kt, c: (0, c, u_blk0 + kt))
    z_spec = pl.BlockSpec((n_b, t_c, n_k * SSM_CH_TILE), lambda kt, c: (0, c, kt))
    h_spec = pl.BlockSpec((n_b, n_k, SSM_STATE_ROWS, LANES), lambda kt, c: (0, kt, 0, 0))
    x_scr = pltpu.VMEM((n_k, n_b, t_c * SSM_TOKEN_PITCH, LANES), F32)
    h_scr = pltpu.VMEM((n_k, n_b, SSM_STATE_ROWS, LANES), F32)
    return pl.pallas_call(
        _s5_prompt_kernel,
        out_shape=(jax.ShapeDtypeStruct((n_b, seq, width), BF16), state, state),
        grid=(n_t // n_k, seq // t_c),
        in_specs=[u_spec, wb_spec, wb_spec, wc_spec, wc_spec, a_spec, a_spec,
                  pl.BlockSpec((1, n_k * SSM_CH_TILE), lambda kt, c: (0, kt))],
        out_specs=(z_spec, h_spec, h_spec),
        scratch_shapes=[x_scr, x_scr, h_scr, h_scr],
        compiler_params=_cparams("parallel", "arbitrary"),
        name="s5_prompt",
    )(u, wb_re, wb_im, wc_re, wc_im, ab_re, ab_im, d_skip)


def _s5_step_kernel(u_ref, h0r_ref, h0i_ref, wbr_ref, wbi_ref, wcr_ref, wci_ref, ar_ref, ai_ref,
                    d_ref, z_ref, hr_out, hi_out):
    u = u_ref[...]
    ub = u.astype(BF16)
    ar, ai = ar_ref[...], ai_ref[...]
    h0r, h0i = h0r_ref[...], h0i_ref[...]
    hr = jnp.dot(ub, wbr_ref[0], preferred_element_type=F32) + (ar * h0r - ai * h0i)
    hi = jnp.dot(ub, wbi_ref[0], preferred_element_type=F32) + (ar * h0i + ai * h0r)
    hr_out[...] = hr
    hi_out[...] = hi
    y = (jnp.dot(hr.astype(BF16), wcr_ref[0], preferred_element_type=F32)
         - jnp.dot(hi.astype(BF16), wci_ref[0], preferred_element_type=F32)
         + d_ref[...] * u)
    z_ref[...] = _gelu_tanh(y).astype(z_ref.dtype)


def _s5_step(u, u_col0, h0_re, h0_im, wb_re, wb_im, wc_re, wc_im, ab_re, ab_im, d_skip):
    n_seq = u.shape[0]
    width = d_skip.shape[1]
    n_t = width // SSM_CH_TILE
    assert u_col0 % SSM_CH_TILE == 0
    u_blk0 = u_col0 // SSM_CH_TILE
    n_state = h0_re.shape[1]
    state = jax.ShapeDtypeStruct((n_seq, n_state), F32)
    u_spec = pl.BlockSpec((n_seq, SSM_CH_TILE), lambda kt: (0, u_blk0 + kt))
    z_spec = pl.BlockSpec((n_seq, SSM_CH_TILE), lambda kt: (0, kt))
    h_spec = pl.BlockSpec((n_seq, SSM_STATE_TILE), lambda kt: (0, kt))
    wb_spec = pl.BlockSpec((1, SSM_CH_TILE, SSM_STATE_TILE), lambda kt: (kt, 0, 0))
    wc_spec = pl.BlockSpec((1, SSM_STATE_TILE, SSM_CH_TILE), lambda kt: (kt, 0, 0))
    a_spec = pl.BlockSpec((1, SSM_STATE_TILE), lambda kt: (0, kt))
    return pl.pallas_call(
        _s5_step_kernel,
        out_shape=(jax.ShapeDtypeStruct((n_seq, width), BF16), state, state),
        grid=(n_t,),
        in_specs=[u_spec, h_spec, h_spec, wb_spec, wb_spec, wc_spec, wc_spec, a_spec, a_spec,
                  pl.BlockSpec((1, SSM_CH_TILE), lambda kt: (0, kt))],
        out_specs=(z_spec, h_spec, h_spec),
        compiler_params=_cparams("parallel"),
        name="s5_step",
    )(u, h0_re, h0_im, wb_re, wb_im, wc_re, wc_im, ab_re, ab_im, d_skip)


def _glu_mix_kernel(z_ref, attn_ref, ga_ref, gs_ref, wv_ref, wg_ref, o_ref, wv16_ref=None, wg16_ref=None):
    z = z_ref[...]
    val = jnp.dot(z, _weight_tile(wv_ref, wv16_ref), preferred_element_type=F32)
    gat = jnp.dot(z, _weight_tile(wg_ref, wg16_ref), preferred_element_type=F32)
    mix = ga_ref[...] * attn_ref[...] + gs_ref[...] * (val * _sigmoid(gat))
    o_ref[...] = mix.astype(o_ref.dtype)


def _glu_mix(z, attn, gates, gate_col0, w_glu_v, w_glu_g, tm, tn):
    m, d = z.shape
    assert gate_col0 % tn == 0
    ga_off = gate_col0 // tn
    gs_off = ga_off + d // tn
    col = pl.BlockSpec((tm, tn), lambda i, j: (i, j))
    w_spec = pl.BlockSpec((d, tn), lambda i, j: (0, j))
    out_shape, out_specs = [jax.ShapeDtypeStruct((m, d), BF16)], [col]
    if _cast_mode(w_glu_v, m, tm):
        out_shape += [jax.ShapeDtypeStruct(w_glu_v.shape, BF16)] * 2
        out_specs += [w_spec, w_spec]
    return pl.pallas_call(
        _glu_mix_kernel,
        out_shape=tuple(out_shape),
        grid=(m // tm, d // tn),
        in_specs=[pl.BlockSpec((tm, d), lambda i, j: (i, 0)), col,
                  pl.BlockSpec((tm, tn), lambda i, j: (i, ga_off + j)),
                  pl.BlockSpec((tm, tn), lambda i, j: (i, gs_off + j)), w_spec, w_spec],
        out_specs=tuple(out_specs),
        compiler_params=_cparams("parallel", "arbitrary"),
        name="glu_mix",
    )(z, attn, gates, gates, w_glu_v, w_glu_g)


def _out_proj_kernel(mix_ref, x_ref, w_ref, o_ref, w16_ref=None):
    o_ref[...] = x_ref[...] + jnp.dot(mix_ref[...], _weight_tile(w_ref, w16_ref),
                                      preferred_element_type=F32)


def _out_proj(mix, x, w_out, tm, tn):
    m, d = x.shape
    col = pl.BlockSpec((tm, tn), lambda i, j: (i, j))
    w_spec = pl.BlockSpec((d, tn), lambda i, j: (0, j))
    out_shape, out_specs = [jax.ShapeDtypeStruct((m, d), F32)], [col]
    if _cast_mode(w_out, m, tm):
        out_shape.append(jax.ShapeDtypeStruct(w_out.shape, BF16))
        out_specs.append(w_spec)
    return pl.pallas_call(
        _out_proj_kernel,
        out_shape=tuple(out_shape),
        grid=(m // tm, d // tn),
        in_specs=[pl.BlockSpec((tm, d), lambda i, j: (i, 0)), col, w_spec],
        out_specs=tuple(out_specs),
        compiler_params=_cparams("parallel", "arbitrary"),
        name="out_proj",
    )(mix, x, w_out)


def _ffn_kernel(x_ref, g_ref, gf_ref, wu_ref, wd_ref, o_ref, *rest):
    wu16_ref, wd16_ref, hn_ref = rest if len(rest) == 3 else (None, None, rest[0])
    f = pl.program_id(1)

    @pl.when(f == 0)
    def _():
        x = x_ref[...]
        hn_ref[...] = (_rms_scale(x) * g_ref[...]).astype(BF16)
        o_ref[...] = x

    up = jnp.dot(hn_ref[...], _weight_tile(wu_ref, wu16_ref), preferred_element_type=F32)
    act = jnp.square(jnp.maximum(up, 0.0))
    o_ref[...] += jnp.dot(act.astype(BF16), _weight_tile(wd_ref, wd16_ref),
                          preferred_element_type=F32)

    @pl.when(f == pl.num_programs(1) - 1)
    def _():
        o_ref[...] = _rms_scale(o_ref[...]) * gf_ref[...]


def _ffn(x, norm_ffn, norm_final, w_up, w_down, tm, tf):
    m, d = x.shape
    d_ff = w_up.shape[1]
    row = pl.BlockSpec((tm, d), lambda i, f: (i, 0))
    vec = pl.BlockSpec((1, d), lambda i, f: (0, 0))
    up_spec = pl.BlockSpec((d, tf), lambda i, f: (0, f))
    down_spec = pl.BlockSpec((tf, d), lambda i, f: (f, 0))
    out_shape, out_specs = [jax.ShapeDtypeStruct((m, d), F32)], [row]
    if _cast_mode(w_up, m, tm):
        out_shape += [jax.ShapeDtypeStruct(w_up.shape, BF16), jax.ShapeDtypeStruct(w_down.shape, BF16)]
        out_specs += [up_spec, down_spec]
    return pl.pallas_call(
        _ffn_kernel,
        out_shape=tuple(out_shape),
        grid=(m // tm, d_ff // tf),
        in_specs=[row, vec, vec, up_spec, down_spec],
        out_specs=tuple(out_specs),
        scratch_shapes=[pltpu.VMEM((tm, d), BF16)],
        compiler_params=_cparams("parallel", "arbitrary"),
        name="ffn",
    )(x, norm_ffn, norm_final, w_up, w_down)


def kernel(x_prompt, x_sample, cache_k, cache_v, state_ssm_re, state_ssm_im, page_table, norm_mix, w_in, lambda_re, lambda_im, log_step, b_re, b_im, c_re, c_im, d_skip, w_glu_v, w_glu_g, w_out, norm_ffn, w_up, w_down, norm_final):
    depth = w_in.shape[0]
    assert depth == 1, "single trunk layer"
    n_b, seq, d = x_prompt.shape
    n_dec, s_dec, _ = x_sample.shape
    assert s_dec == 1
    width = N_HEADS * HEAD_DIM
    n_pool, page = cache_k.shape[1], cache_k.shape[2]
    past_len = page_table.shape[1] * page
    n_groups = lambda_re.shape[1]

    rope_p = _rope_tables(max(seq, past_len + s_dec))
    ab_re, ab_im, bb_re, bb_im = _ssm_prep(lambda_re[0], lambda_im[0], log_step[0], b_re[0], b_im[0])
    to_in = lambda bb: _block_diag_tiles(
        bb.reshape(n_groups, SSM_STATE, SSM_GROUP).transpose(0, 2, 1))
    wb_re, wb_im = to_in(bb_re), to_in(bb_im)
    to_out = lambda cc: _block_diag_tiles(cc.transpose(0, 2, 1))
    wc_re, wc_im = to_out(c_re[0]), to_out(c_im[0])
    n_t = d // SSM_CH_TILE
    ab_re_t = ab_re.reshape(n_t, SSM_STATE_ROWS, LANES)
    ab_im_t = ab_im.reshape(n_t, SSM_STATE_ROWS, LANES)

    g_mix, g_ffn, g_fin = norm_mix.reshape(1, d), norm_ffn.reshape(1, d), norm_final.reshape(1, d)
    d_row = d_skip.reshape(1, d)
    drop_depth = lambda w: w.reshape(w.shape[1:])
    in_widths = (width, width, width, d, 2 * d)

    xs = x_sample.reshape(n_dec, d)
    rope_s = tuple(jnp.broadcast_to(t[past_len:past_len + 1], (n_dec, LANES)) for t in rope_p)
    u_col0, gate_col0 = width, width + d
    qug_s, ks, vs, w_in16 = _in_proj(xs, g_mix, drop_depth(w_in), rope_s, 1, in_widths,
                                     tm=n_dec, tn=IN_PROJ_COL_TILE)
    heads = (n_dec, N_HEADS, HEAD_DIM)
    cache_shape = (n_pool, page, N_HEADS, HEAD_DIM)
    attn_s = _moba_decode(qug_s[:, :width].reshape(heads), ks.reshape(heads), vs.reshape(heads),
                          cache_k.reshape(cache_shape), cache_v.reshape(cache_shape), page_table,
                          pages_per_step=min(DECODE_PAGES_PER_STEP, page_table.shape[1]))
    n_state = n_groups * SSM_STATE
    zs, hsr, hsi = _s5_step(qug_s, u_col0, state_ssm_re.reshape(n_dec, n_state),
                            state_ssm_im.reshape(n_dec, n_state),
                            wb_re, wb_im, wc_re, wc_im,
                            ab_re.reshape(1, n_state), ab_im.reshape(1, n_state), d_row)
    mix_s, w_v16, w_g16 = _glu_mix(zs, attn_s.reshape(n_dec, width), qug_s, gate_col0,
                                   drop_depth(w_glu_v), drop_depth(w_glu_g),
                                   tm=n_dec, tn=CAST_COL_TILE)
    x1_s, w_o16 = _out_proj(mix_s, xs, drop_depth(w_out), tm=n_dec, tn=CAST_COL_TILE)
    y_sample, w_up16, w_dn16 = _ffn(x1_s, g_ffn, g_fin, drop_depth(w_up), drop_depth(w_down),
                                    tm=n_dec, tf=CAST_COL_TILE)

    tm_p = PROMPT_ROW_TILE
    xp = x_prompt.reshape(n_b * seq, d)
    qug_p, kp, vp = _in_proj(xp, g_mix, w_in16, rope_p, seq // tm_p, in_widths,
                             tm=tm_p, tn=IN_PROJ_COL_TILE)
    qug_p3 = qug_p.reshape(n_b, seq, qug_p.shape[1])
    attn_p = _moba_prompt(qug_p3, kp.reshape(n_b, seq, width), vp.reshape(n_b, seq, width),
                          heads_per_step=MOBA_HEADS_PER_STEP)
    zp, hpr, hpi = _s5_prompt(qug_p3, u_col0, wb_re, wb_im, wc_re, wc_im, ab_re_t, ab_im_t, d_row,
                              t_c=S5_TOKEN_CHUNK, tiles_per_step=S5_TILES_PER_STEP)
    (mix_p,) = _glu_mix(zp.reshape(n_b * seq, d), attn_p.reshape(n_b * seq, width), qug_p, gate_col0,
                        w_v16, w_g16, tm=tm_p, tn=GLU_COL_TILE)
    (x1_p,) = _out_proj(mix_p, xp, w_o16, tm=tm_p, tn=OUT_PROJ_COL_TILE)
    (y_prompt,) = _ffn(x1_p, g_ffn, g_fin, w_up16, w_dn16, tm=FFN_ROW_TILE, tf=FFN_HIDDEN_TILE)

    kv_p = (1, n_b, seq, N_HEADS, HEAD_DIM)
    kv_s = (1, n_dec, s_dec, N_HEADS, HEAD_DIM)
    st_p = (1, n_b, n_groups, SSM_STATE)
    st_s = (1, n_dec, n_groups, SSM_STATE)
    return (y_prompt.reshape(n_b, seq, d), y_sample.reshape(n_dec, s_dec, d),
            kp.reshape(kv_p), vp.reshape(kv_p), hpr.reshape(st_p), hpi.reshape(st_p),
            ks.reshape(kv_s), vs.reshape(kv_s), hsr.reshape(st_s), hsi.reshape(st_s))
```

```python
import functools
import math

import jax
import jax.numpy as jnp
from jax import lax
from jax.experimental import pallas as pl
from jax.experimental.pallas import tpu as pltpu

N_HEADS = 16
HEAD_DIM = 128
ROT_DIM = HEAD_DIM // 4
ROT_HALF = ROT_DIM // 2
ROPE_THETA = 500000.0
MOBA_BLOCK = 256
MOBA_TOPK = 3
SSM_GROUP = 16
SSM_STATE = 64
RMS_EPS = 1e-6
NEG = -1e30
MASK_BIAS = -(2.0 ** 100)
MOBA_SCORES_AHEAD = 1

LANES = 128
SUBLANES = 8
MXU_DIM = 256
VMEM_LIMIT_BYTES = 60 * 1024 * 1024

SSM_CH_TILE = MXU_DIM
SSM_GROUPS_PER_TILE = SSM_CH_TILE // SSM_GROUP
SSM_STATE_TILE = SSM_GROUPS_PER_TILE * SSM_STATE
SSM_STATE_ROWS = SSM_STATE_TILE // LANES
SSM_TOKEN_PITCH = 12

ROPE_ROW_CHUNK = 8 * SUBLANES

PROMPT_ROW_TILE = 1024
IN_PROJ_COL_TILE = 1024
GLU_COL_TILE = 512
OUT_PROJ_COL_TILE = 1024
FFN_ROW_TILE = 512
FFN_HIDDEN_TILE = 1024
CAST_COL_TILE = 512
S5_TOKEN_CHUNK = 256
S5_TILES_PER_STEP = 2
MOBA_HEADS_PER_STEP = 4
DECODE_PAGES_PER_STEP = 16
VALUE_DMA_PRIORITY = 1

F32 = jnp.float32
BF16 = jnp.bfloat16


def _cparams(*sem):
    return pltpu.CompilerParams(dimension_semantics=sem, vmem_limit_bytes=VMEM_LIMIT_BYTES)


def _sigmoid(x):
    return 0.5 * jnp.tanh(0.5 * x) + 0.5


def _gelu_tanh(x):
    c = math.sqrt(2.0 / math.pi)
    return 0.5 * x * (1.0 + jnp.tanh(c * (x + 0.044715 * (x * x * x))))


def _rms_scale(x):
    return x * lax.rsqrt(jnp.mean(x * x, axis=-1, keepdims=True) + RMS_EPS)


def _rope_table_kernel(inv_ref, cos_ref, sin_ref):
    rows = cos_ref.shape[0]
    pos = lax.broadcasted_iota(jnp.int32, (rows, LANES), 0).astype(F32)
    lane = lax.broadcasted_iota(jnp.int32, (rows, LANES), 1)
    ang = pos * inv_ref[...]
    cos_ref[...] = jnp.cos(ang)
    s = jnp.sin(ang)
    sin_ref[...] = jnp.where(lane < ROT_HALF, -s, s)


def _rope_tables(n_pos):
    rows = -(-n_pos // SUBLANES) * SUBLANES
    inv = ROPE_THETA ** (-jnp.arange(ROT_HALF, dtype=F32) / ROT_HALF)
    inv_row = jnp.concatenate([inv, inv, jnp.zeros((LANES - ROT_DIM,), F32)])[None, :]
    out = jax.ShapeDtypeStruct((rows, LANES), F32)
    return pl.pallas_call(_rope_table_kernel, out_shape=(out, out), name="rope_tables")(inv_row)


def _ssm_prep_kernel(ls_ref, lr_ref, li_ref, lrr_ref, lir_ref, br_ref, bi_ref,
                     abr_ref, abi_ref, bbr_ref, bbi_ref):
    step = jnp.exp(ls_ref[...])

    def disc(lr, li):
        mag = jnp.exp(lr * step)
        ang = li * step
        ab_re, ab_im = mag * jnp.cos(ang), mag * jnp.sin(ang)
        den = lr * lr + li * li
        nr, ni = ab_re - 1.0, ab_im
        f_re = (nr * lr + ni * li) / den
        f_im = (ni * lr - nr * li) / den
        return ab_re, ab_im, f_re, f_im

    ab_re, ab_im, _, _ = disc(lr_ref[...], li_ref[...])
    abr_ref[...] = ab_re
    abi_ref[...] = ab_im
    _, _, f_re, f_im = disc(lrr_ref[...], lir_ref[...])
    br, bi = br_ref[...], bi_ref[...]
    bbr_ref[...] = f_re * br - f_im * bi
    bbi_ref[...] = f_re * bi + f_im * br


def _ssm_prep(lambda_re, lambda_im, log_step, b_re, b_im):
    g, n = lambda_re.shape
    flat = g, n * SSM_GROUP
    outs = (jax.ShapeDtypeStruct((g, n), F32),) * 2 + (jax.ShapeDtypeStruct(flat, F32),) * 2
    return pl.pallas_call(_ssm_prep_kernel, out_shape=outs, name="ssm_prep")(
        log_step.reshape(g, 1), lambda_re, lambda_im,
        jnp.repeat(lambda_re, SSM_GROUP, axis=1), jnp.repeat(lambda_im, SSM_GROUP, axis=1),
        b_re.reshape(flat), b_im.reshape(flat))


def _block_diag_tiles(w):
    g, r, c = w.shape
    t, n = g // SSM_GROUPS_PER_TILE, SSM_GROUPS_PER_TILE
    tiled = jnp.tile(w.reshape(t, n * r, c), (1, 1, n))
    row_grp = jnp.arange(n * r, dtype=jnp.int32)[:, None] // r
    col_grp = jnp.arange(n * c, dtype=jnp.int32)[None, :] // c
    return jnp.where(row_grp == col_grp, tiled, 0.0).astype(BF16)


def _weight_tile(w_ref, w16_ref):
    if w16_ref is None:
        return w_ref[...]
    w16_ref[...] = w_ref[...].astype(BF16)
    return w16_ref[...]


def _cast_mode(w, m, tm):
    cast = w.dtype == F32
    assert not cast or m == tm
    return cast


def _in_proj_kernel(x_ref, g_ref, w_ref, cos_ref, sin_ref, qug_ref, k_ref, v_ref, *rest, ends):
    w16_ref, hn_ref = (rest[0], rest[1]) if len(rest) == 2 else (None, rest[0])
    j = pl.program_id(1)
    q_end, k_end, v_end, u_end = ends

    @pl.when(j == 0)
    def _():
        hn_ref[...] = (_rms_scale(x_ref[...]) * g_ref[...]).astype(BF16)

    def proj():
        return jnp.dot(hn_ref[...], _weight_tile(w_ref, w16_ref), preferred_element_type=F32)

    def rope_into(o_ref):
        acc = proj()
        chunk = min(ROPE_ROW_CHUNK, acc.shape[0])
        lane = lax.broadcasted_iota(jnp.int32, (chunk, HEAD_DIM), 1)
        partner = jnp.where(lane < ROT_HALF, lane + ROT_HALF,
                            jnp.where(lane < ROT_DIM, lane - ROT_HALF, lane))
        for r0 in range(0, acc.shape[0], chunk):
            rows = slice(r0, r0 + chunk)
            cos, sin = cos_ref[rows, :], sin_ref[rows, :]
            for h in range(acc.shape[1] // HEAD_DIM):
                cols = slice(h * HEAD_DIM, (h + 1) * HEAD_DIM)
                xh = acc[rows, cols]
                o_ref[rows, cols] = xh * cos + jnp.take_along_axis(xh, partner, axis=1) * sin

    @pl.when(j < q_end)
    def _():
        rope_into(qug_ref)

    @pl.when((j >= q_end) & (j < k_end))
    def _():
        rope_into(k_ref)

    @pl.when((j >= k_end) & (j < v_end))
    def _():
        v_ref[...] = proj()

    @pl.when((j >= v_end) & (j < u_end))
    def _():
        qug_ref[...] = proj()

    @pl.when(j >= u_end)
    def _():
        qug_ref[...] = _sigmoid(proj())


def _in_proj(x, gain, w, rope, rope_blocks, widths, tm, tn):
    m, d = x.shape
    assert all(wd % tn == 0 for wd in widths)
    tiles = [wd // tn for wd in widths]
    starts = [sum(tiles[:n]) for n in range(len(tiles))]

    def own_tile(n):
        return lambda i, j: (i, jnp.clip(j - starts[n], 0, tiles[n] - 1))

    def qug_tile(i, j):
        return i, jnp.where(j < starts[1], j, jnp.maximum(j - tiles[1] - tiles[2], tiles[0] - 1))

    table = pl.BlockSpec((tm, LANES), lambda i, j: (i % rope_blocks, 0))
    w_spec = pl.BlockSpec((d, tn), lambda i, j: (0, j))
    out_shape = [jax.ShapeDtypeStruct((m, widths[0] + widths[3] + widths[4]), F32),
                 jax.ShapeDtypeStruct((m, widths[1]), F32), jax.ShapeDtypeStruct((m, widths[2]), F32)]
    out_specs = [pl.BlockSpec((tm, tn), qug_tile), pl.BlockSpec((tm, tn), own_tile(1)),
                 pl.BlockSpec((tm, tn), own_tile(2))]
    if _cast_mode(w, m, tm):
        out_shape.append(jax.ShapeDtypeStruct(w.shape, BF16))
        out_specs.append(w_spec)
    x_spec = pl.BlockSpec((tm, d), lambda i, j: (i, 0), pipeline_mode=pl.Buffered(1))
    return pl.pallas_call(
        functools.partial(_in_proj_kernel, ends=tuple(starts[1:])),
        out_shape=tuple(out_shape),
        grid=(m // tm, sum(tiles)),
        in_specs=[x_spec, pl.BlockSpec((1, d), lambda i, j: (0, 0)), w_spec, table, table],
        out_specs=tuple(out_specs),
        scratch_shapes=[pltpu.VMEM((tm, d), BF16)],
        compiler_params=_cparams("parallel", "arbitrary"),
        name="in_proj",
    )(x, gain, w, *rope)


def _topk_rank(gates, i):
    gi = gates[i]
    rank = jnp.zeros(gi.shape, F32)
    for i2, g2 in enumerate(gates):
        if i2 == i:
            continue
        ahead = (g2 >= gi) if i2 < i else (g2 > gi)
        rank = rank + jnp.where(ahead, 1.0, 0.0)
    return rank


def _moba_prompt_kernel(q_ref, k_ref, v_ref, kbias_ref, o_ref):
    seq = q_ref.shape[1]
    nb = seq // MOBA_BLOCK
    heads = q_ref.shape[2] // HEAD_DIM
    row = lax.broadcasted_iota(jnp.int32, (MOBA_BLOCK, MOBA_BLOCK), 0)
    col = lax.broadcasted_iota(jnp.int32, (MOBA_BLOCK, MOBA_BLOCK), 1)
    causal = col <= row
    operands = [_moba_head_operands(q_ref, k_ref, v_ref, kbias_ref, h) for h in range(heads)]
    items = [(h, j) for j in range(nb) for h in range(heads)]

    def scores(item):
        h, j = item
        q_aug, k_aug, _ = operands[h]
        return lax.dot_general(q_aug[j * MOBA_BLOCK:(j + 1) * MOBA_BLOCK], k_aug[:(j + 1) * MOBA_BLOCK],
                               (((1,), (1,)), ((), ())), preferred_element_type=F32)

    def finish(item, s):
        h, j = item
        v_aug = operands[h][2]
        s_own = jnp.where(causal, s[:, j * MOBA_BLOCK:], NEG)
        s = jnp.concatenate([s[:, :j * MOBA_BLOCK], s_own], axis=1) if j else s_own
        p = jnp.exp2(s - jnp.max(s, axis=-1, keepdims=True))
        o = jnp.dot(p.astype(BF16), v_aug[:(j + 1) * MOBA_BLOCK], preferred_element_type=F32)
        o_ref[0, j * MOBA_BLOCK:(j + 1) * MOBA_BLOCK, h * HEAD_DIM:(h + 1) * HEAD_DIM] = (
            o[:, :HEAD_DIM] / o[:, HEAD_DIM:])

    ahead = [scores(item) for item in items[:MOBA_SCORES_AHEAD]]
    for n, item in enumerate(items):
        if n + MOBA_SCORES_AHEAD < len(items):
            ahead.append(scores(items[n + MOBA_SCORES_AHEAD]))
        finish(item, ahead.pop(0))


def _moba_head_operands(q_ref, k_ref, v_ref, kbias_ref, h):
    seq = q_ref.shape[1]
    nb = seq // MOBA_BLOCK
    cols = slice(h * HEAD_DIM, (h + 1) * HEAD_DIM)
    q, k, v = q_ref[0, :, cols], k_ref[0, :, cols], v_ref[0, :, cols]
    k_mean = jnp.concatenate(
        [jnp.mean(k[i * MOBA_BLOCK:(i + 1) * MOBA_BLOCK], axis=0, keepdims=True) for i in range(nb)],
        axis=0)
    gate_t = lax.dot_general(k_mean, q, (((1,), (1,)), ((), ())),
                             precision=lax.Precision.HIGHEST, preferred_element_type=F32)
    blk = lax.broadcasted_iota(jnp.int32, (nb, seq), 0)
    own_blk = lax.broadcasted_iota(jnp.int32, (nb, seq), 1) // MOBA_BLOCK
    past = blk < own_blk
    dropped = jnp.zeros((nb, seq), F32)
    for i in range(nb - 1):
        gi = gate_t[i:i + 1, :]
        ahead = (gate_t > gi) | ((gate_t == gi) & (blk < i))
        rank = jnp.sum(jnp.where(ahead & past, 1.0, 0.0), axis=0, keepdims=True)
        dropped = jnp.where((blk == i) & (rank >= float(MOBA_TOPK)), 1.0, dropped)
    dropped = jnp.where(past, dropped, 0.0)
    drop_cols = jnp.concatenate([dropped, jnp.zeros((HEAD_DIM - nb, seq), F32)], axis=0).T

    c = HEAD_DIM ** -0.5 * math.log2(math.e)
    q_aug = jnp.concatenate([(q * c).astype(BF16), drop_cols.astype(BF16)], axis=1)
    k_aug = jnp.concatenate([k.astype(BF16), kbias_ref[...]], axis=1)
    v_aug = jnp.concatenate([v.astype(BF16), jnp.ones((seq, HEAD_DIM), BF16)], axis=1)
    return q_aug, k_aug, v_aug


def _moba_prompt(q, k, v, heads_per_step):
    b, s, _ = q.shape
    nb = s // MOBA_BLOCK
    assert nb <= HEAD_DIM and N_HEADS % heads_per_step == 0
    spec = pl.BlockSpec((1, s, heads_per_step * HEAD_DIM), lambda bi, h: (bi, 0, h))
    key_blk = jnp.arange(s, dtype=jnp.int32)[:, None] // MOBA_BLOCK
    kbias = jnp.where(key_blk == jnp.arange(HEAD_DIM, dtype=jnp.int32)[None, :], MASK_BIAS, 0.0)
    return pl.pallas_call(
        _moba_prompt_kernel,
        out_shape=jax.ShapeDtypeStruct(k.shape, F32),
        grid=(b, N_HEADS // heads_per_step),
        in_specs=[spec, spec, spec, pl.BlockSpec((s, HEAD_DIM), lambda bi, h: (0, 0))],
        out_specs=spec,
        compiler_params=_cparams("parallel", "parallel"),
        name="moba_prompt",
    )(q, k, v, kbias.astype(BF16))


def _value_slab_copy(pt_ref, cv_ref, vbuf, sem, seq, slot, rank, head, blk, p):
    pages_per_blk = vbuf.shape[2]
    pg = pt_ref[seq, blk * pages_per_blk + p]
    return pltpu.make_async_copy(cv_ref.at[pg, :, head, :], vbuf.at[slot, rank, p, :, head, :],
                                 sem.at[slot])


def _moba_decode_kernel(pt_ref, q_ref, kn_ref, vn_ref, *refs, pages_per_step, n_seq):
    k_refs = refs[:pages_per_step]
    cv_ref, o_ref, g_s, m_s, l_s, p_s, rank_s, own_s, vbuf, sem = refs[pages_per_step:]
    b, step = pl.program_id(0), pl.program_id(1)
    last_step = step == pl.num_programs(1) - 1
    slot = lax.rem(b, 2)
    page = k_refs[0].shape[1]
    pages_per_blk = MOBA_BLOCK // page
    blks_per_step = pages_per_step // pages_per_blk
    rows_per_page = page * N_HEADS
    n_cols = pages_per_blk * rows_per_page
    nb = g_s.shape[0]
    scale = HEAD_DIM ** -0.5
    lanes = (N_HEADS, HEAD_DIM)

    @pl.when(b < n_seq)
    def _key_pass():
        q = q_ref[0]
        qb = q.astype(BF16)
        col_head = lax.broadcasted_iota(jnp.int32, (N_HEADS, n_cols), 1) % N_HEADS
        own = col_head == lax.broadcasted_iota(jnp.int32, (N_HEADS, n_cols), 0)
        for bi in range(blks_per_step):
            kp = [k_refs[bi * pages_per_blk + p][0] for p in range(pages_per_blk)]
            k_rows = jnp.concatenate([k.reshape(rows_per_page, HEAD_DIM) for k in kp], axis=0)
            k_sum = kp[0].sum(axis=0)
            for k in kp[1:]:
                k_sum = k_sum + k.sum(axis=0)
            gate = jnp.sum(q * (k_sum * (1.0 / MOBA_BLOCK)), axis=-1, keepdims=True)
            s = lax.dot_general(qb, k_rows.astype(BF16), (((1,), (1,)), ((), ())),
                                preferred_element_type=F32) * scale
            s = jnp.where(own, s, NEG)
            m = jnp.max(s, axis=-1, keepdims=True)
            p = jnp.exp(s - m)
            blk = step * blks_per_step + bi
            g_s[blk] = jnp.broadcast_to(gate, lanes)
            m_s[slot, blk] = jnp.broadcast_to(m, lanes)
            l_s[slot, blk] = jnp.broadcast_to(jnp.sum(p, axis=-1, keepdims=True), lanes)
            p_s[slot, blk] = p

    @pl.when(last_step & (b < n_seq))
    def _select_and_fetch():
        qb = q_ref[0].astype(BF16).astype(F32)
        kn = kn_ref[0].astype(BF16).astype(F32)
        s_own = jnp.sum(qb * kn, axis=-1, keepdims=True) * scale
        own_s[slot, 0] = jnp.broadcast_to(s_own, lanes)
        own_s[slot, 1] = vn_ref[0].astype(BF16).astype(F32)
        gates = [g_s[i] for i in range(nb)]
        ranks = [_topk_rank(gates, i) for i in range(nb)]
        for i in range(nb):
            rank_s[slot, i] = ranks[i]
        for r in range(MOBA_TOPK):
            blk_of_head = jnp.zeros(lanes, F32)
            for i in range(nb):
                blk_of_head = jnp.where(ranks[i] == float(r), float(i), blk_of_head)
            blk_of_head = blk_of_head.astype(jnp.int32)
            for h in range(N_HEADS):
                blk = blk_of_head[h, 0]
                for p in range(pages_per_blk):
                    _value_slab_copy(pt_ref, cv_ref, vbuf, sem, b, slot, r, h, blk, p).start(
                        priority=VALUE_DMA_PRIORITY)

    @pl.when(last_step & (b >= 1))
    def _merge_previous():
        prev = 1 - slot
        for r in range(MOBA_TOPK):
            for h in range(N_HEADS):
                for p in range(pages_per_blk):
                    _value_slab_copy(pt_ref, cv_ref, vbuf, sem, 0, prev, r, h, 0, p).wait()
        s_own, v_own = own_s[prev, 0], own_s[prev, 1]
        ranks = [rank_s[prev, i] for i in range(nb)]
        m_blk = [m_s[prev, i] for i in range(nb)]
        m_all = s_own
        for i in range(nb):
            m_all = jnp.maximum(m_all, jnp.where(ranks[i] < float(MOBA_TOPK), m_blk[i], NEG))
        w_own = jnp.exp(s_own - m_all)
        den = w_own
        num = w_own.astype(BF16).astype(F32) * v_own
        for r in range(MOBA_TOPK):
            w_r = jnp.zeros(lanes, F32)
            p_r = jnp.zeros((N_HEADS, n_cols), F32)
            for i in range(nb):
                mine = jnp.where(ranks[i] == float(r), 1.0, 0.0)
                w_i = jnp.where(ranks[i] == float(r), jnp.exp(m_blk[i] - m_all), 0.0)
                w_r = w_r + w_i
                den = den + w_i * l_s[prev, i]
                p_r = p_r + jnp.broadcast_to(mine[:, :1], p_r.shape) * p_s[prev, i]
            v_rows = vbuf[prev, r].reshape(n_cols, HEAD_DIM)
            o_r = jnp.dot(p_r.astype(BF16), v_rows.astype(BF16), preferred_element_type=F32)
            num = num + w_r * o_r
        o_ref[0] = num / den


def _moba_decode(q, k_new, v_new, cache_k, cache_v, page_table, pages_per_step):
    n_dec = q.shape[0]
    page = cache_k.shape[1]
    n_pages = page_table.shape[1]
    pages_per_blk = MOBA_BLOCK // page
    assert MOBA_BLOCK % page == 0 and (n_pages * page) % MOBA_BLOCK == 0
    assert pages_per_step % pages_per_blk == 0 and n_pages % pages_per_step == 0
    nb = n_pages // pages_per_blk
    assert nb >= MOBA_TOPK
    last = n_dec - 1
    row = pl.BlockSpec((1, N_HEADS, HEAD_DIM), lambda b, i, pt: (jnp.minimum(b, last), 0, 0))

    def page_spec(p):
        return pl.BlockSpec((1, page, N_HEADS, HEAD_DIM),
                            lambda b, i, pt: (pt[jnp.minimum(b, last), i * pages_per_step + p], 0, 0, 0))

    n_cols = MOBA_BLOCK * N_HEADS
    stat = (N_HEADS, HEAD_DIM)
    return pl.pallas_call(
        functools.partial(_moba_decode_kernel, pages_per_step=pages_per_step, n_seq=n_dec),
        out_shape=jax.ShapeDtypeStruct((n_dec, N_HEADS, HEAD_DIM), F32),
        grid_spec=pltpu.PrefetchScalarGridSpec(
            num_scalar_prefetch=1,
            grid=(n_dec + 1, n_pages // pages_per_step),
            in_specs=[row, row, row] + [page_spec(p) for p in range(pages_per_step)]
            + [pl.BlockSpec(memory_space=pl.ANY)],
            out_specs=pl.BlockSpec((1, N_HEADS, HEAD_DIM), lambda b, i, pt: (jnp.maximum(b - 1, 0), 0, 0)),
            scratch_shapes=[
                pltpu.VMEM((nb,) + stat, F32),
                pltpu.VMEM((2, nb) + stat, F32),
                pltpu.VMEM((2, nb) + stat, F32),
                pltpu.VMEM((2, nb, N_HEADS, n_cols), F32),
                pltpu.VMEM((2, nb) + stat, F32),
                pltpu.VMEM((2, 2) + stat, F32),
                pltpu.VMEM((2, MOBA_TOPK, pages_per_blk, page) + stat, F32),
                pltpu.SemaphoreType.DMA((2,))]),
        compiler_params=_cparams("arbitrary", "arbitrary"),
        name="moba_decode",
    )(page_table, q, k_new, v_new, *([cache_k] * pages_per_step), cache_v)


def _s5_prompt_kernel(u_ref, wbr_ref, wbi_ref, wcr_ref, wci_ref, ar_ref, ai_ref, d_ref,
                      z_ref, hr_out, hi_out, xr_s, xi_s, hr_s, hi_s):
    c = pl.program_id(1)
    n_b, t_c = u_ref.shape[0], u_ref.shape[1]
    n_k = wbr_ref.shape[0]

    @pl.when(c == 0)
    def _():
        hr_s[...] = jnp.zeros_like(hr_s)
        hi_s[...] = jnp.zeros_like(hi_s)

    u_all = u_ref[...].reshape(n_b * t_c, u_ref.shape[2])

    def u_tile(k):
        return u_all[:, k * SSM_CH_TILE:(k + 1) * SSM_CH_TILE]

    def project_in(k):
        ub = u_tile(k).astype(BF16)
        for w_ref, x_s in ((wbr_ref, xr_s), (wbi_ref, xi_s)):
            x = jnp.dot(ub, w_ref[k], preferred_element_type=F32)
            for b in range(n_b):
                for j in range(SSM_STATE_ROWS):
                    x_s[k, b, pl.ds(j, t_c, stride=SSM_TOKEN_PITCH), :] = (
                        x[b * t_c:(b + 1) * t_c, j * LANES:(j + 1) * LANES])

    def scan(k):
        ar, ai = ar_ref[k], ai_ref[k]
        h = [(hr_s[k, b], hi_s[k, b]) for b in range(n_b)]
        for t in range(t_c):
            rows = slice(t * SSM_TOKEN_PITCH, t * SSM_TOKEN_PITCH + SSM_STATE_ROWS)
            for b in range(n_b):
                hr, hi = h[b]
                nhr = ar * hr - ai * hi + xr_s[k, b, rows, :]
                nhi = ar * hi + ai * hr + xi_s[k, b, rows, :]
                xr_s[k, b, rows, :] = nhr
                xi_s[k, b, rows, :] = nhi
                h[b] = (nhr, nhi)
        for b in range(n_b):
            hr_s[k, b], hi_s[k, b] = h[b]

    def project_out(k):
        def gather_states(s_ref):
            return jnp.concatenate(
                [jnp.concatenate([s_ref[k, b, pl.ds(j, t_c, stride=SSM_TOKEN_PITCH), :].astype(BF16)
                                  for j in range(SSM_STATE_ROWS)], axis=1) for b in range(n_b)], axis=0)

        y = (jnp.dot(gather_states(xr_s), wcr_ref[k], preferred_element_type=F32)
             - jnp.dot(gather_states(xi_s), wci_ref[k], preferred_element_type=F32)
             + d_ref[:, k * SSM_CH_TILE:(k + 1) * SSM_CH_TILE] * u_tile(k))
        z = _gelu_tanh(y).astype(z_ref.dtype).reshape(n_b, t_c, SSM_CH_TILE)
        z_ref[:, :, k * SSM_CH_TILE:(k + 1) * SSM_CH_TILE] = z

    project_in(0)
    for k in range(n_k):
        if k + 1 < n_k:
            project_in(k + 1)
        scan(k)
        if k > 0:
            project_out(k - 1)
    project_out(n_k - 1)

    @pl.when(c == pl.num_programs(1) - 1)
    def _():
        for k in range(n_k):
            for b in range(n_b):
                hr_out[b, k] = hr_s[k, b]
                hi_out[b, k] = hi_s[k, b]


def _s5_prompt(u, u_col0, wb_re, wb_im, wc_re, wc_im, ab_re, ab_im, d_skip, t_c, tiles_per_step):
    n_b, seq, _ = u.shape
    width = d_skip.shape[1]
    n_t = width // SSM_CH_TILE
    n_k = tiles_per_step
    assert n_t % n_k == 0 and u_col0 % (n_k * SSM_CH_TILE) == 0
    u_blk0 = u_col0 // (n_k * SSM_CH_TILE)
    state = jax.ShapeDtypeStruct((n_b, n_t, SSM_STATE_ROWS, LANES), F32)
    wb_spec = pl.BlockSpec((n_k, SSM_CH_TILE, SSM_STATE_TILE), lambda kt, c: (kt, 0, 0))
    wc_spec = pl.BlockSpec((n_k, SSM_STATE_TILE, SSM_CH_TILE), lambda kt, c: (kt, 0, 0))
    a_spec = pl.BlockSpec((n_k, SSM_STATE_ROWS, LANES), lambda kt, c: (kt, 0, 0))
    u_spec = pl.BlockSpec((n_b, t_c, n_k * SSM_CH_TILE), lambda kt, c: (0, c, u_blk0 + kt))
    z_spec = pl.BlockSpec((n_b, t_c, n_k * SSM_CH_TILE), lambda kt, c: (0, c, kt))
    h_spec = pl.BlockSpec((n_b, n_k, SSM_STATE_ROWS, LANES), lambda kt, c: (0, kt, 0, 0))
    x_scr = pltpu.VMEM((n_k, n_b, t_c * SSM_TOKEN_PITCH, LANES), F32)
    h_scr = pltpu.VMEM((n_k, n_b, SSM_STATE_ROWS, LANES), F32)
    return pl.pallas_call(
        _s5_prompt_kernel,
        out_shape=(jax.ShapeDtypeStruct((n_b, seq, width), BF16), state, state),
        grid=(n_t // n_k, seq // t_c),
        in_specs=[u_spec, wb_spec, wb_spec, wc_spec, wc_spec, a_spec, a_spec,
                  pl.BlockSpec((1, n_k * SSM_CH_TILE), lambda kt, c: (0, kt))],
        out_specs=(z_spec, h_spec, h_spec),
        scratch_shapes=[x_scr, x_scr, h_scr, h_scr],
        compiler_params=_cparams("parallel", "arbitrary"),
        name="s5_prompt",
    )(u, wb_re, wb_im, wc_re, wc_im, ab_re, ab_im, d_skip)


def _s5_step_kernel(u_ref, h0r_ref, h0i_ref, wbr_ref, wbi_ref, wcr_ref, wci_ref, ar_ref, ai_ref,
                    d_ref, z_ref, hr_out, hi_out):
    u = u_ref[...]
    ub = u.astype(BF16)
    ar, ai = ar_ref[...], ai_ref[...]
    h0r, h0i = h0r_ref[...], h0i_ref[...]
    hr = jnp.dot(ub, wbr_ref[0], preferred_element_type=F32) + (ar * h0r - ai * h0i)
    hi = jnp.dot(ub, wbi_ref[0], preferred_element_type=F32) + (ar * h0i + ai * h0r)
    hr_out[...] = hr
    hi_out[...] = hi
    y = (jnp.dot(hr.astype(BF16), wcr_ref[0], preferred_element_type=F32)
         - jnp.dot(hi.astype(BF16), wci_ref[0], preferred_element_type=F32)
         + d_ref[...] * u)
    z_ref[...] = _gelu_tanh(y).astype(z_ref.dtype)


def _s5_step(u, u_col0, h0_re, h0_im, wb_re, wb_im, wc_re, wc_im, ab_re, ab_im, d_skip):
    n_seq = u.shape[0]
    width = d_skip.shape[1]
    n_t = width // SSM_CH_TILE
    assert u_col0 % SSM_CH_TILE == 0
    u_blk0 = u_col0 // SSM_CH_TILE
    n_state = h0_re.shape[1]
    state = jax.ShapeDtypeStruct((n_seq, n_state), F32)
    u_spec = pl.BlockSpec((n_seq, SSM_CH_TILE), lambda kt: (0, u_blk0 + kt))
    z_spec = pl.BlockSpec((n_seq, SSM_CH_TILE), lambda kt: (0, kt))
    h_spec = pl.BlockSpec((n_seq, SSM_STATE_TILE), lambda kt: (0, kt))
    wb_spec = pl.BlockSpec((1, SSM_CH_TILE, SSM_STATE_TILE), lambda kt: (kt, 0, 0))
    wc_spec = pl.BlockSpec((1, SSM_STATE_TILE, SSM_CH_TILE), lambda kt: (kt, 0, 0))
    a_spec = pl.BlockSpec((1, SSM_STATE_TILE), lambda kt: (0, kt))
    return pl.pallas_call(
        _s5_step_kernel,
        out_shape=(jax.ShapeDtypeStruct((n_seq, width), BF16), state, state),
        grid=(n_t,),
        in_specs=[u_spec, h_spec, h_spec, wb_spec, wb_spec, wc_spec, wc_spec, a_spec, a_spec,
                  pl.BlockSpec((1, SSM_CH_TILE), lambda kt: (0, kt))],
        out_specs=(z_spec, h_spec, h_spec),
        compiler_params=_cparams("parallel"),
        name="s5_step",
    )(u, h0_re, h0_im, wb_re, wb_im, wc_re, wc_im, ab_re, ab_im, d_skip)


def _glu_mix_kernel(z_ref, attn_ref, ga_ref, gs_ref, wv_ref, wg_ref, o_ref, wv16_ref=None, wg16_ref=None):
    z = z_ref[...]
    val = jnp.dot(z, _weight_tile(wv_ref, wv16_ref), preferred_element_type=F32)
    gat = jnp.dot(z, _weight_tile(wg_ref, wg16_ref), preferred_element_type=F32)
    mix = ga_ref[...] * attn_ref[...] + gs_ref[...] * (val * _sigmoid(gat))
    o_ref[...] = mix.astype(o_ref.dtype)


def _glu_mix(z, attn, gates, gate_col0, w_glu_v, w_glu_g, tm, tn):
    m, d = z.shape
    assert gate_col0 % tn == 0
    ga_off = gate_col0 // tn
    gs_off = ga_off + d // tn
    col = pl.BlockSpec((tm, tn), lambda i, j: (i, j))
    w_spec = pl.BlockSpec((d, tn), lambda i, j: (0, j))
    out_shape, out_specs = [jax.ShapeDtypeStruct((m, d), BF16)], [col]
    if _cast_mode(w_glu_v, m, tm):
        out_shape += [jax.ShapeDtypeStruct(w_glu_v.shape, BF16)] * 2
        out_specs += [w_spec, w_spec]
    return pl.pallas_call(
        _glu_mix_kernel,
        out_shape=tuple(out_shape),
        grid=(m // tm, d // tn),
        in_specs=[pl.BlockSpec((tm, d), lambda i, j: (i, 0)), col,
                  pl.BlockSpec((tm, tn), lambda i, j: (i, ga_off + j)),
                  pl.BlockSpec((tm, tn), lambda i, j: (i, gs_off + j)), w_spec, w_spec],
        out_specs=tuple(out_specs),
        compiler_params=_cparams("parallel", "arbitrary"),
        name="glu_mix",
    )(z, attn, gates, gates, w_glu_v, w_glu_g)


def _out_proj_kernel(mix_ref, x_ref, w_ref, o_ref, w16_ref=None):
    o_ref[...] = x_ref[...] + jnp.dot(mix_ref[...], _weight_tile(w_ref, w16_ref),
                                      preferred_element_type=F32)


def _out_proj(mix, x, w_out, tm, tn):
    m, d = x.shape
    col = pl.BlockSpec((tm, tn), lambda i, j: (i, j))
    w_spec = pl.BlockSpec((d, tn), lambda i, j: (0, j))
    out_shape, out_specs = [jax.ShapeDtypeStruct((m, d), F32)], [col]
    if _cast_mode(w_out, m, tm):
        out_shape.append(jax.ShapeDtypeStruct(w_out.shape, BF16))
        out_specs.append(w_spec)
    return pl.pallas_call(
        _out_proj_kernel,
        out_shape=tuple(out_shape),
        grid=(m // tm, d // tn),
        in_specs=[pl.BlockSpec((tm, d), lambda i, j: (i, 0)), col, w_spec],
        out_specs=tuple(out_specs),
        compiler_params=_cparams("parallel", "arbitrary"),
        name="out_proj",
    )(mix, x, w_out)


def _ffn_kernel(x_ref, g_ref, gf_ref, wu_ref, wd_ref, o_ref, *rest):
    wu16_ref, wd16_ref, hn_ref = rest if len(rest) == 3 else (None, None, rest[0])
    f = pl.program_id(1)

    @pl.when(f == 0)
    def _():
        x = x_ref[...]
        hn_ref[...] = (_rms_scale(x) * g_ref[...]).astype(BF16)
        o_ref[...] = x

    up = jnp.dot(hn_ref[...], _weight_tile(wu_ref, wu16_ref), preferred_element_type=F32)
    act = jnp.square(jnp.maximum(up, 0.0))
    o_ref[...] += jnp.dot(act.astype(BF16), _weight_tile(wd_ref, wd16_ref),
                          preferred_element_type=F32)

    @pl.when(f == pl.num_programs(1) - 1)
    def _():
        o_ref[...] = _rms_scale(o_ref[...]) * gf_ref[...]


def _ffn(x, norm_ffn, norm_final, w_up, w_down, tm, tf):
    m, d = x.shape
    d_ff = w_up.shape[1]
    row = pl.BlockSpec((tm, d), lambda i, f: (i, 0))
    vec = pl.BlockSpec((1, d), lambda i, f: (0, 0))
    up_spec = pl.BlockSpec((d, tf), lambda i, f: (0, f))
    down_spec = pl.BlockSpec((tf, d), lambda i, f: (f, 0))
    out_shape, out_specs = [jax.ShapeDtypeStruct((m, d), F32)], [row]
    if _cast_mode(w_up, m, tm):
        out_shape += [jax.ShapeDtypeStruct(w_up.shape, BF16), jax.ShapeDtypeStruct(w_down.shape, BF16)]
        out_specs += [up_spec, down_spec]
    return pl.pallas_call(
        _ffn_kernel,
        out_shape=tuple(out_shape),
        grid=(m // tm, d_ff // tf),
        in_specs=[row, vec, vec, up_spec, down_spec],
        out_specs=tuple(out_specs),
        scratch_shapes=[pltpu.VMEM((tm, d), BF16)],
        compiler_params=_cparams("parallel", "arbitrary"),
        name="ffn",
    )(x, norm_ffn, norm_final, w_up, w_down)


def kernel(x_prompt, x_sample, cache_k, cache_v, state_ssm_re, state_ssm_im, page_table, norm_mix, w_in, lambda_re, lambda_im, log_step, b_re, b_im, c_re, c_im, d_skip, w_glu_v, w_glu_g, w_out, norm_ffn, w_up, w_down, norm_final):
    depth = w_in.shape[0]
    assert depth == 1, "single trunk layer"
    n_b, seq, d = x_prompt.shape
    n_dec, s_dec, _ = x_sample.shape
    assert s_dec == 1
    width = N_HEADS * HEAD_DIM
    n_pool, page = cache_k.shape[1], cache_k.shape[2]
    past_len = page_table.shape[1] * page
    n_groups = lambda_re.shape[1]

    rope_p = _rope_tables(max(seq, past_len + s_dec))
    ab_re, ab_im, bb_re, bb_im = _ssm_prep(lambda_re[0], lambda_im[0], log_step[0], b_re[0], b_im[0])
    to_in = lambda bb: _block_diag_tiles(
        bb.reshape(n_groups, SSM_STATE, SSM_GROUP).transpose(0, 2, 1))
    wb_re, wb_im = to_in(bb_re), to_in(bb_im)
    to_out = lambda cc: _block_diag_tiles(cc.transpose(0, 2, 1))
    wc_re, wc_im = to_out(c_re[0]), to_out(c_im[0])
    n_t = d // SSM_CH_TILE
    ab_re_t = ab_re.reshape(n_t, SSM_STATE_ROWS, LANES)
    ab_im_t = ab_im.reshape(n_t, SSM_STATE_ROWS, LANES)

    g_mix, g_ffn, g_fin = norm_mix.reshape(1, d), norm_ffn.reshape(1, d), norm_final.reshape(1, d)
    d_row = d_skip.reshape(1, d)
    drop_depth = lambda w: w.reshape(w.shape[1:])
    in_widths = (width, width, width, d, 2 * d)

    xs = x_sample.reshape(n_dec, d)
    rope_s = tuple(jnp.broadcast_to(t[past_len:past_len + 1], (n_dec, LANES)) for t in rope_p)
    u_col0, gate_col0 = width, width + d
    qug_s, ks, vs, w_in16 = _in_proj(xs, g_mix, drop_depth(w_in), rope_s, 1, in_widths,
                                     tm=n_dec, tn=IN_PROJ_COL_TILE)
    heads = (n_dec, N_HEADS, HEAD_DIM)
    cache_shape = (n_pool, page, N_HEADS, HEAD_DIM)
    attn_s = _moba_decode(qug_s[:, :width].reshape(heads), ks.reshape(heads), vs.reshape(heads),
                          cache_k.reshape(cache_shape), cache_v.reshape(cache_shape), page_table,
                          pages_per_step=min(DECODE_PAGES_PER_STEP, page_table.shape[1]))
    n_state = n_groups * SSM_STATE
    zs, hsr, hsi = _s5_step(qug_s, u_col0, state_ssm_re.reshape(n_dec, n_state),
                            state_ssm_im.reshape(n_dec, n_state),
                            wb_re, wb_im, wc_re, wc_im,
                            ab_re.reshape(1, n_state), ab_im.reshape(1, n_state), d_row)
    mix_s, w_v16, w_g16 = _glu_mix(zs, attn_s.reshape(n_dec, width), qug_s, gate_col0,
                                   drop_depth(w_glu_v), drop_depth(w_glu_g),
                                   tm=n_dec, tn=CAST_COL_TILE)
    x1_s, w_o16 = _out_proj(mix_s, xs, drop_depth(w_out), tm=n_dec, tn=CAST_COL_TILE)
    y_sample, w_up16, w_dn16 = _ffn(x1_s, g_ffn, g_fin, drop_depth(w_up), drop_depth(w_down),
                                    tm=n_dec, tf=CAST_COL_TILE)

    tm_p = PROMPT_ROW_TILE
    xp = x_prompt.reshape(n_b * seq, d)
    qug_p, kp, vp = _in_proj(xp, g_mix, w_in16, rope_p, seq // tm_p, in_widths,
                             tm=tm_p, tn=IN_PROJ_COL_TILE)
    qug_p3 = qug_p.reshape(n_b, seq, qug_p.shape[1])
    attn_p = _moba_prompt(qug_p3, kp.reshape(n_b, seq, width), vp.reshape(n_b, seq, width),
                          heads_per_step=MOBA_HEADS_PER_STEP)
    zp, hpr, hpi = _s5_prompt(qug_p3, u_col0, wb_re, wb_im, wc_re, wc_im, ab_re_t, ab_im_t, d_row,
                              t_c=S5_TOKEN_CHUNK, tiles_per_step=S5_TILES_PER_STEP)
    (mix_p,) = _glu_mix(zp.reshape(n_b * seq, d), attn_p.reshape(n_b * seq, width), qug_p, gate_col0,
                        w_v16, w_g16, tm=tm_p, tn=GLU_COL_TILE)
    (x1_p,) = _out_proj(mix_p, xp, w_o16, tm=tm_p, tn=OUT_PROJ_COL_TILE)
    (y_prompt,) = _ffn(x1_p, g_ffn, g_fin, w_up16, w_dn16, tm=FFN_ROW_TILE, tf=FFN_HIDDEN_TILE)

    kv_p = (1, n_b, seq, N_HEADS, HEAD_DIM)
    kv_s = (1, n_dec, s_dec, N_HEADS, HEAD_DIM)
    st_p = (1, n_b, n_groups, SSM_STATE)
    st_s = (1, n_dec, n_groups, SSM_STATE)
    return (y_prompt.reshape(n_b, seq, d), y_sample.reshape(n_dec, s_dec, d),
            kp.reshape(kv_p), vp.reshape(kv_p), hpr.reshape(st_p), hpi.reshape(st_p),
            ks.reshape(kv_s), vs.reshape(kv_s), hsr.reshape(st_s), hsi.reshape(st_s))
```
